```python
import jax, jax.numpy as jnp
from jax import lax
import numpy as np

D_MODEL = 1024
BATCH = 32
SEQ = 2048
DEPTH = 1

GRID_W = 64
N_HEADS = 16
HEAD_DIM = D_MODEL // N_HEADS
ATTN_WIDTH = N_HEADS * HEAD_DIM
WIN_ROWS = 8
WIN_COLS = 16
Q_COLS = 16
K_COLS = 32
N_CBLK = GRID_W // Q_COLS
LRU_WIDTH = D_MODEL
LRU_BLOCKS = 16
LRU_BLOCK = LRU_WIDTH // LRU_BLOCKS
LRU_C = 8.0
CONV_W = 4
CONV_LEFT = 2
D_FF = 2816
FFN_RES = 0.5
EPS = 1e-6
IN_WIDTH = 2 * LRU_WIDTH + 3 * ATTN_WIDTH + 2 * D_MODEL

kernel_name = 'hybrid_rglru_natten_macaron_block'


def _rmsnorm(x, g):
    xf = x.astype(jnp.float32)
    y = xf * lax.rsqrt(jnp.mean(xf * xf, axis=-1, keepdims=True) + EPS)
    return (y * g.astype(jnp.float32)).astype(x.dtype)


def _swiglu(h, w_gu, w_down):
    g, u = jnp.split(h @ w_gu, 2, axis=-1)
    return (jax.nn.silu(g) * u) @ w_down


def _centred_dwconv(x, w, b):
    S = x.shape[1]
    xp = jnp.pad(x, ((0, 0), (CONV_LEFT, CONV_W - 1 - CONV_LEFT), (0, 0)))
    y = b
    for k in range(CONV_W):
        y = y + xp[:, k:k + S] * w[k]
    return y


def _lin_combine(left, right):
    a1, b1 = left
    a2, b2 = right
    return a1 * a2, a2 * b1 + b2


def _rglru_scan(xc, w_gates, b_gates, lam):
    B_, S, W = xc.shape
    xb = xc.reshape(B_, S, LRU_BLOCKS, LRU_BLOCK)
    gates = jnp.einsum('bsni,gnio->gbsno', xb, w_gates.astype(jnp.float32)).reshape(2, B_, S, W)
    gates = gates + b_gates.astype(jnp.float32)[:, None, None, :]
    r = jax.nn.sigmoid(gates[0])
    i = jax.nn.sigmoid(gates[1])
    log_a = -LRU_C * r * jax.nn.softplus(-lam.astype(jnp.float32))
    a = jnp.exp(log_a)
    bx = jnp.sqrt(-jnp.expm1(2.0 * log_a)) * (i * xc)
    _, h = lax.associative_scan(_lin_combine, (a, bx), axis=1)
    return h


def _col_tables():
    qc = np.arange(GRID_W).reshape(N_CBLK, Q_COLS)
    kstart = np.clip(np.arange(N_CBLK) * Q_COLS - WIN_COLS // 2, 0, GRID_W - K_COLS)
    kc = kstart[:, None] + np.arange(K_COLS)[None, :]
    cs = np.clip(qc - WIN_COLS // 2, 0, GRID_W - WIN_COLS)
    valid = (kc[:, None, :] >= cs[:, :, None]) & (kc[:, None, :] < cs[:, :, None] + WIN_COLS)
    dc = np.clip(kc[:, None, :] - qc[:, :, None], -(WIN_COLS - 1), WIN_COLS - 1) + WIN_COLS - 1
    return kc, valid, dc


def _neighbourhood_attention(q, k, v, rpb):
    B_, S, H, Dh = q.shape
    rows = S // GRID_W
    kr = min(WIN_ROWS, rows)
    kc, valid, dc = _col_tables()
    kg = k.reshape(B_, rows, GRID_W, H, Dh)[:, :, kc]
    vg = v.reshape(B_, rows, GRID_W, H, Dh)[:, :, kc]
    qg = jnp.moveaxis(q.reshape(B_, rows, N_CBLK, Q_COLS, H, Dh), 1, 0)
    rpb32 = rpb.astype(jnp.float32)
    mask = valid[None, None, :, :, None, :]

    def row_block(args):
        r, q_r = args
        rs = jnp.clip(r - kr // 2, 0, rows - kr)
        k_r = lax.dynamic_slice_in_dim(kg, rs, kr, axis=1)
        v_r = lax.dynamic_slice_in_dim(vg, rs, kr, axis=1)
        s = jnp.einsum('bnqhd,brnkhd->bhnqrk', q_r, k_r, preferred_element_type=jnp.float32)
        dr = rs + jnp.arange(kr) - r + WIN_ROWS - 1
        bias = rpb32[:, dr][:, :, dc]
        s = s + jnp.transpose(bias, (0, 2, 3, 1, 4))[None]
        s = jnp.where(mask, s, -jnp.inf)
        shp = s.shape
        p = jax.nn.softmax(s.reshape(shp[:4] + (kr * K_COLS,)), axis=-1).reshape(shp)
        return jnp.einsum('bhnqrk,brnkhd->bnqhd', p.astype(v.dtype), v_r)

    o = lax.map(row_block, (jnp.arange(rows), qg))
    return jnp.moveaxis(o, 0, 1).reshape(B_, S, H * Dh)


def setup_inputs(seed: int = 0) -> dict:
    key = jax.random.key(seed)
    ks = jax.random.split(key, 20)
    f32 = jnp.float32
    nrm = lambda k, shape, s: jax.random.normal(k, shape, f32) * s
    u = jax.random.uniform(ks[10], (DEPTH, 2, LRU_WIDTH), f32, minval=0.9, maxval=0.999)
    sg = u ** (1.0 / LRU_C)
    lru_lambda = jnp.log(sg) - jnp.log1p(-sg)
    return {
        'x': jax.random.normal(ks[0], (BATCH, SEQ, D_MODEL), f32),
        'norm_ffn1': 1.0 + nrm(ks[1], (DEPTH, D_MODEL), 0.02),
        'w_ffn1_gu': nrm(ks[2], (DEPTH, D_MODEL, 2 * D_FF), D_MODEL ** -0.5),
        'w_ffn1_down': nrm(ks[3], (DEPTH, D_FF, D_MODEL), D_FF ** -0.5),
        'norm_mix': 1.0 + nrm(ks[4], (DEPTH, D_MODEL), 0.02),
        'w_in': nrm(ks[5], (DEPTH, D_MODEL, IN_WIDTH), D_MODEL ** -0.5),
        'conv_w': nrm(ks[6], (DEPTH, CONV_W, LRU_WIDTH), CONV_W ** -0.5),
        'conv_b': nrm(ks[7], (DEPTH, LRU_WIDTH), 0.01),
        'lru_w_gates': nrm(ks[8], (DEPTH, 2, 2, LRU_BLOCKS, LRU_BLOCK, LRU_BLOCK), LRU_BLOCK ** -0.5),
        'lru_b_gates': nrm(ks[9], (DEPTH, 2, 2, LRU_WIDTH), 0.1),
        'lru_lambda': lru_lambda,
        'q_norm': 1.0 + nrm(ks[11], (DEPTH, HEAD_DIM), 0.02),
        'k_norm': 1.0 + nrm(ks[12], (DEPTH, HEAD_DIM), 0.02),
        'rel_pos_bias': nrm(ks[13], (DEPTH, N_HEADS, 2 * WIN_ROWS - 1, 2 * WIN_COLS - 1), 0.1),
        'w_out': nrm(ks[14], (DEPTH, D_MODEL, D_MODEL), D_MODEL ** -0.5),
        'norm_ffn2': 1.0 + nrm(ks[15], (DEPTH, D_MODEL), 0.02),
        'w_ffn2_gu': nrm(ks[16], (DEPTH, D_MODEL, 2 * D_FF), D_MODEL ** -0.5),
        'w_ffn2_down': nrm(ks[17], (DEPTH, D_FF, D_MODEL), D_FF ** -0.5),
    }


def reference(x, norm_ffn1, w_ffn1_gu, w_ffn1_down, norm_mix, w_in, conv_w, conv_b,
              lru_w_gates, lru_b_gates, lru_lambda, q_norm, k_norm, rel_pos_bias, w_out,
              norm_ffn2, w_ffn2_gu, w_ffn2_down):
    B_, S, _ = x.shape
    splits = list(np.cumsum([LRU_WIDTH, LRU_WIDTH, ATTN_WIDTH, ATTN_WIDTH, ATTN_WIDTH, D_MODEL]))
    for l in range(DEPTH):
        x = x + FFN_RES * _swiglu(_rmsnorm(x, norm_ffn1[l]), w_ffn1_gu[l], w_ffn1_down[l])

        h = _rmsnorm(x, norm_mix[l])
        xr, gr, q, k, v, ga, gb = jnp.split(h @ w_in[l], splits, axis=-1)

        xc = _centred_dwconv(xr, conv_w[l], conv_b[l]).astype(jnp.float32)
        h_fwd = _rglru_scan(xc, lru_w_gates[l, 0], lru_b_gates[l, 0], lru_lambda[l, 0])
        h_bwd = jnp.flip(_rglru_scan(jnp.flip(xc, 1), lru_w_gates[l, 1], lru_b_gates[l, 1],
                                     lru_lambda[l, 1]), 1)
        y_lru = ((h_fwd + h_bwd) * jax.nn.gelu(gr.astype(jnp.float32))).astype(x.dtype)

        q = _rmsnorm(q.reshape(B_, S, N_HEADS, HEAD_DIM), q_norm[l]) * (HEAD_DIM ** -0.5)
        k = _rmsnorm(k.reshape(B_, S, N_HEADS, HEAD_DIM), k_norm[l])
        v = v.reshape(B_, S, N_HEADS, HEAD_DIM)
        y_att = _neighbourhood_attention(q, k, v, rel_pos_bias[l]).astype(x.dtype)

        y = jax.nn.sigmoid(ga) * y_lru + jax.nn.sigmoid(gb) * y_att
        x = x + y @ w_out[l]

        x = x + FFN_RES * _swiglu(_rmsnorm(x, norm_ffn2[l]), w_ffn2_gu[l], w_ffn2_down[l])
    return x
```

```python
import functools

import jax
import jax.numpy as jnp
import numpy as np
from jax import lax
from jax.experimental import pallas as pl
from jax.experimental.pallas import tpu as pltpu

D = 1024
B = 32
S = 2048
T = B * S
GRID_W = 64
ROWS = S // GRID_W
N_HEADS = 16
HEAD_DIM = 64
WIN_ROWS = 8
WIN_COLS = 16
LRU_BLOCKS = 16
LRU_BLOCK = 64
LRU_C = 8.0
CONV_W = 4
D_FF = 2816
FFN_RES = 0.5
EPS = 1e-6
NEG = -1e30

SUBLANES = 8
MXU_N = 256
VMEM_LIMIT = 56 * 1024 * 1024

TM_FFN = 512
TM_OUT = 256
TM_PROJ = 512
FF_CHUNK = MXU_N
LRU_SC = 256
LRU_WC = 256
LRU_SUB = 32
HG = 4
HW = HG * HEAD_DIM
QR = 4
QC = 16
KR = QR + WIN_ROWS - 1
KC = 32
NKEY = KR * KC
N_CB = GRID_W // QC
N_RG = ROWS // QR

f32 = jnp.float32
bf16 = jnp.bfloat16


def _sigmoid(x):
  return 0.5 * jnp.tanh(0.5 * x) + 0.5


def _softplus(x):
  return jnp.maximum(x, 0.0) + jnp.log1p(jnp.exp(-jnp.abs(x)))


def _gelu_tanh(x):
  c = float(np.sqrt(2.0 / np.pi))
  return 0.5 * x * (1.0 + jnp.tanh(c * (x + 0.044715 * (x * x * x))))


def _rmsnorm(x, g):
  ms = jnp.mean(x * x, axis=-1, keepdims=True)
  return (x * lax.rsqrt(ms + EPS)) * g


def _dot(a, b):
  return jnp.dot(a, b, preferred_element_type=f32)


def _swiglu(x, g_ref, wgu_ref, wdn_ref, act_ref):
  h = _rmsnorm(x, g_ref[...]).astype(bf16)
  for c in range(D_FF // FF_CHUNK):
    lo = c * FF_CHUNK
    gate = _dot(h, wgu_ref[:, lo:lo + FF_CHUNK])
    up = _dot(h, wgu_ref[:, D_FF + lo:D_FF + lo + FF_CHUNK])
    act_ref[:, lo:lo + FF_CHUNK] = (gate * _sigmoid(gate) * up).astype(bf16)
  return _dot(act_ref[...], wdn_ref[...])


def _ffn1_kernel(x_ref, g_ref, wgu_ref, wdn_ref, o_ref, act_ref):
  x = x_ref[...]
  o_ref[...] = x + FFN_RES * _swiglu(x, g_ref, wgu_ref, wdn_ref, act_ref)


def _const_spec(shape):
  nd = len(shape)
  return pl.BlockSpec(shape, lambda *_: (0,) * nd, pipeline_mode=pl.Buffered(1))


def _ffn1(x2d, g, wgu, wdn):
  nj = S // TM_FFN
  return pl.pallas_call(
      _ffn1_kernel,
      grid=(B, nj),
      in_specs=[
          pl.BlockSpec((TM_FFN, D), lambda b, j: (b * nj + j, 0)),
          _const_spec((1, D)),
          _const_spec((D, 2 * D_FF)),
          _const_spec((D_FF, D)),
      ],
      out_specs=pl.BlockSpec((TM_FFN, D), lambda b, j: (j, b)),
      out_shape=jax.ShapeDtypeStruct((S, B * D), f32),
      scratch_shapes=[pltpu.VMEM((TM_FFN, D_FF), bf16)],
      compiler_params=pltpu.CompilerParams(
          dimension_semantics=("parallel", "parallel"), vmem_limit_bytes=VMEM_LIMIT),
      name="ffn1",
  )(x2d, g, wgu, wdn)


def _head_rms(x, ones_ref):
  parts = []
  for j in range(D // MXU_N):
    sq = x[:, j * MXU_N:(j + 1) * MXU_N]
    sq = sq * sq
    hi = sq.astype(bf16)
    lo = (sq - hi.astype(f32)).astype(bf16)
    parts.append(_dot(hi, ones_ref[...]) + _dot(lo, ones_ref[...]))
  return jnp.concatenate(parts, axis=-1) * (1.0 / HEAD_DIM)


def _inproj_kernel(x_ref, g_ref, w_ref, qn_ref, kn_ref, ones_ref,
                   xr_ref, g1_ref, g2_ref, q_ref, k_ref, v_ref):
  h = _rmsnorm(x_ref[...], g_ref[...]).astype(bf16)

  def col(i):
    return _dot(h, w_ref[:, i * D:(i + 1) * D])

  xr_ref[...] = col(0)
  gr = col(1)
  ga = col(5)
  g1_ref[...] = _gelu_tanh(gr) * _sigmoid(ga)
  g2_ref[...] = _sigmoid(col(6))
  q = col(2)
  q = (q * lax.rsqrt(_head_rms(q, ones_ref) + EPS)) * qn_ref[...] * (HEAD_DIM ** -0.5)
  q_ref[...] = q.astype(bf16)
  k = col(3)
  k_ref[...] = (k * lax.rsqrt(_head_rms(k, ones_ref) + EPS)) * kn_ref[...]
  v_ref[...] = col(4)


def _inproj(x1, g, w_in, qn, kn, ones_bd):
  row = lambda i: (i, 0)
  out = lambda dt: jax.ShapeDtypeStruct((T, D), dt)
  return pl.pallas_call(
      _inproj_kernel,
      grid=(T // TM_PROJ,),
      in_specs=[
          pl.BlockSpec((TM_PROJ, D), row),
          _const_spec((1, D)),
          _const_spec((D, 7 * D)),
          _const_spec((1, D)),
          _const_spec((1, D)),
          _const_spec((MXU_N, MXU_N)),
      ],
      out_specs=[pl.BlockSpec((TM_PROJ, D), row)] * 6,
      out_shape=[out(f32), out(f32), out(f32), out(bf16), out(f32), out(f32)],
      compiler_params=pltpu.CompilerParams(
          dimension_semantics=("parallel",), vmem_limit_bytes=VMEM_LIMIT),
      name="inproj",
  )(x1, g, w_in, qn, kn, ones_bd)


def _lru_kernel(xf_ref, xfp_ref, xfn_ref, xb_ref, xbp_ref, xbn_ref,
                cw_ref, cb_ref, wg_ref, bg_ref, lam_ref,
                hf_ref, hb_ref, af_ref, bf_ref, ab_ref, bb_ref, car_ref):
  sc = pl.program_id(2)
  ns = pl.num_programs(2)

  @pl.when(sc == 0)
  def _():
    car_ref[...] = jnp.zeros_like(car_ref)

  def gates(d, chunk, x_ref, xp_ref, xn_ref, a_ref, b_ref):
    prev = jnp.where(chunk == 0, 0.0, xp_ref[...])
    nxt = jnp.where(chunk == ns - 1, 0.0, xn_ref[...])
    nsp = -LRU_C * _softplus(-lam_ref[d:d + 1, :])
    b_r = bg_ref[2 * d:2 * d + 1, :]
    b_i = bg_ref[2 * d + 1:2 * d + 2, :]

    def rows(lo, hi):
      parts = []
      if lo < 0:
        parts.append(prev[lo + 2:min(hi, 0) + 2])
      if hi > 0 and lo < LRU_SC:
        parts.append(x_ref[max(lo, 0):min(hi, LRU_SC)])
      if hi > LRU_SC:
        parts.append(nxt)
      return parts[0] if len(parts) == 1 else jnp.concatenate(parts, axis=0)

    for j in range(LRU_SC // LRU_SUB):
      s0 = j * LRU_SUB
      xc = cb_ref[...]
      for k in range(CONV_W):
        xc = xc + rows(s0 + k - 2, s0 + k - 2 + LRU_SUB) * cw_ref[k:k + 1, :]
      xc = xc.reshape(LRU_SUB * SUBLANES, LRU_WC)
      g = _dot(xc.astype(bf16), wg_ref[d, 0])
      r = _sigmoid(g[:, :LRU_WC] + b_r)
      i = _sigmoid(g[:, LRU_WC:] + b_i)
      a = jnp.exp(r * nsp)
      bx = jnp.sqrt(1.0 - a * a) * (i * xc)
      a_ref[s0:s0 + LRU_SUB] = a.reshape(LRU_SUB, SUBLANES, LRU_WC)
      b_ref[s0:s0 + LRU_SUB] = bx.reshape(LRU_SUB, SUBLANES, LRU_WC)

  gates(0, sc, xf_ref, xfp_ref, xfn_ref, af_ref, bf_ref)
  gates(1, ns - 1 - sc, xb_ref, xbp_ref, xbn_ref, ab_ref, bb_ref)

  def step(s, carry):
    hf, hb = carry
    hf = af_ref[s] * hf + bf_ref[s]
    hf_ref[s] = hf
    sb = LRU_SC - 1 - s
    hb = ab_ref[sb] * hb + bb_ref[sb]
    hb_ref[sb] = hb
    return hf, hb

  hf, hb = lax.fori_loop(0, LRU_SC, step, (car_ref[0], car_ref[1]), unroll=8)
  car_ref[0] = hf
  car_ref[1] = hb


def _lru(xr3, conv_w, conv_b, wg, bg, lam):
  ns = S // LRU_SC
  ncb = D // LRU_WC
  half = LRU_SC // 2
  blk = (LRU_SC, SUBLANES, LRU_WC)
  fwd = lambda g, c, s: (s, g, c)
  bwd = lambda g, c, s: (ns - 1 - s, g, c)
  fwd_p = lambda g, c, s: (jnp.maximum(s * half - 1, 0), g, c)
  bwd_p = lambda g, c, s: (jnp.maximum((ns - 1 - s) * half - 1, 0), g, c)
  fwd_n = lambda g, c, s: (jnp.minimum((s + 1) * LRU_SC, S - 1), g, c)
  bwd_n = lambda g, c, s: (jnp.minimum((ns - s) * LRU_SC, S - 1), g, c)
  chan = lambda g, c, s: (0, c)
  out = jax.ShapeDtypeStruct((S, B, D), f32)
  return pl.pallas_call(
      _lru_kernel,
      grid=(B // SUBLANES, ncb, ns),
      in_specs=[
          pl.BlockSpec(blk, fwd),
          pl.BlockSpec((2, SUBLANES, LRU_WC), fwd_p),
          pl.BlockSpec((1, SUBLANES, LRU_WC), fwd_n),
          pl.BlockSpec(blk, bwd),
          pl.BlockSpec((2, SUBLANES, LRU_WC), bwd_p),
          pl.BlockSpec((1, SUBLANES, LRU_WC), bwd_n),
          pl.BlockSpec((CONV_W, LRU_WC), chan),
          pl.BlockSpec((1, LRU_WC), chan),
          pl.BlockSpec((2, 1, LRU_WC, 2 * LRU_WC), lambda g, c, s: (0, c, 0, 0)),
          pl.BlockSpec((4, LRU_WC), chan),
          pl.BlockSpec((2, LRU_WC), chan),
      ],
      out_specs=[pl.BlockSpec(blk, fwd), pl.BlockSpec(blk, bwd)],
      out_shape=[out, out],
      scratch_shapes=[pltpu.VMEM(blk, f32)] * 4 + [pltpu.VMEM((2, SUBLANES, LRU_WC), f32)],
      compiler_params=pltpu.CompilerParams(
          dimension_semantics=("parallel", "parallel", "arbitrary"),
          vmem_limit_bytes=VMEM_LIMIT),
      name="lru",
  )(xr3, xr3, xr3, xr3, xr3, xr3, conv_w, conv_b, wg, bg, lam)


def _key_row0(rg):
  return int(np.clip(QR * rg - WIN_ROWS // 2, 0, ROWS - KR))


def _key_col0(n):
  return int(np.clip(QC * n - WIN_COLS // 2, 0, GRID_W - KC))


def _table_kind(rg):
  return 0 if rg == 0 else (2 if rg == N_RG - 1 else 1)


def _natt_kernel(q_ref, k_ref, v_ref, tbl_ref, o_ref):
  lane_head = lax.broadcasted_iota(jnp.int32, (QR * QC, HW), 1) // HEAD_DIM
  for rg in range(N_RG):
    k0 = _key_row0(rg)
    for n in range(N_CB):
      c0 = _key_col0(n)
      q_rows = [(QR * rg + rl) * GRID_W + QC * n for rl in range(QR)]
      k_rows = [(k0 + i) * GRID_W + c0 for i in range(KR)]
      qs = jnp.concatenate([q_ref[r:r + QC, :] for r in q_rows], axis=0)
      xq = jnp.concatenate(
          [jnp.where(lane_head == h, qs, jnp.zeros_like(qs)) for h in range(HG)], axis=0)
      kb = jnp.concatenate([k_ref[r:r + KC, :] for r in k_rows], axis=0).astype(bf16)
      vb = jnp.concatenate([v_ref[r:r + KC, :] for r in k_rows], axis=0).astype(bf16)
      st = lax.dot_general(kb, xq, (((1,), (1,)), ((), ())), preferred_element_type=f32)
      st = st + tbl_ref[0, _table_kind(rg), n]
      m = jnp.max(st, axis=0, keepdims=True)
      p = jnp.exp(st - m)
      p = (p * (1.0 / jnp.sum(p, axis=0, keepdims=True))).astype(bf16)
      o = lax.dot_general(p, vb, (((0,), (0,)), ((), ())), preferred_element_type=f32)
      nq = QR * QC
      acc = jnp.where(lane_head == 0, o[0:nq], 0.0)
      for h in range(1, HG):
        acc = jnp.where(lane_head == h, o[h * nq:(h + 1) * nq], acc)
      acc = acc.astype(bf16)
      for rl, r in enumerate(q_rows):
        o_ref[r:r + QC, :] = acc[rl * QC:(rl + 1) * QC]


def _natt(q2, k2, v2, tbl):
  ng = N_HEADS // HG
  col = lambda g, b: (0, b * ng + g)
  return pl.pallas_call(
      _natt_kernel,
      grid=(ng, B),
      in_specs=[
          pl.BlockSpec((S, HW), col),
          pl.BlockSpec((S, HW), col),
          pl.BlockSpec((S, HW), col),
          pl.BlockSpec((1, 3, N_CB, NKEY, HG * QR * QC), lambda g, b: (g, 0, 0, 0, 0)),
      ],
      out_specs=pl.BlockSpec((S, HW), col),
      out_shape=jax.ShapeDtypeStruct((S, B * D), bf16),
      compiler_params=pltpu.CompilerParams(
          dimension_semantics=("parallel", "parallel"), vmem_limit_bytes=VMEM_LIMIT),
      name="natt",
  )(q2, k2, v2, tbl)


def _bias_tables(rpb):
  i = np.arange(KR)[:, None, None, None]
  c = np.arange(KC)[None, :, None, None]
  rl = np.arange(QR)[None, None, :, None]
  ql = np.arange(QC)[None, None, None, :]
  kinds = []
  for rg in (0, 1, N_RG - 1):
    r = QR * rg + rl
    rs = np.clip(r - WIN_ROWS // 2, 0, ROWS - WIN_ROWS)
    krow = _key_row0(rg) + i
    ok_r = (krow >= rs) & (krow < rs + WIN_ROWS)
    dr = np.clip(krow - r + WIN_ROWS - 1, 0, 2 * WIN_ROWS - 2)
    per_n = []
    for n in range(N_CB):
      qc = QC * n + ql
      kc = _key_col0(n) + c
      cs = np.clip(qc - WIN_COLS // 2, 0, GRID_W - WIN_COLS)
      ok_c = (kc >= cs) & (kc < cs + WIN_COLS)
      dc = np.clip(kc - qc, -(WIN_COLS - 1), WIN_COLS - 1) + WIN_COLS - 1
      shape = (KR, KC, QR, QC)
      per_n.append((np.broadcast_to(dr, shape).reshape(NKEY, QR * QC),
                    np.broadcast_to(dc, shape).reshape(NKEY, QR * QC),
                    np.broadcast_to(ok_r & ok_c, shape).reshape(NKEY, QR * QC)))
    kinds.append(per_n)
  dr = np.array([[p[0] for p in k] for k in kinds])
  dc = np.array([[p[1] for p in k] for k in kinds])
  ok = np.array([[p[2] for p in k] for k in kinds])
  bias = rpb.astype(f32)[:, dr, dc]
  bias = jnp.where(ok[None], bias, NEG)
  ng = N_HEADS // HG
  bias = bias.reshape(ng, HG, 3, N_CB, NKEY, QR * QC)
  return jnp.moveaxis(bias, 1, 4).reshape(ng, 3, N_CB, NKEY, HG * QR * QC)


def _ffn2_kernel(x1_ref, hf_ref, hb_ref, g1_ref, g2_ref, ya_ref, wo_ref,
                 g_ref, wgu_ref, wdn_ref, o_ref, act_ref):
  y = g1_ref[...] * (hf_ref[...] + hb_ref[...]) + g2_ref[...] * ya_ref[...].astype(f32)
  x2 = x1_ref[...] + _dot(y.astype(bf16), wo_ref[...])
  o_ref[...] = x2 + FFN_RES * _swiglu(x2, g_ref, wgu_ref, wdn_ref, act_ref)


def _ffn2(x1t, hf, hb, g1, g2, ya, wo, g, wgu, wdn):
  nj = S // TM_OUT
  tm_spec = pl.BlockSpec((TM_OUT, D), lambda b, j: (j, b))
  return pl.pallas_call(
      _ffn2_kernel,
      grid=(B, nj),
      in_specs=[tm_spec] * 6 + [
          _const_spec((D, D)),
          _const_spec((1, D)),
          _const_spec((D, 2 * D_FF)),
          _const_spec((D_FF, D)),
      ],
      out_specs=pl.BlockSpec((TM_OUT, D), lambda b, j: (b * nj + j, 0)),
      out_shape=jax.ShapeDtypeStruct((T, D), f32),
      scratch_shapes=[pltpu.VMEM((TM_OUT, D_FF), bf16)],
      compiler_params=pltpu.CompilerParams(
          dimension_semantics=("parallel", "parallel"), vmem_limit_bytes=VMEM_LIMIT),
      name="ffn2",
  )(x1t, hf, hb, g1, g2, ya, wo, g, wgu, wdn)


def _gate_weights(w):
  per = LRU_WC // LRU_BLOCK
  ncb = D // LRU_WC
  w = w.reshape(2, 2, ncb, per, LRU_BLOCK, LRU_BLOCK)
  eye = jnp.eye(per, dtype=w.dtype)
  dense = w[:, :, :, :, :, None, :] * eye[None, None, None, :, None, :, None]
  dense = dense.reshape(2, 2, ncb, LRU_WC, LRU_WC)
  return jnp.concatenate([dense[:, 0], dense[:, 1]], axis=-1).astype(bf16)


def kernel(x, norm_ffn1, w_ffn1_gu, w_ffn1_down, norm_mix, w_in, conv_w, conv_b,
           lru_w_gates, lru_b_gates, lru_lambda, q_norm, k_norm, rel_pos_bias, w_out,
           norm_ffn2, w_ffn2_gu, w_ffn2_down):
  assert x.shape == (B, S, D) and norm_ffn1.shape[0] == 1
  l = 0
  row = lambda a: a.reshape(1, D).astype(f32)
  ones_bd = jnp.asarray(
      np.kron(np.eye(MXU_N // HEAD_DIM), np.ones((HEAD_DIM, HEAD_DIM))), dtype=bf16)

  x1t = _ffn1(x.reshape(T, D), row(norm_ffn1[l]), w_ffn1_gu[l].astype(bf16),
              w_ffn1_down[l].astype(bf16))

  xr, g1, g2, q, k, v = _inproj(
      x1t.reshape(T, D), row(norm_mix[l]), w_in[l].astype(bf16),
      row(jnp.tile(q_norm[l], N_HEADS)), row(jnp.tile(k_norm[l], N_HEADS)), ones_bd)

  hf, hb = _lru(xr.reshape(S, B, D), conv_w[l].astype(f32), row(conv_b[l]),
                _gate_weights(lru_w_gates[l]), lru_b_gates[l].reshape(4, D).astype(f32),
                lru_lambda[l].astype(f32))

  view = lambda a: a.reshape(S, B * D)
  ya = _natt(view(q), view(k), view(v), _bias_tables(rel_pos_bias[l]))

  out = _ffn2(x1t, view(hf), view(hb), view(g1), view(g2), ya, w_out[l].astype(bf16),
              row(norm_ffn2[l]), w_ffn2_gu[l].astype(bf16), w_ffn2_down[l].astype(bf16))
  return out.reshape(B, S, D)
```

```python
import functools

import jax
import jax.numpy as jnp
import numpy as np
from jax import lax
from jax.experimental import pallas as pl
from jax.experimental.pallas import tpu as pltpu

D = 1024
B = 32
S = 2048
T = B * S
GRID_W = 64
ROWS = S // GRID_W
N_HEADS = 16
HEAD_DIM = 64
WIN_ROWS = 8
WIN_COLS = 16
LRU_BLOCKS = 16
LRU_BLOCK = 64
LRU_C = 8.0
CONV_W = 4
D_FF = 2816
FFN_RES = 0.5
EPS = 1e-6
NEG = -1e30

SUBLANES = 8
LANES = 128
MXU_N = 256
VMEM_LIMIT = 56 * 1024 * 1024

TM_FFN = 512
TS_PROJ = 16
PITCH_PROJ = 24
TS_OUT = 8
PITCH_OUT = 40
FF_CHUNK = MXU_N
LRU_SC = 256
LRU_WC = 256
LRU_SUB = 32
HG = 4
HW = HG * HEAD_DIM
QR = 4
QC = 16
KR = QR + WIN_ROWS - 1
KC = 32
NKEY = KR * KC
N_CB = GRID_W // QC
N_RG = ROWS // QR

f32 = jnp.float32
bf16 = jnp.bfloat16


def _sigmoid(x):
  return 0.5 * jnp.tanh(0.5 * x) + 0.5


def _softplus(x):
  return jnp.maximum(x, 0.0) + jnp.log1p(jnp.exp(-jnp.abs(x)))


def _gelu_tanh(x):
  c = float(np.sqrt(2.0 / np.pi))
  return 0.5 * x * (1.0 + jnp.tanh(c * (x + 0.044715 * (x * x * x))))


def _rmsnorm(x, g):
  ms = jnp.mean(x * x, axis=-1, keepdims=True)
  return (x * lax.rsqrt(ms + EPS)) * g


def _dot(a, b):
  return jnp.dot(a, b, preferred_element_type=f32)


def _swiglu(x, g_ref, wgu_ref, wdn_ref, act_ref):
  h = _rmsnorm(x, g_ref[...]).astype(bf16)
  for c in range(D_FF // FF_CHUNK):
    lo = c * FF_CHUNK
    gate = _dot(h, wgu_ref[:, lo:lo + FF_CHUNK])
    up = _dot(h, wgu_ref[:, D_FF + lo:D_FF + lo + FF_CHUNK])
    act_ref[:, lo:lo + FF_CHUNK] = (gate * _sigmoid(gate) * up).astype(bf16)
  return _dot(act_ref[...], wdn_ref[...])


def _ffn1_kernel(x_ref, g_ref, wgu_ref, wdn_ref, o_ref, act_ref):
  x = x_ref[...]
  o_ref[...] = x + FFN_RES * _swiglu(x, g_ref, wgu_ref, wdn_ref, act_ref)


def _const_spec(shape):
  nd = len(shape)
  return pl.BlockSpec(shape, lambda *_: (0,) * nd, pipeline_mode=pl.Buffered(1))


def _ffn1(x2d, g, wgu, wdn):
  row = lambda i: (i, 0)
  return pl.pallas_call(
      _ffn1_kernel,
      grid=(T // TM_FFN,),
      in_specs=[
          pl.BlockSpec((TM_FFN, D), row),
          _const_spec((1, D)),
          _const_spec((D, 2 * D_FF)),
          _const_spec((D_FF, D)),
      ],
      out_specs=pl.BlockSpec((TM_FFN, D), row),
      out_shape=jax.ShapeDtypeStruct((T, D), f32),
      scratch_shapes=[pltpu.VMEM((TM_FFN, D_FF), bf16)],
      compiler_params=pltpu.CompilerParams(
          dimension_semantics=("parallel",), vmem_limit_bytes=VMEM_LIMIT),
      name="ffn1",
  )(x2d, g, wgu, wdn)


def _head_rms(x, ones_ref):
  parts = []
  for j in range(D // MXU_N):
    sq = x[:, j * MXU_N:(j + 1) * MXU_N]
    sq = sq * sq
    hi = sq.astype(bf16)
    lo = (sq - hi.astype(f32)).astype(bf16)
    parts.append(_dot(hi, ones_ref[...]) + _dot(lo, ones_ref[...]))
  return jnp.concatenate(parts, axis=-1) * (1.0 / HEAD_DIM)


def _swap_rows(x, perm_ref, m, n, pitch):
  for i in range(m):
    for c in range(D // LANES):
      perm_ref[c, i * pitch:i * pitch + n, :] = x[i * n:(i + 1) * n, c * LANES:(c + 1) * LANES]
  return jnp.concatenate(
      [jnp.concatenate([perm_ref[c, pl.ds(j, m, stride=pitch), :] for c in range(D // LANES)], axis=-1)
       for j in range(n)], axis=0)


def _inproj_kernel(x_ref, g_ref, w_ref, qn_ref, kn_ref, ones_ref,
                   xr_ref, g1_ref, g2_ref, q_ref, k_ref, v_ref, perm_ref):
  h = _rmsnorm(x_ref[...].reshape(B * TS_PROJ, D), g_ref[...])
  h_bm = h.astype(bf16)
  h_tm = _swap_rows(h, perm_ref, B, TS_PROJ, PITCH_PROJ).astype(bf16)

  def col(lhs, i):
    return _dot(lhs, w_ref[:, i * D:(i + 1) * D])

  tm = lambda a: a.reshape(TS_PROJ, B, D)
  bm = lambda a: a.reshape(B, TS_PROJ, D)
  xr_ref[...] = tm(col(h_tm, 0))
  g1_ref[...] = tm(_gelu_tanh(col(h_tm, 1)) * _sigmoid(col(h_tm, 5)))
  g2_ref[...] = bm(_sigmoid(col(h_bm, 6)))
  q = col(h_bm, 2)
  q = (q * lax.rsqrt(_head_rms(q, ones_ref) + EPS)) * qn_ref[...] * (HEAD_DIM ** -0.5)
  q_ref[...] = bm(q.astype(bf16))
  k = col(h_bm, 3)
  k_ref[...] = bm((k * lax.rsqrt(_head_rms(k, ones_ref) + EPS)) * kn_ref[...])
  v_ref[...] = bm(col(h_bm, 4))


def _inproj(x1, g, w_in, qn, kn, ones_bd):
  bm_spec = pl.BlockSpec((B, TS_PROJ, D), lambda j: (0, j, 0))
  tm_spec = pl.BlockSpec((TS_PROJ, B, D), lambda j: (j, 0, 0))
  bm_out = lambda dt: jax.ShapeDtypeStruct((B, S, D), dt)
  tm_out = jax.ShapeDtypeStruct((S, B, D), f32)
  return pl.pallas_call(
      _inproj_kernel,
      grid=(S // TS_PROJ,),
      in_specs=[
          bm_spec,
          _const_spec((1, D)),
          _const_spec((D, 7 * D)),
          _const_spec((1, D)),
          _const_spec((1, D)),
          _const_spec((MXU_N, MXU_N)),
      ],
      out_specs=[tm_spec, tm_spec, bm_spec, bm_spec, bm_spec, bm_spec],
      out_shape=[tm_out, tm_out, bm_out(f32), bm_out(bf16), bm_out(f32), bm_out(f32)],
      scratch_shapes=[pltpu.VMEM((D // LANES, B * PITCH_PROJ, LANES), f32)],
      compiler_params=pltpu.CompilerParams(
          dimension_semantics=("parallel",), vmem_limit_bytes=VMEM_LIMIT),
      name="inproj",
  )(x1, g, w_in, qn, kn, ones_bd)


def _lru_kernel(xf_ref, xfp_ref, xfn_ref, xb_ref, xbp_ref, xbn_ref,
                cw_ref, cb_ref, wg_ref, bg_ref, lam_ref,
                hf_ref, hb_ref, af_ref, bf_ref, ab_ref, bb_ref, car_ref):
  sc = pl.program_id(2)
  ns = pl.num_programs(2)

  @pl.when(sc == 0)
  def _():
    car_ref[...] = jnp.zeros_like(car_ref)

  def gates(d, chunk, x_ref, xp_ref, xn_ref, a_ref, b_ref):
    prev = jnp.where(chunk == 0, 0.0, xp_ref[...])
    nxt = jnp.where(chunk == ns - 1, 0.0, xn_ref[...])
    nsp = -LRU_C * _softplus(-lam_ref[d:d + 1, :])
    b_r = bg_ref[2 * d:2 * d + 1, :]
    b_i = bg_ref[2 * d + 1:2 * d + 2, :]

    def rows(lo, hi):
      parts = []
      if lo < 0:
        parts.append(prev[lo + 2:min(hi, 0) + 2])
      if hi > 0 and lo < LRU_SC:
        parts.append(x_ref[max(lo, 0):min(hi, LRU_SC)])
      if hi > LRU_SC:
        parts.append(nxt)
      return parts[0] if len(parts) == 1 else jnp.concatenate(parts, axis=0)

    for j in range(LRU_SC // LRU_SUB):
      s0 = j * LRU_SUB
      xc = cb_ref[...]
      for k in range(CONV_W):
        xc = xc + rows(s0 + k - 2, s0 + k - 2 + LRU_SUB) * cw_ref[k:k + 1, :]
      xc = xc.reshape(LRU_SUB * SUBLANES, LRU_WC)
      g = _dot(xc.astype(bf16), wg_ref[d, 0])
      r = _sigmoid(g[:, :LRU_WC] + b_r)
      i = _sigmoid(g[:, LRU_WC:] + b_i)
      a = jnp.exp(r * nsp)
      bx = jnp.sqrt(1.0 - a * a) * (i * xc)
      a_ref[s0:s0 + LRU_SUB] = a.reshape(LRU_SUB, SUBLANES, LRU_WC)
      b_ref[s0:s0 + LRU_SUB] = bx.reshape(LRU_SUB, SUBLANES, LRU_WC)

  gates(0, sc, xf_ref, xfp_ref, xfn_ref, af_ref, bf_ref)
  gates(1, ns - 1 - sc, xb_ref, xbp_ref, xbn_ref, ab_ref, bb_ref)

  def step(s, carry):
    hf, hb = carry
    hf = af_ref[s] * hf + bf_ref[s]
    hf_ref[s] = hf
    sb = LRU_SC - 1 - s
    hb = ab_ref[sb] * hb + bb_ref[sb]
    hb_ref[sb] = hb
    return hf, hb

  hf, hb = lax.fori_loop(0, LRU_SC, step, (car_ref[0], car_ref[1]), unroll=8)
  car_ref[0] = hf
  car_ref[1] = hb


def _lru(xr3, conv_w, conv_b, wg, bg, lam):
  ns = S // LRU_SC
  ncb = D // LRU_WC
  half = LRU_SC // 2
  blk = (LRU_SC, SUBLANES, LRU_WC)
  fwd = lambda g, c, s: (s, g, c)
  bwd = lambda g, c, s: (ns - 1 - s, g, c)
  fwd_p = lambda g, c, s: (jnp.maximum(s * half - 1, 0), g, c)
  bwd_p = lambda g, c, s: (jnp.maximum((ns - 1 - s) * half - 1, 0), g, c)
  fwd_n = lambda g, c, s: (jnp.minimum((s + 1) * LRU_SC, S - 1), g, c)
  bwd_n = lambda g, c, s: (jnp.minimum((ns - s) * LRU_SC, S - 1), g, c)
  chan = lambda g, c, s: (0, c)
  out = jax.ShapeDtypeStruct((S, B, D), f32)
  return pl.pallas_call(
      _lru_kernel,
      grid=(B // SUBLANES, ncb, ns),
      in_specs=[
          pl.BlockSpec(blk, fwd),
          pl.BlockSpec((2, SUBLANES, LRU_WC), fwd_p),
          pl.BlockSpec((1, SUBLANES, LRU_WC), fwd_n),
          pl.BlockSpec(blk, bwd),
          pl.BlockSpec((2, SUBLANES, LRU_WC), bwd_p),
          pl.BlockSpec((1, SUBLANES, LRU_WC), bwd_n),
          pl.BlockSpec((CONV_W, LRU_WC), chan),
          pl.BlockSpec((1, LRU_WC), chan),
          pl.BlockSpec((2, 1, LRU_WC, 2 * LRU_WC), lambda g, c, s: (0, c, 0, 0)),
          pl.BlockSpec((4, LRU_WC), chan),
          pl.BlockSpec((2, LRU_WC), chan),
      ],
      out_specs=[pl.BlockSpec(blk, fwd), pl.BlockSpec(blk, bwd)],
      out_shape=[out, out],
      scratch_shapes=[pltpu.VMEM(blk, f32)] * 4 + [pltpu.VMEM((2, SUBLANES, LRU_WC), f32)],
      compiler_params=pltpu.CompilerParams(
          dimension_semantics=("parallel", "parallel", "arbitrary"),
          vmem_limit_bytes=VMEM_LIMIT),
      name="lru",
  )(xr3, xr3, xr3, xr3, xr3, xr3, conv_w, conv_b, wg, bg, lam)


def _key_row0(rg):
  return int(np.clip(QR * rg - WIN_ROWS // 2, 0, ROWS - KR))


def _key_col0(n):
  return int(np.clip(QC * n - WIN_COLS // 2, 0, GRID_W - KC))


def _table_kind(rg):
  return 0 if rg == 0 else (2 if rg == N_RG - 1 else 1)


def _natt_kernel(q_ref, k_ref, v_ref, tbl_ref, o_ref):
  lane_head = lax.broadcasted_iota(jnp.int32, (QR * QC, HW), 1) // HEAD_DIM
  for rg in range(N_RG):
    k0 = _key_row0(rg)
    for n in range(N_CB):
      c0 = _key_col0(n)
      q_rows = [(QR * rg + rl) * GRID_W + QC * n for rl in range(QR)]
      k_rows = [(k0 + i) * GRID_W + c0 for i in range(KR)]
      qs = jnp.concatenate([q_ref[r:r + QC, :] for r in q_rows], axis=0)
      xq = jnp.concatenate(
          [jnp.where(lane_head == h, qs, jnp.zeros_like(qs)) for h in range(HG)], axis=0)
      kb = jnp.concatenate([k_ref[r:r + KC, :] for r in k_rows], axis=0).astype(bf16)
      vb = jnp.concatenate([v_ref[r:r + KC, :] for r in k_rows], axis=0).astype(bf16)
      st = lax.dot_general(kb, xq, (((1,), (1,)), ((), ())), preferred_element_type=f32)
      st = st + tbl_ref[0, _table_kind(rg), n]
      m = jnp.max(st, axis=0, keepdims=True)
      p = jnp.exp(st - m)
      p = (p * (1.0 / jnp.sum(p, axis=0, keepdims=True))).astype(bf16)
      o = lax.dot_general(p, vb, (((0,), (0,)), ((), ())), preferred_element_type=f32)
      nq = QR * QC
      acc = jnp.where(lane_head == 0, o[0:nq], 0.0)
      for h in range(1, HG):
        acc = jnp.where(lane_head == h, o[h * nq:(h + 1) * nq], acc)
      for rl, r in enumerate(q_rows):
        o_ref[r:r + QC, :] = acc[rl * QC:(rl + 1) * QC]


def _natt(q, k, v, tbl):
  ng = N_HEADS // HG
  spec = pl.BlockSpec((None, S, HW), lambda g, b: (b, 0, g))
  return pl.pallas_call(
      _natt_kernel,
      grid=(ng, B),
      in_specs=[
          spec, spec, spec,
          pl.BlockSpec((1, 3, N_CB, NKEY, HG * QR * QC), lambda g, b: (g, 0, 0, 0, 0)),
      ],
      out_specs=spec,
      out_shape=jax.ShapeDtypeStruct((B, S, D), f32),
      compiler_params=pltpu.CompilerParams(
          dimension_semantics=("parallel", "parallel"), vmem_limit_bytes=VMEM_LIMIT),
      name="natt",
  )(q, k, v, tbl)


def _bias_tables(rpb):
  n_dr, n_dc = 2 * WIN_ROWS - 1, 2 * WIN_COLS - 1
  i = np.arange(KR)[:, None]
  rl = np.arange(QR)[None, :]
  row_hot, row_ok = [], []
  for rg in (0, 1, N_RG - 1):
    r = QR * rg + rl
    rs = np.clip(r - WIN_ROWS // 2, 0, ROWS - WIN_ROWS)
    krow = _key_row0(rg) + i
    row_ok.append((krow >= rs) & (krow < rs + WIN_ROWS))
    row_hot.append(np.eye(n_dr)[np.clip(krow - r + WIN_ROWS - 1, 0, n_dr - 1)])
  c = np.arange(KC)[:, None]
  ql = np.arange(QC)[None, :]
  col_hot, col_ok = [], []
  for n in range(N_CB):
    qc = QC * n + ql
    kc = _key_col0(n) + c
    cs = np.clip(qc - WIN_COLS // 2, 0, GRID_W - WIN_COLS)
    col_ok.append((kc >= cs) & (kc < cs + WIN_COLS))
    col_hot.append(np.eye(n_dc)[np.clip(kc - qc, -(WIN_COLS - 1), WIN_COLS - 1) + WIN_COLS - 1])
  row_ok, col_ok = np.stack(row_ok), np.stack(col_ok)
  ok = row_ok[:, None, :, None, None, :, None] & col_ok[None, :, None, :, None, None, :]
  ng = N_HEADS // HG
  bias = jnp.einsum(
      "kira,ghab,ncqb->gknichrq", jnp.asarray(np.stack(row_hot), f32),
      rpb.astype(f32).reshape(ng, HG, n_dr, n_dc), jnp.asarray(np.stack(col_hot), f32),
      precision=lax.Precision.HIGHEST)
  bias = jnp.where(ok[None], bias, NEG)
  return bias.reshape(ng, 3, N_CB, NKEY, HG * QR * QC)


def _ffn2_kernel(x1_ref, hf_ref, hb_ref, g1_ref, g2_ref, ya_ref, wo_ref,
                 g_ref, wgu_ref, wdn_ref, o_ref, act_ref, perm_ref):
  rows = B * TS_OUT
  y_lru = (g1_ref[...] * (hf_ref[...] + hb_ref[...])).reshape(rows, D)
  y_lru = _swap_rows(y_lru, perm_ref, TS_OUT, B, PITCH_OUT)
  y = y_lru + (g2_ref[...] * ya_ref[...]).reshape(rows, D)
  x2 = x1_ref[...].reshape(rows, D) + _dot(y.astype(bf16), wo_ref[...])
  out = x2 + FFN_RES * _swiglu(x2, g_ref, wgu_ref, wdn_ref, act_ref)
  o_ref[...] = out.reshape(B, TS_OUT, D)


def _ffn2(x1, hf, hb, g1, g2, ya, wo, g, wgu, wdn):
  bm_spec = pl.BlockSpec((B, TS_OUT, D), lambda j: (0, j, 0))
  tm_spec = pl.BlockSpec((TS_OUT, B, D), lambda j: (j, 0, 0))
  return pl.pallas_call(
      _ffn2_kernel,
      grid=(S // TS_OUT,),
      in_specs=[bm_spec, tm_spec, tm_spec, tm_spec, bm_spec, bm_spec,
                _const_spec((D, D)),
                _const_spec((1, D)),
                _const_spec((D, 2 * D_FF)),
                _const_spec((D_FF, D))],
      out_specs=bm_spec,
      out_shape=jax.ShapeDtypeStruct((B, S, D), f32),
      scratch_shapes=[pltpu.VMEM((B * TS_OUT, D_FF), bf16),
                      pltpu.VMEM((D // LANES, TS_OUT * PITCH_OUT, LANES), f32)],
      compiler_params=pltpu.CompilerParams(
          dimension_semantics=("parallel",), vmem_limit_bytes=VMEM_LIMIT),
      name="ffn2",
  )(x1, hf, hb, g1, g2, ya, wo, g, wgu, wdn)


def _gate_weights(w):
  per = LRU_WC // LRU_BLOCK
  ncb = D // LRU_WC
  w = w.reshape(2, 2, ncb, per, LRU_BLOCK, LRU_BLOCK)
  eye = jnp.eye(per, dtype=w.dtype)
  dense = w[:, :, :, :, :, None, :] * eye[None, None, None, :, None, :, None]
  dense = dense.reshape(2, 2, ncb, LRU_WC, LRU_WC)
  return jnp.concatenate([dense[:, 0], dense[:, 1]], axis=-1).astype(bf16)


def kernel(x, norm_ffn1, w_ffn1_gu, w_ffn1_down, norm_mix, w_in, conv_w, conv_b,
           lru_w_gates, lru_b_gates, lru_lambda, q_norm, k_norm, rel_pos_bias, w_out,
           norm_ffn2, w_ffn2_gu, w_ffn2_down):
  assert x.shape == (B, S, D) and norm_ffn1.shape[0] == 1
  l = 0
  row = lambda a: a.reshape(1, D).astype(f32)
  ones_bd = jnp.asarray(
      np.kron(np.eye(MXU_N // HEAD_DIM), np.ones((HEAD_DIM, HEAD_DIM))), dtype=bf16)

  x1 = _ffn1(x.reshape(T, D), row(norm_ffn1[l]), w_ffn1_gu[l].astype(bf16),
             w_ffn1_down[l].astype(bf16)).reshape(B, S, D)

  xr, g1, g2, q, k, v = _inproj(
      x1, row(norm_mix[l]), w_in[l].astype(bf16),
      row(jnp.tile(q_norm[l], N_HEADS)), row(jnp.tile(k_norm[l], N_HEADS)), ones_bd)

  hf, hb = _lru(xr, conv_w[l].astype(f32), row(conv_b[l]),
                _gate_weights(lru_w_gates[l]), lru_b_gates[l].reshape(4, D).astype(f32),
                lru_lambda[l].astype(f32))

  ya = _natt(q, k, v, _bias_tables(rel_pos_bias[l]))

  return _ffn2(x1, hf, hb, g1, g2, ya, w_out[l].astype(bf16),
               row(norm_ffn2[l]), w_ffn2_gu[l].astype(bf16), w_ffn2_down[l].astype(bf16))
```

```python
import jax
import jax.numpy as jnp
import numpy as np
from jax import lax
from jax.experimental import pallas as pl
from jax.experimental.pallas import tpu as pltpu

D = 1024
B = 32
S = 2048
T = B * S
GRID_W = 64
ROWS = S // GRID_W
N_HEADS = 16
HEAD_DIM = 64
WIN_ROWS = 8
WIN_COLS = 16
LRU_BLOCK = 64
LRU_C = 8.0
CONV_W = 4
CONV_LEFT = 2
D_FF = 2816
FFN_RES = 0.5
EPS = 1e-6
NEG = -1e30
LOG2E = float(np.log2(np.e))
TINY = 1e-37

SUBLANES = 8
LANES = 128
MXU_N = 256
VMEM_LIMIT = 56 * 1024 * 1024

TM_FFN = 512
TS_PROJ = 16
PITCH_PROJ = 24
TS_OUT = 16
PITCH_OUT = 40
FF_CHUNK = MXU_N
LRU_SC = 256
LRU_WC = 512
LRU_GW = MXU_N
LRU_SUB = 32
HG = 4
HW = HG * HEAD_DIM
QR = 4
QC = 16
KR = 12
KC = 32
NQ = QR * QC
NKEY = KR * KC
N_CB = GRID_W // QC
N_RG = ROWS // QR

f32 = jnp.float32
bf16 = jnp.bfloat16


def _sigmoid(x):
  return 0.5 * jnp.tanh(0.5 * x) + 0.5


def _softplus(x):
  return jnp.maximum(x, 0.0) + jnp.log1p(jnp.exp(-jnp.abs(x)))


def _gelu_tanh(x, scale=1.0):
  c = float(np.sqrt(2.0 / np.pi))
  return (0.5 * scale) * x * (1.0 + jnp.tanh(c * (x + 0.044715 * (x * x * x))))


def _rmsnorm(x, g):
  ms = jnp.mean(x * x, axis=-1, keepdims=True)
  return (x * lax.rsqrt(ms + EPS)) * g


def _dot(a, b):
  return jnp.dot(a, b, preferred_element_type=f32)


def _const_spec(shape):
  nd = len(shape)
  return pl.BlockSpec(shape, lambda *_: (0,) * nd, pipeline_mode=pl.Buffered(1))


def _swiglu(x, g_ref, wgu_ref, wdn_ref, act_ref):
  h = _rmsnorm(x, g_ref[...]).astype(bf16)
  for c in range(D_FF // FF_CHUNK):
    lo = c * FF_CHUNK
    gate = _dot(h, wgu_ref[:, lo:lo + FF_CHUNK])
    up = _dot(h, wgu_ref[:, D_FF + lo:D_FF + lo + FF_CHUNK])
    act_ref[:, lo:lo + FF_CHUNK] = (gate * _sigmoid(gate) * up).astype(bf16)
  return _dot(act_ref[...], wdn_ref[...])


def _swap_rows(x, perm_ref, m, n, pitch):
  for i in range(m):
    for c in range(D // LANES):
      perm_ref[c, i * pitch:i * pitch + n, :] = x[i * n:(i + 1) * n, c * LANES:(c + 1) * LANES]
  return jnp.concatenate(
      [jnp.concatenate([perm_ref[c, pl.ds(j, m, stride=pitch), :] for c in range(D // LANES)], axis=-1)
       for j in range(n)], axis=0)


def _ffn1_kernel(x_ref, g_ref, wgu_ref, wdn_ref, o_ref, act_ref):
  x = x_ref[...]
  o_ref[...] = x + FFN_RES * _swiglu(x, g_ref, wgu_ref, wdn_ref, act_ref)


def _ffn1(x2d, g, wgu, wdn):
  row = lambda i: (i, 0)
  return pl.pallas_call(
      _ffn1_kernel,
      grid=(T // TM_FFN,),
      in_specs=[
          pl.BlockSpec((TM_FFN, D), row),
          _const_spec((1, D)),
          _const_spec((D, 2 * D_FF)),
          _const_spec((D_FF, D)),
      ],
      out_specs=pl.BlockSpec((TM_FFN, D), row),
      out_shape=jax.ShapeDtypeStruct((T, D), f32),
      scratch_shapes=[pltpu.VMEM((TM_FFN, D_FF), bf16)],
      compiler_params=pltpu.CompilerParams(
          dimension_semantics=("parallel",), vmem_limit_bytes=VMEM_LIMIT),
      name="ffn1",
  )(x2d, g, wgu, wdn)


def _head_rms(x, ones_ref):
  parts = []
  for j in range(D // MXU_N):
    sq = x[:, j * MXU_N:(j + 1) * MXU_N]
    parts.append(_dot((sq * sq).astype(bf16), ones_ref[...]))
  return jnp.concatenate(parts, axis=-1) * (1.0 / HEAD_DIM)


def _inproj_kernel(x_ref, g_ref, w_ref, qn_ref, kn_ref, ones_ref,
                   xr_ref, g1_ref, g2_ref, q_ref, k_ref, v_ref, perm_ref):
  h = _rmsnorm(x_ref[...].reshape(B * TS_PROJ, D), g_ref[...])
  h_bm = h.astype(bf16)
  h_tm = _swap_rows(h, perm_ref, B, TS_PROJ, PITCH_PROJ).astype(bf16)

  def col(lhs, i):
    return _dot(lhs, w_ref[:, i * D:(i + 1) * D])

  tm = lambda a: a.reshape(TS_PROJ, B, D)
  bm = lambda a: a.reshape(B, TS_PROJ, D)
  xr_ref[...] = tm(col(h_tm, 0))
  g1_ref[...] = tm(_gelu_tanh(col(h_tm, 1), scale=0.5) * _sigmoid(col(h_tm, 5)))
  g2_ref[...] = bm(_sigmoid(col(h_bm, 6)))
  q = col(h_bm, 2)
  q = (q * lax.rsqrt(_head_rms(q, ones_ref) + EPS)) * qn_ref[...] * (HEAD_DIM ** -0.5)
  q_ref[...] = bm(q.astype(bf16))
  k = col(h_bm, 3)
  k_ref[...] = bm((k * lax.rsqrt(_head_rms(k, ones_ref) + EPS)) * kn_ref[...])
  v_ref[...] = bm(col(h_bm, 4))


def _inproj(x1, g, w_in, qn, kn, ones_bd):
  bm_spec = pl.BlockSpec((B, TS_PROJ, D), lambda j: (0, j, 0))
  tm_spec = pl.BlockSpec((TS_PROJ, B, D), lambda j: (j, 0, 0))
  bm_out = lambda dt: jax.ShapeDtypeStruct((B, S, D), dt)
  tm_out = jax.ShapeDtypeStruct((S, B, D), f32)
  return pl.pallas_call(
      _inproj_kernel,
      grid=(S // TS_PROJ,),
      in_specs=[
          bm_spec,
          _const_spec((1, D)),
          _const_spec((D, 7 * D)),
          _const_spec((1, D)),
          _const_spec((1, D)),
          _const_spec((MXU_N, MXU_N)),
      ],
      out_specs=[tm_spec, tm_spec, bm_spec, bm_spec, bm_spec, bm_spec],
      out_shape=[tm_out, tm_out, bm_out(f32), bm_out(bf16), bm_out(f32), bm_out(f32)],
      scratch_shapes=[pltpu.VMEM((D // LANES, B * PITCH_PROJ, LANES), f32)],
      compiler_params=pltpu.CompilerParams(
          dimension_semantics=("parallel",), vmem_limit_bytes=VMEM_LIMIT),
      name="inproj",
  )(x1, g, w_in, qn, kn, ones_bd)


def _make_lru_kernel(reverse):
  def kernel(x_ref, xp_ref, xn_ref, cw_ref, cb_ref, wg_ref, bg_ref, lam_ref, *rest):
    if reverse:
      hf_ref, g1_ref, o_ref, a_ref, b_ref, car_ref = rest
    else:
      o_ref, a_ref, b_ref, car_ref = rest
    sc = pl.program_id(2)
    ns = pl.num_programs(2)
    chunk = ns - 1 - sc if reverse else sc

    @pl.when(sc == 0)
    def _():
      car_ref[...] = jnp.zeros_like(car_ref)

    prev = jnp.where(chunk == 0, 0.0, xp_ref[...])
    nxt = jnp.where(chunk == ns - 1, 0.0, xn_ref[...])
    kexp = (-0.5 * LRU_C * LOG2E) * _softplus(-lam_ref[...])
    b_r = 0.5 * bg_ref[0:1, :]
    b_i = 0.5 * bg_ref[1:2, :]

    def rows(lo, hi):
      parts = []
      if lo < 0:
        parts.append(prev[lo + CONV_LEFT:min(hi, 0) + CONV_LEFT])
      if hi > 0 and lo < LRU_SC:
        parts.append(x_ref[max(lo, 0):min(hi, LRU_SC)])
      if hi > LRU_SC:
        parts.append(nxt)
      return parts[0] if len(parts) == 1 else jnp.concatenate(parts, axis=0)

    for j in range(LRU_SC // LRU_SUB):
      s0 = j * LRU_SUB
      xc = cb_ref[...]
      for k in range(CONV_W):
        lo = s0 + k - CONV_LEFT
        xc = xc + rows(lo, lo + LRU_SUB) * cw_ref[k:k + 1, :]
      xc = xc.reshape(LRU_SUB * SUBLANES, LRU_WC)
      for hh in range(LRU_WC // LRU_GW):
        cs = slice(hh * LRU_GW, (hh + 1) * LRU_GW)
        xh = xc[:, cs]
        g = _dot(xh.astype(bf16), wg_ref[hh])
        tr = jnp.tanh(g[:, :LRU_GW] + b_r[:, cs])
        ti = jnp.tanh(g[:, LRU_GW:] + b_i[:, cs])
        a = jnp.exp2(tr * kexp[:, cs] + kexp[:, cs])
        v = 1.0 - a * a
        sq = v * lax.rsqrt(jnp.maximum(v, TINY))
        bx = sq * ((ti + 1.0) * xh)
        a_ref[s0:s0 + LRU_SUB, :, cs] = a.reshape(LRU_SUB, SUBLANES, LRU_GW)
        b_ref[s0:s0 + LRU_SUB, :, cs] = bx.reshape(LRU_SUB, SUBLANES, LRU_GW)

    def step(s, h):
      t = LRU_SC - 1 - s if reverse else s
      h = a_ref[t] * h + b_ref[t]
      if reverse:
        o_ref[t] = g1_ref[t] * (hf_ref[t] + h)
      else:
        o_ref[t] = h
      return h

    car_ref[...] = lax.fori_loop(0, LRU_SC, step, car_ref[...], unroll=8)

  return kernel


def _lru(reverse, xr, conv_w, conv_b, wg, bg, lam, extra=()):
  ns = S // LRU_SC
  half = LRU_SC // CONV_LEFT
  blk = (LRU_SC, SUBLANES, LRU_WC)
  chunk = (lambda s: ns - 1 - s) if reverse else (lambda s: s)
  main = lambda g, c, s: (chunk(s), g, c)
  halo_p = lambda g, c, s: (jnp.maximum(chunk(s) * half - 1, 0), g, c)
  halo_n = lambda g, c, s: (jnp.minimum((chunk(s) + 1) * LRU_SC, S - 1), g, c)
  chan = lambda g, c, s: (0, c)
  main_spec = pl.BlockSpec(blk, main)
  return pl.pallas_call(
      _make_lru_kernel(reverse),
      grid=(B // SUBLANES, D // LRU_WC, ns),
      in_specs=[
          main_spec,
          pl.BlockSpec((CONV_LEFT, SUBLANES, LRU_WC), halo_p),
          pl.BlockSpec((1, SUBLANES, LRU_WC), halo_n),
          pl.BlockSpec((CONV_W, LRU_WC), chan),
          pl.BlockSpec((1, LRU_WC), chan),
          pl.BlockSpec((LRU_WC // LRU_GW, LRU_GW, 2 * LRU_GW), lambda g, c, s: (c, 0, 0)),
          pl.BlockSpec((2, LRU_WC), chan),
          pl.BlockSpec((1, LRU_WC), chan),
      ] + [main_spec] * len(extra),
      out_specs=main_spec,
      out_shape=jax.ShapeDtypeStruct((S, B, D), f32),
      scratch_shapes=[pltpu.VMEM(blk, f32), pltpu.VMEM(blk, f32),
                      pltpu.VMEM((SUBLANES, LRU_WC), f32)],
      compiler_params=pltpu.CompilerParams(
          dimension_semantics=("parallel", "parallel", "arbitrary"),
          vmem_limit_bytes=VMEM_LIMIT),
      name="lru_b" if reverse else "lru_f",
  )(xr, xr, xr, conv_w, conv_b, wg, bg, lam, *extra)


def _gate_weights(w):
  per = LRU_GW // LRU_BLOCK
  ncb = D // LRU_GW
  w = w.reshape(2, ncb, per, LRU_BLOCK, LRU_BLOCK)
  eye = jnp.eye(per, dtype=w.dtype)
  dense = w[:, :, :, :, None, :] * eye[None, None, :, None, :, None]
  dense = dense.reshape(2, ncb, LRU_GW, LRU_GW)
  return (0.5 * jnp.concatenate([dense[0], dense[1]], axis=-1)).astype(bf16)


def _key_row0(rg):
  return int(np.clip(QR * rg - WIN_ROWS // 2, 0, ROWS - KR))


def _key_col0(n):
  return int(np.clip(QC * n - WIN_COLS // 2, 0, GRID_W - KC))


def _table_kind(rg):
  return 0 if rg == 0 else (2 if rg == N_RG - 1 else 1)


def _natt_kernel(q_ref, k_ref, v_ref, g2_ref, tbl_ref, o_ref):
  lane_head = lax.broadcasted_iota(jnp.int32, (NQ, HW), 1) // HEAD_DIM
  for rg in range(N_RG):
    k0 = _key_row0(rg)
    for n in range(N_CB):
      c0 = _key_col0(n)
      q_rows = [(QR * rg + rl) * GRID_W + QC * n for rl in range(QR)]
      k_rows = [(k0 + i) * GRID_W + c0 for i in range(KR)]
      qs = jnp.concatenate([q_ref[r:r + QC, :] for r in q_rows], axis=0)
      xq = jnp.concatenate(
          [jnp.where(lane_head == h, qs, jnp.zeros_like(qs)) for h in range(HG)], axis=0)
      kb = jnp.concatenate([k_ref[r:r + KC, :] for r in k_rows], axis=0).astype(bf16)
      vb = jnp.concatenate([v_ref[r:r + KC, :] for r in k_rows], axis=0).astype(bf16)
      s = lax.dot_general(xq, kb, (((1,), (1,)), ((), ())), preferred_element_type=f32)
      s = s + tbl_ref[0, _table_kind(rg), n]
      m = jnp.max(s, axis=1, keepdims=True)
      p = jnp.exp(s - m)
      p = (p * (1.0 / jnp.sum(p, axis=1, keepdims=True))).astype(bf16)
      o = _dot(p, vb)
      acc = jnp.where(lane_head == 0, o[0:NQ], 0.0)
      for h in range(1, HG):
        acc = jnp.where(lane_head == h, o[h * NQ:(h + 1) * NQ], acc)
      for rl, r in enumerate(q_rows):
        o_ref[r:r + QC, :] = g2_ref[r:r + QC, :] * acc[rl * QC:(rl + 1) * QC]


def _natt(q, k, v, g2, tbl):
  ng = N_HEADS // HG
  spec = pl.BlockSpec((None, S, HW), lambda g, b: (b, 0, g))
  return pl.pallas_call(
      _natt_kernel,
      grid=(ng, B),
      in_specs=[
          spec, spec, spec, spec,
          pl.BlockSpec((1, 3, N_CB, HG * NQ, NKEY), lambda g, b: (g, 0, 0, 0, 0)),
      ],
      out_specs=spec,
      out_shape=jax.ShapeDtypeStruct((B, S, D), f32),
      compiler_params=pltpu.CompilerParams(
          dimension_semantics=("parallel", "parallel"), vmem_limit_bytes=VMEM_LIMIT),
      name="natt",
  )(q, k, v, g2, tbl)


def _bias_tables(rpb):
  n_dr, n_dc = 2 * WIN_ROWS - 1, 2 * WIN_COLS - 1
  i = np.arange(KR)[:, None]
  rl = np.arange(QR)[None, :]
  row_hot, row_ok = [], []
  for rg in (0, 1, N_RG - 1):
    r = QR * rg + rl
    rs = np.clip(r - WIN_ROWS // 2, 0, ROWS - WIN_ROWS)
    krow = _key_row0(rg) + i
    row_ok.append((krow >= rs) & (krow < rs + WIN_ROWS))
    row_hot.append(np.eye(n_dr)[np.clip(krow - r + WIN_ROWS - 1, 0, n_dr - 1)])
  c = np.arange(KC)[:, None]
  ql = np.arange(QC)[None, :]
  col_hot, col_ok = [], []
  for n in range(N_CB):
    qc = QC * n + ql
    kc = _key_col0(n) + c
    cs = np.clip(qc - WIN_COLS // 2, 0, GRID_W - WIN_COLS)
    col_ok.append((kc >= cs) & (kc < cs + WIN_COLS))
    col_hot.append(np.eye(n_dc)[np.clip(kc - qc, -(WIN_COLS - 1), WIN_COLS - 1) + WIN_COLS - 1])
  row_ok, col_ok = np.stack(row_ok), np.stack(col_ok)
  ok = (row_ok.transpose(0, 2, 1)[:, None, None, :, None, :, None]
        & col_ok.transpose(0, 2, 1)[None, :, None, None, :, None, :])
  ng = N_HEADS // HG
  bias = jnp.einsum(
      "kira,ghab,ncqb->gknhrqic", jnp.asarray(np.stack(row_hot), f32),
      rpb.astype(f32).reshape(ng, HG, n_dr, n_dc), jnp.asarray(np.stack(col_hot), f32),
      precision=lax.Precision.HIGHEST)
  bias = jnp.where(ok[None], bias, NEG)
  return bias.reshape(ng, 3, N_CB, HG * NQ, NKEY)


def _ffn2_kernel(x1_ref, yl_ref, ya_ref, wo_ref, g_ref, wgu_ref, wdn_ref, o_ref, act_ref, perm_ref):
  rows = B * TS_OUT
  y_lru = _swap_rows(yl_ref[...].reshape(rows, D), perm_ref, TS_OUT, B, PITCH_OUT)
  y = y_lru + ya_ref[...].reshape(rows, D)
  x2 = x1_ref[...].reshape(rows, D) + _dot(y.astype(bf16), wo_ref[...])
  out = x2 + FFN_RES * _swiglu(x2, g_ref, wgu_ref, wdn_ref, act_ref)
  o_ref[...] = out.reshape(B, TS_OUT, D)


def _ffn2(x1, yl, ya, wo, g, wgu, wdn):
  bm_spec = pl.BlockSpec((B, TS_OUT, D), lambda j: (0, j, 0))
  tm_spec = pl.BlockSpec((TS_OUT, B, D), lambda j: (j, 0, 0))
  return pl.pallas_call(
      _ffn2_kernel,
      grid=(S // TS_OUT,),
      in_specs=[bm_spec, tm_spec, bm_spec,
                _const_spec((D, D)),
                _const_spec((1, D)),
                _const_spec((D, 2 * D_FF)),
                _const_spec((D_FF, D))],
      out_specs=bm_spec,
      out_shape=jax.ShapeDtypeStruct((B, S, D), f32),
      scratch_shapes=[pltpu.VMEM((B * TS_OUT, D_FF), bf16),
                      pltpu.VMEM((D // LANES, TS_OUT * PITCH_OUT, LANES), f32)],
      compiler_params=pltpu.CompilerParams(
          dimension_semantics=("parallel",), vmem_limit_bytes=VMEM_LIMIT),
      name="ffn2",
  )(x1, yl, ya, wo, g, wgu, wdn)


def kernel(x, norm_ffn1, w_ffn1_gu, w_ffn1_down, norm_mix, w_in, conv_w, conv_b,
           lru_w_gates, lru_b_gates, lru_lambda, q_norm, k_norm, rel_pos_bias, w_out,
           norm_ffn2, w_ffn2_gu, w_ffn2_down):
  assert x.shape == (B, S, D) and norm_ffn1.shape[0] == 1
  l = 0
  row = lambda a: a.reshape(1, D).astype(f32)
  ones_bd = jnp.asarray(
      np.kron(np.eye(MXU_N // HEAD_DIM), np.ones((HEAD_DIM, HEAD_DIM))), dtype=bf16)

  x1 = _ffn1(x.reshape(T, D), row(norm_ffn1[l]), w_ffn1_gu[l].astype(bf16),
             w_ffn1_down[l].astype(bf16)).reshape(B, S, D)

  xr, g1, g2, q, k, v = _inproj(
      x1, row(norm_mix[l]), w_in[l].astype(bf16),
      row(jnp.tile(q_norm[l], N_HEADS)), row(jnp.tile(k_norm[l], N_HEADS)), ones_bd)

  cw, cb = conv_w[l].astype(f32), row(conv_b[l])
  lru_args = lambda d: (xr, cw, cb, _gate_weights(lru_w_gates[l, d]),
                        lru_b_gates[l, d].astype(f32), row(lru_lambda[l, d]))
  hf = _lru(False, *lru_args(0))
  yl = _lru(True, *lru_args(1), extra=(hf, g1))

  ya = _natt(q, k, v, g2, _bias_tables(rel_pos_bias[l]))

  return _ffn2(x1, yl, ya, w_out[l].astype(bf16),
               row(norm_ffn2[l]), w_ffn2_gu[l].astype(bf16), w_ffn2_down[l].astype(bf16))
```

```python
import jax
import jax.numpy as jnp
import numpy as np
from jax import lax
from jax.experimental import pallas as pl
from jax.experimental.pallas import tpu as pltpu

D = 1024
B = 32
S = 2048
T = B * S
GRID_W = 64
ROWS = S // GRID_W
N_HEADS = 16
HEAD_DIM = 64
WIN_ROWS = 8
WIN_COLS = 16
LRU_BLOCK = 64
LRU_C = 8.0
CONV_W = 4
CONV_LEFT = 2
D_FF = 2816
FFN_RES = 0.5
EPS = 1e-6
NEG = -1e30
LOG2E = float(np.log2(np.e))
TINY = 1e-37

SUBLANES = 8
BF16_ROWS = 16
LANES = 128
MXU_N = 256
VMEM_LIMIT = 56 * 1024 * 1024

TM_FFN = 512
TS_PROJ = 16
PITCH_PROJ = 24
TS_OUT = 16
PITCH_OUT = 40
FF_CHUNK = MXU_N
LRU_SC = 256
LRU_WC = 512
LRU_GW = MXU_N
LRU_SUB = 32
HG = 4
HW = HG * HEAD_DIM
QR = 4
QC = 16
KR = 12
KC = 32
NQ = QR * QC
NKEY = KR * KC
N_CB = GRID_W // QC
N_RG = ROWS // QR

f32 = jnp.float32
bf16 = jnp.bfloat16


def _sigmoid(x):
  return 0.5 * jnp.tanh(0.5 * x) + 0.5


def _softplus(x):
  return jnp.maximum(x, 0.0) + jnp.log1p(jnp.exp(-jnp.abs(x)))


def _gelu_tanh(x, scale=1.0):
  c = float(np.sqrt(2.0 / np.pi))
  return (0.5 * scale) * x * (1.0 + jnp.tanh(c * (x + 0.044715 * (x * x * x))))


def _rmsnorm(x, g):
  ms = jnp.mean(x * x, axis=-1, keepdims=True)
  return (x * lax.rsqrt(ms + EPS)) * g


def _dot(a, b):
  return jnp.dot(a, b, preferred_element_type=f32)


def _const_spec(shape):
  nd = len(shape)
  return pl.BlockSpec(shape, lambda *_: (0,) * nd, pipeline_mode=pl.Buffered(1))


def _swiglu(x, g_ref, wgu_ref, wdn_ref, act_ref):
  h = _rmsnorm(x, g_ref[...]).astype(bf16)
  for c in range(D_FF // FF_CHUNK):
    lo = c * FF_CHUNK
    gate = _dot(h, wgu_ref[:, lo:lo + FF_CHUNK])
    up = _dot(h, wgu_ref[:, D_FF + lo:D_FF + lo + FF_CHUNK])
    act_ref[:, lo:lo + FF_CHUNK] = (gate * _sigmoid(gate) * up).astype(bf16)
  return _dot(act_ref[...], wdn_ref[...])


def _swap_rows(x, perm_ref, m, n, pitch):
  for i in range(m):
    for c in range(D // LANES):
      perm_ref[c, i * pitch:i * pitch + n, :] = x[i * n:(i + 1) * n, c * LANES:(c + 1) * LANES]
  return jnp.concatenate(
      [jnp.concatenate([perm_ref[c, pl.ds(j, m, stride=pitch), :] for c in range(D // LANES)], axis=-1)
       for j in range(n)], axis=0)


def _ffn1_kernel(x_ref, g_ref, wgu_ref, wdn_ref, o_ref, act_ref):
  x = x_ref[...]
  o_ref[...] = x + FFN_RES * _swiglu(x, g_ref, wgu_ref, wdn_ref, act_ref)


def _ffn1(x2d, g, wgu, wdn):
  row = lambda i: (i, 0)
  return pl.pallas_call(
      _ffn1_kernel,
      grid=(T // TM_FFN,),
      in_specs=[
          pl.BlockSpec((TM_FFN, D), row),
          _const_spec((1, D)),
          _const_spec((D, 2 * D_FF)),
          _const_spec((D_FF, D)),
      ],
      out_specs=pl.BlockSpec((TM_FFN, D), row),
      out_shape=jax.ShapeDtypeStruct((T, D), f32),
      scratch_shapes=[pltpu.VMEM((TM_FFN, D_FF), bf16)],
      compiler_params=pltpu.CompilerParams(
          dimension_semantics=("parallel",), vmem_limit_bytes=VMEM_LIMIT),
      name="ffn1",
  )(x2d, g, wgu, wdn)


def _head_rms(x, ones_ref):
  parts = []
  for j in range(D // MXU_N):
    sq = x[:, j * MXU_N:(j + 1) * MXU_N]
    parts.append(_dot((sq * sq).astype(bf16), ones_ref[...]))
  return jnp.concatenate(parts, axis=-1) * (1.0 / HEAD_DIM)


def _inproj_kernel(x_ref, g_ref, w_ref, qn_ref, kn_ref, ones_ref,
                   xr_ref, g1_ref, g2_ref, q_ref, k_ref, v_ref, perm_ref):
  h = _rmsnorm(x_ref[...].reshape(B * TS_PROJ, D), g_ref[...])
  h_bm = h.astype(bf16)
  h_tm = _swap_rows(h, perm_ref, B, TS_PROJ, PITCH_PROJ).astype(bf16)

  def col(lhs, i):
    return _dot(lhs, w_ref[:, i * D:(i + 1) * D])

  tm = lambda a: a.reshape(TS_PROJ, B, D)
  bm = lambda a: a.reshape(B, TS_PROJ, D)
  xr_ref[...] = tm(col(h_tm, 0))
  g1_ref[...] = tm(_gelu_tanh(col(h_tm, 1), scale=0.5) * _sigmoid(col(h_tm, 5)))
  g2_ref[...] = bm(_sigmoid(col(h_bm, 6)))
  q = col(h_bm, 2)
  q = (q * lax.rsqrt(_head_rms(q, ones_ref) + EPS)) * qn_ref[...] * (HEAD_DIM ** -0.5 * LOG2E)
  q_ref[...] = bm(q.astype(bf16))
  k = col(h_bm, 3)
  k_ref[...] = bm((k * lax.rsqrt(_head_rms(k, ones_ref) + EPS)) * kn_ref[...])
  v_ref[...] = bm(col(h_bm, 4))


def _inproj(x1, g, w_in, qn, kn, ones_bd):
  bm_spec = pl.BlockSpec((B, TS_PROJ, D), lambda j: (0, j, 0))
  tm_spec = pl.BlockSpec((TS_PROJ, B, D), lambda j: (j, 0, 0))
  bm_out = lambda dt: jax.ShapeDtypeStruct((B, S, D), dt)
  tm_out = jax.ShapeDtypeStruct((S, B, D), f32)
  return pl.pallas_call(
      _inproj_kernel,
      grid=(S // TS_PROJ,),
      in_specs=[
          bm_spec,
          _const_spec((1, D)),
          _const_spec((D, 7 * D)),
          _const_spec((1, D)),
          _const_spec((1, D)),
          _const_spec((MXU_N, MXU_N)),
      ],
      out_specs=[tm_spec, tm_spec, bm_spec, bm_spec, bm_spec, bm_spec],
      out_shape=[tm_out, tm_out, bm_out(f32), bm_out(bf16), bm_out(f32), bm_out(f32)],
      scratch_shapes=[pltpu.VMEM((D // LANES, B * PITCH_PROJ, LANES), f32)],
      compiler_params=pltpu.CompilerParams(
          dimension_semantics=("parallel",), vmem_limit_bytes=VMEM_LIMIT),
      name="inproj",
  )(x1, g, w_in, qn, kn, ones_bd)


def _make_lru_kernel(reverse):
  def kernel(x_ref, xp_ref, xn_ref, cw_ref, cb_ref, wg_ref, bg_ref, lam_ref, *rest):
    if reverse:
      hf_ref, g1_ref, o_ref, a_ref, b_ref, car_ref = rest
    else:
      o_ref, a_ref, b_ref, car_ref = rest
    sc = pl.program_id(2)
    ns = pl.num_programs(2)
    chunk = ns - 1 - sc if reverse else sc

    @pl.when(sc == 0)
    def _():
      car_ref[...] = jnp.zeros_like(car_ref)

    prev = jnp.where(chunk == 0, 0.0, xp_ref[...])
    nxt = jnp.where(chunk == ns - 1, 0.0, xn_ref[...])
    kexp = (-0.5 * LRU_C * LOG2E) * _softplus(-lam_ref[...])
    b_r = 0.5 * bg_ref[0:1, :]
    b_i = 0.5 * bg_ref[1:2, :]

    def rows(lo, hi):
      parts = []
      if lo < 0:
        parts.append(prev[lo + CONV_LEFT:min(hi, 0) + CONV_LEFT])
      if hi > 0 and lo < LRU_SC:
        parts.append(x_ref[max(lo, 0):min(hi, LRU_SC)])
      if hi > LRU_SC:
        parts.append(nxt)
      return parts[0] if len(parts) == 1 else jnp.concatenate(parts, axis=0)

    for j in range(LRU_SC // LRU_SUB):
      s0 = j * LRU_SUB
      xc = cb_ref[...]
      for k in range(CONV_W):
        lo = s0 + k - CONV_LEFT
        xc = xc + rows(lo, lo + LRU_SUB) * cw_ref[k:k + 1, :]
      xc = xc.reshape(LRU_SUB * SUBLANES, LRU_WC)
      for hh in range(LRU_WC // LRU_GW):
        cs = slice(hh * LRU_GW, (hh + 1) * LRU_GW)
        xh = xc[:, cs]
        g = _dot(xh.astype(bf16), wg_ref[hh])
        tr = jnp.tanh(g[:, :LRU_GW] + b_r[:, cs])
        ti = jnp.tanh(g[:, LRU_GW:] + b_i[:, cs])
        a = jnp.exp2(tr * kexp[:, cs] + kexp[:, cs])
        v = 1.0 - a * a
        sq = v * lax.rsqrt(jnp.maximum(v, TINY))
        bx = sq * ((ti + 1.0) * xh)
        a_ref[s0:s0 + LRU_SUB, :, cs] = a.reshape(LRU_SUB, SUBLANES, LRU_GW)
        b_ref[s0:s0 + LRU_SUB, :, cs] = bx.reshape(LRU_SUB, SUBLANES, LRU_GW)

    def step(s, h):
      t = LRU_SC - 1 - s if reverse else s
      h = a_ref[t] * h + b_ref[t]
      if reverse:
        o_ref[t] = g1_ref[t] * (hf_ref[t] + h)
      else:
        o_ref[t] = h
      return h

    car_ref[...] = lax.fori_loop(0, LRU_SC, step, car_ref[...], unroll=8)

  return kernel


def _lru(reverse, xr, conv_w, conv_b, wg, bg, lam, extra=()):
  ns = S // LRU_SC
  half = LRU_SC // CONV_LEFT
  blk = (LRU_SC, SUBLANES, LRU_WC)
  chunk = (lambda s: ns - 1 - s) if reverse else (lambda s: s)
  main = lambda g, c, s: (chunk(s), g, c)
  halo_p = lambda g, c, s: (jnp.maximum(chunk(s) * half - 1, 0), g, c)
  halo_n = lambda g, c, s: (jnp.minimum((chunk(s) + 1) * LRU_SC, S - 1), g, c)
  chan = lambda g, c, s: (0, c)
  main_spec = pl.BlockSpec(blk, main)
  return pl.pallas_call(
      _make_lru_kernel(reverse),
      grid=(B // SUBLANES, D // LRU_WC, ns),
      in_specs=[
          main_spec,
          pl.BlockSpec((CONV_LEFT, SUBLANES, LRU_WC), halo_p),
          pl.BlockSpec((1, SUBLANES, LRU_WC), halo_n),
          pl.BlockSpec((CONV_W, LRU_WC), chan),
          pl.BlockSpec((1, LRU_WC), chan),
          pl.BlockSpec((LRU_WC // LRU_GW, LRU_GW, 2 * LRU_GW), lambda g, c, s: (c, 0, 0)),
          pl.BlockSpec((2, LRU_WC), chan),
          pl.BlockSpec((1, LRU_WC), chan),
      ] + [main_spec] * len(extra),
      out_specs=main_spec,
      out_shape=jax.ShapeDtypeStruct((S, B, D), f32),
      scratch_shapes=[pltpu.VMEM(blk, f32), pltpu.VMEM(blk, f32),
                      pltpu.VMEM((SUBLANES, LRU_WC), f32)],
      compiler_params=pltpu.CompilerParams(
          dimension_semantics=("parallel", "parallel", "arbitrary"),
          vmem_limit_bytes=VMEM_LIMIT),
      name="lru_b" if reverse else "lru_f",
  )(xr, xr, xr, conv_w, conv_b, wg, bg, lam, *extra)


def _gate_weights(w):
  per = LRU_GW // LRU_BLOCK
  ncb = D // LRU_GW
  w = w.reshape(2, ncb, per, LRU_BLOCK, LRU_BLOCK)
  eye = jnp.eye(per, dtype=w.dtype)
  dense = w[:, :, :, :, None, :] * eye[None, None, :, None, :, None]
  dense = dense.reshape(2, ncb, LRU_GW, LRU_GW)
  return (0.5 * jnp.concatenate([dense[0], dense[1]], axis=-1)).astype(bf16)


def _key_row0(rg):
  return int(np.clip(QR * rg - WIN_ROWS // 2, 0, ROWS - KR))


def _key_col0(n):
  return int(np.clip(QC * n - WIN_COLS // 2, 0, GRID_W - KC))


def _table_kind(rg):
  return 0 if rg == 0 else (2 if rg == N_RG - 1 else 1)


def _natt_kernel(q_ref, k_ref, v_ref, g2_ref, tbl_ref, o_ref, k0_ref, k8_ref, v0_ref, v8_ref):
  half = BF16_ROWS // 2
  for src, dst0, dst8 in ((k_ref, k0_ref, k8_ref), (v_ref, v0_ref, v8_ref)):
    dst0[...] = src[...].astype(bf16)
    dst8[0:S - BF16_ROWS, :] = src[half:S - half, :].astype(bf16)

  def key_block(ref0, ref8, rows):
    parts = []
    for r in rows:
      parts.append(ref0[r:r + KC, :] if r % BF16_ROWS == 0 else ref8[r - half:r - half + KC, :])
    return jnp.concatenate(parts, axis=0)

  lane_head = lax.broadcasted_iota(jnp.int32, (NQ, HW), 1) // HEAD_DIM
  for rg in range(N_RG):
    k0 = _key_row0(rg)
    for n in range(N_CB):
      c0 = _key_col0(n)
      q_rows = [(QR * rg + rl) * GRID_W + QC * n for rl in range(QR)]
      k_rows = [(k0 + i) * GRID_W + c0 for i in range(KR)]
      qs = jnp.concatenate([q_ref[r:r + QC, :] for r in q_rows], axis=0)
      xq = jnp.concatenate(
          [jnp.where(lane_head == h, qs, jnp.zeros_like(qs)) for h in range(HG)], axis=0)
      kb = key_block(k0_ref, k8_ref, k_rows)
      vb = key_block(v0_ref, v8_ref, k_rows)
      s = lax.dot_general(xq, kb, (((1,), (1,)), ((), ())), preferred_element_type=f32)
      s = s + tbl_ref[0, _table_kind(rg), n]
      p = jnp.exp2(s - jnp.max(s, axis=1, keepdims=True))
      inv = 1.0 / jnp.sum(p, axis=1, keepdims=True)
      o = _dot(p.astype(bf16), vb) * inv
      acc = jnp.where(lane_head == 0, o[0:NQ], 0.0)
      for h in range(1, HG):
        acc = jnp.where(lane_head == h, o[h * NQ:(h + 1) * NQ], acc)
      for rl, r in enumerate(q_rows):
        o_ref[r:r + QC, :] = g2_ref[r:r + QC, :] * acc[rl * QC:(rl + 1) * QC]


def _natt(q, k, v, g2, tbl):
  ng = N_HEADS // HG
  spec = pl.BlockSpec((None, S, HW), lambda g, b: (b, 0, g))
  return pl.pallas_call(
      _natt_kernel,
      grid=(ng, B),
      in_specs=[
          spec, spec, spec, spec,
          pl.BlockSpec((1, 3, N_CB, HG * NQ, NKEY), lambda g, b: (g, 0, 0, 0, 0)),
      ],
      out_specs=spec,
      out_shape=jax.ShapeDtypeStruct((B, S, D), f32),
      scratch_shapes=[pltpu.VMEM((S, HW), bf16)] * 4,
      compiler_params=pltpu.CompilerParams(
          dimension_semantics=("parallel", "parallel"), vmem_limit_bytes=VMEM_LIMIT),
      name="natt",
  )(q, k, v, g2, tbl)


def _bias_tables(rpb):
  n_dr, n_dc = 2 * WIN_ROWS - 1, 2 * WIN_COLS - 1
  i = np.arange(KR)[:, None]
  rl = np.arange(QR)[None, :]
  row_hot, row_ok = [], []
  for rg in (0, 1, N_RG - 1):
    r = QR * rg + rl
    rs = np.clip(r - WIN_ROWS // 2, 0, ROWS - WIN_ROWS)
    krow = _key_row0(rg) + i
    row_ok.append((krow >= rs) & (krow < rs + WIN_ROWS))
    row_hot.append(np.eye(n_dr)[np.clip(krow - r + WIN_ROWS - 1, 0, n_dr - 1)])
  c = np.arange(KC)[:, None]
  ql = np.arange(QC)[None, :]
  col_hot, col_ok = [], []
  for n in range(N_CB):
    qc = QC * n + ql
    kc = _key_col0(n) + c
    cs = np.clip(qc - WIN_COLS // 2, 0, GRID_W - WIN_COLS)
    col_ok.append((kc >= cs) & (kc < cs + WIN_COLS))
    col_hot.append(np.eye(n_dc)[np.clip(kc - qc, -(WIN_COLS - 1), WIN_COLS - 1) + WIN_COLS - 1])
  row_ok, col_ok = np.stack(row_ok), np.stack(col_ok)
  ok = (row_ok.transpose(0, 2, 1)[:, None, None, :, None, :, None]
        & col_ok.transpose(0, 2, 1)[None, :, None, None, :, None, :])
  ng = N_HEADS // HG
  bias = jnp.einsum(
      "kira,ghab,ncqb->gknhrqic", jnp.asarray(np.stack(row_hot), f32),
      rpb.astype(f32).reshape(ng, HG, n_dr, n_dc), jnp.asarray(np.stack(col_hot), f32),
      precision=lax.Precision.HIGHEST)
  bias = jnp.where(ok[None], LOG2E * bias, NEG)
  return bias.reshape(ng, 3, N_CB, HG * NQ, NKEY)


def _ffn2_kernel(x1_ref, yl_ref, ya_ref, wo_ref, g_ref, wgu_ref, wdn_ref, o_ref, act_ref, perm_ref):
  rows = B * TS_OUT
  y_lru = _swap_rows(yl_ref[...].reshape(rows, D), perm_ref, TS_OUT, B, PITCH_OUT)
  y = y_lru + ya_ref[...].reshape(rows, D)
  x2 = x1_ref[...].reshape(rows, D) + _dot(y.astype(bf16), wo_ref[...])
  out = x2 + FFN_RES * _swiglu(x2, g_ref, wgu_ref, wdn_ref, act_ref)
  o_ref[...] = out.reshape(B, TS_OUT, D)


def _ffn2(x1, yl, ya, wo, g, wgu, wdn):
  bm_spec = pl.BlockSpec((B, TS_OUT, D), lambda j: (0, j, 0))
  tm_spec = pl.BlockSpec((TS_OUT, B, D), lambda j: (j, 0, 0))
  return pl.pallas_call(
      _ffn2_kernel,
      grid=(S // TS_OUT,),
      in_specs=[bm_spec, tm_spec, bm_spec,
                _const_spec((D, D)),
                _const_spec((1, D)),
                _const_spec((D, 2 * D_FF)),
                _const_spec((D_FF, D))],
      out_specs=bm_spec,
      out_shape=jax.ShapeDtypeStruct((B, S, D), f32),
      scratch_shapes=[pltpu.VMEM((B * TS_OUT, D_FF), bf16),
                      pltpu.VMEM((D // LANES, TS_OUT * PITCH_OUT, LANES), f32)],
      compiler_params=pltpu.CompilerParams(
          dimension_semantics=("parallel",), vmem_limit_bytes=VMEM_LIMIT),
      name="ffn2",
  )(x1, yl, ya, wo, g, wgu, wdn)


def kernel(x, norm_ffn1, w_ffn1_gu, w_ffn1_down, norm_mix, w_in, conv_w, conv_b,
           lru_w_gates, lru_b_gates, lru_lambda, q_norm, k_norm, rel_pos_bias, w_out,
           norm_ffn2, w_ffn2_gu, w_ffn2_down):
  assert x.shape == (B, S, D) and norm_ffn1.shape[0] == 1
  l = 0
  row = lambda a: a.reshape(1, D).astype(f32)
  ones_bd = jnp.asarray(
      np.kron(np.eye(MXU_N // HEAD_DIM), np.ones((HEAD_DIM, HEAD_DIM))), dtype=bf16)

  x1 = _ffn1(x.reshape(T, D), row(norm_ffn1[l]), w_ffn1_gu[l].astype(bf16),
             w_ffn1_down[l].astype(bf16)).reshape(B, S, D)

  xr, g1, g2, q, k, v = _inproj(
      x1, row(norm_mix[l]), w_in[l].astype(bf16),
      row(jnp.tile(q_norm[l], N_HEADS)), row(jnp.tile(k_norm[l], N_HEADS)), ones_bd)

  cw, cb = conv_w[l].astype(f32), row(conv_b[l])
  lru_args = lambda d: (xr, cw, cb, _gate_weights(lru_w_gates[l, d]),
                        lru_b_gates[l, d].astype(f32), row(lru_lambda[l, d]))
  hf = _lru(False, *lru_args(0))
  yl = _lru(True, *lru_args(1), extra=(hf, g1))

  ya = _natt(q, k, v, g2, _bias_tables(rel_pos_bias[l]))

  return _ffn2(x1, yl, ya, w_out[l].astype(bf16),
               row(norm_ffn2[l]), w_ffn2_gu[l].astype(bf16), w_ffn2_down[l].astype(bf16))
```

```python
import jax
import jax.numpy as jnp
import numpy as np
from jax import lax
from jax.experimental import pallas as pl
from jax.experimental.pallas import tpu as pltpu

D = 1024
B = 32
S = 2048
T = B * S
GRID_W = 64
ROWS = S // GRID_W
N_HEADS = 16
HEAD_DIM = 64
WIN_ROWS = 8
WIN_COLS = 16
LRU_BLOCK = 64
LRU_C = 8.0
CONV_W = 4
CONV_LEFT = 2
D_FF = 2816
FFN_RES = 0.5
EPS = 1e-6
NEG = -1e30
LOG2E = float(np.log2(np.e))
TINY = 1e-37

SUBLANES = 8
BF16_ROWS = 16
LANES = 128
MXU_N = 256
VMEM_LIMIT = 56 * 1024 * 1024

TM_FFN = 512
TS_PROJ = 16
PITCH_PROJ = 24
TS_OUT = 16
PITCH_OUT = 40
FF_CHUNK = MXU_N
LRU_SC = 256
LRU_WC = 512
LRU_GW = MXU_N
LRU_SUB = 32
HG = 4
HW = HG * HEAD_DIM
QR = 4
QC = 16
KR = 12
KC = 32
NQ = QR * QC
NKEY = KR * KC
N_CB = GRID_W // QC
N_RG = ROWS // QR

f32 = jnp.float32
bf16 = jnp.bfloat16


def _sigmoid(x):
  return 0.5 * jnp.tanh(0.5 * x) + 0.5


def _softplus(x):
  return jnp.maximum(x, 0.0) + jnp.log1p(jnp.exp(-jnp.abs(x)))


def _gelu_tanh(x, scale=1.0):
  c = float(np.sqrt(2.0 / np.pi))
  return (0.5 * scale) * x * (1.0 + jnp.tanh(c * (x + 0.044715 * (x * x * x))))


def _rmsnorm(x, g):
  ms = jnp.mean(x * x, axis=-1, keepdims=True)
  return (x * lax.rsqrt(ms + EPS)) * g


def _dot(a, b):
  return jnp.dot(a, b, preferred_element_type=f32)


def _const_spec(shape):
  nd = len(shape)
  return pl.BlockSpec(shape, lambda *_: (0,) * nd, pipeline_mode=pl.Buffered(1))


def _swiglu(x, g_ref, wgu_ref, wdn_ref, act_ref):
  h = _rmsnorm(x, g_ref[...]).astype(bf16)
  for c in range(D_FF // FF_CHUNK):
    lo = c * FF_CHUNK
    gate = _dot(h, wgu_ref[:, lo:lo + FF_CHUNK])
    up = _dot(h, wgu_ref[:, D_FF + lo:D_FF + lo + FF_CHUNK])
    act_ref[:, lo:lo + FF_CHUNK] = (gate * _sigmoid(gate) * up).astype(bf16)
  return _dot(act_ref[...], wdn_ref[...])


def _swap_rows(x, perm_ref, m, n, pitch):
  for i in range(m):
    for c in range(D // LANES):
      perm_ref[c, i * pitch:i * pitch + n, :] = x[i * n:(i + 1) * n, c * LANES:(c + 1) * LANES]
  return jnp.concatenate(
      [jnp.concatenate([perm_ref[c, pl.ds(j, m, stride=pitch), :] for c in range(D // LANES)], axis=-1)
       for j in range(n)], axis=0)


def _ffn1_kernel(x_ref, g_ref, wgu_ref, wdn_ref, o_ref, act_ref):
  x = x_ref[...]
  o_ref[...] = x + FFN_RES * _swiglu(x, g_ref, wgu_ref, wdn_ref, act_ref)


def _ffn1(x2d, g, wgu, wdn):
  row = lambda i: (i, 0)
  return pl.pallas_call(
      _ffn1_kernel,
      grid=(T // TM_FFN,),
      in_specs=[
          pl.BlockSpec((TM_FFN, D), row),
          _const_spec((1, D)),
          _const_spec((D, 2 * D_FF)),
          _const_spec((D_FF, D)),
      ],
      out_specs=pl.BlockSpec((TM_FFN, D), row),
      out_shape=jax.ShapeDtypeStruct((T, D), f32),
      scratch_shapes=[pltpu.VMEM((TM_FFN, D_FF), bf16)],
      compiler_params=pltpu.CompilerParams(
          dimension_semantics=("parallel",), vmem_limit_bytes=VMEM_LIMIT),
      name="ffn1",
  )(x2d, g, wgu, wdn)


def _lru_coeffs(xc, wg_ref, bg_ref, lam_ref):
  kexp = (-0.5 * LRU_C * LOG2E) * _softplus(-lam_ref[...])
  b_r = 0.5 * bg_ref[0:1, :]
  b_i = 0.5 * bg_ref[1:2, :]
  a_parts, b_parts = [], []
  for c in range(xc.shape[1] // LRU_GW):
    cs = slice(c * LRU_GW, (c + 1) * LRU_GW)
    xh = xc[:, cs]
    g = _dot(xh.astype(bf16), wg_ref[c])
    tr = jnp.tanh(g[:, :LRU_GW] + b_r[:, cs])
    ti = jnp.tanh(g[:, LRU_GW:] + b_i[:, cs])
    a = jnp.exp2(tr * kexp[:, cs] + kexp[:, cs])
    v = 1.0 - a * a
    sq = v * lax.rsqrt(jnp.maximum(v, TINY))
    a_parts.append(a)
    b_parts.append(sq * ((ti + 1.0) * xh))
  return a_parts, b_parts


def _gate_weights(w):
  per = LRU_GW // LRU_BLOCK
  ncb = D // LRU_GW
  w = w.reshape(2, ncb, per, LRU_BLOCK, LRU_BLOCK)
  eye = jnp.eye(per, dtype=w.dtype)
  dense = w[:, :, :, :, None, :] * eye[None, None, :, None, :, None]
  dense = dense.reshape(2, ncb, LRU_GW, LRU_GW)
  return (0.5 * jnp.concatenate([dense[0], dense[1]], axis=-1)).astype(bf16)


def _inproj_a_kernel(x_ref, xn_ref, g_ref, w_ref, cw_ref, cb_ref, wg_ref, bg_ref, lam_ref,
                     xc_ref, hf_ref, g1_ref, g2_ref, perm_ref, halo_ref, xprev_ref, car_ref):
  j = pl.program_id(0)
  last = pl.num_programs(0) - 1
  rows = B * TS_PROJ

  @pl.when(j == 0)
  def _():
    xprev_ref[...] = jnp.zeros_like(xprev_ref)
    car_ref[...] = jnp.zeros_like(car_ref)

  h = _rmsnorm(x_ref[...].reshape(rows, D), g_ref[...])
  h_bm = h.astype(bf16)
  h_tm = _swap_rows(h, perm_ref, B, TS_PROJ, PITCH_PROJ)
  hn = _rmsnorm(xn_ref[...].reshape(B * SUBLANES, D), g_ref[...])
  for c in range(D // LANES):
    halo_ref[c] = hn[:, c * LANES:(c + 1) * LANES]
  hn0 = jnp.concatenate(
      [halo_ref[c, pl.ds(0, B, stride=SUBLANES), :] for c in range(D // LANES)], axis=-1)

  def col(lhs, i):
    return _dot(lhs, w_ref[:, i * D:(i + 1) * D])

  xr = col(jnp.concatenate([h_tm, hn0], axis=0).astype(bf16), 0)
  nxt = jnp.where(j == last, 0.0, xr[rows:]).reshape(1, B, D)
  xr = xr[:rows].reshape(TS_PROJ, B, D)
  seq = jnp.concatenate([xprev_ref[...], xr, nxt], axis=0)
  xprev_ref[...] = xr[TS_PROJ - CONV_LEFT:]
  xc = cb_ref[...]
  for k in range(CONV_W):
    xc = xc + seq[k:k + TS_PROJ] * cw_ref[k:k + 1, :]
  xc_ref[...] = xc

  h_tm = h_tm.astype(bf16)
  g1_ref[...] = (_gelu_tanh(col(h_tm, 1), scale=0.5) * _sigmoid(col(h_tm, 2))).reshape(TS_PROJ, B, D)
  g2_ref[...] = _sigmoid(col(h_bm, 3)).reshape(B, TS_PROJ, D)

  a_parts, b_parts = _lru_coeffs(xc.reshape(rows, D), wg_ref, bg_ref, lam_ref)
  for c, (a, bx) in enumerate(zip(a_parts, b_parts)):
    cs = slice(c * LRU_GW, (c + 1) * LRU_GW)
    hcur = car_ref[:, cs]
    for s in range(TS_PROJ):
      hcur = a[s * B:(s + 1) * B] * hcur + bx[s * B:(s + 1) * B]
      hf_ref[s, :, cs] = hcur
    car_ref[:, cs] = hcur


def _inproj_a(x1, g, w_a, conv_w, conv_b, wg, bg, lam):
  nh = S // SUBLANES
  bm_spec = pl.BlockSpec((B, TS_PROJ, D), lambda j: (0, j, 0))
  tm_spec = pl.BlockSpec((TS_PROJ, B, D), lambda j: (j, 0, 0))
  halo_spec = pl.BlockSpec(
      (B, SUBLANES, D), lambda j: (0, jnp.minimum((j + 1) * (TS_PROJ // SUBLANES), nh - 1), 0))
  tm_out = jax.ShapeDtypeStruct((S, B, D), f32)
  return pl.pallas_call(
      _inproj_a_kernel,
      grid=(S // TS_PROJ,),
      in_specs=[
          bm_spec,
          halo_spec,
          _const_spec((1, D)),
          _const_spec((D, 4 * D)),
          _const_spec((CONV_W, D)),
          _const_spec((1, D)),
          _const_spec((D // LRU_GW, LRU_GW, 2 * LRU_GW)),
          _const_spec((2, D)),
          _const_spec((1, D)),
      ],
      out_specs=[tm_spec, tm_spec, tm_spec, bm_spec],
      out_shape=[tm_out, tm_out, tm_out, jax.ShapeDtypeStruct((B, S, D), f32)],
      scratch_shapes=[pltpu.VMEM((D // LANES, B * PITCH_PROJ, LANES), f32),
                      pltpu.VMEM((D // LANES, B * SUBLANES, LANES), f32),
                      pltpu.VMEM((CONV_LEFT, B, D), f32),
                      pltpu.VMEM((B, D), f32)],
      compiler_params=pltpu.CompilerParams(
          dimension_semantics=("arbitrary",), vmem_limit_bytes=VMEM_LIMIT),
      name="inproj_a",
  )(x1, x1, g, w_a, conv_w, conv_b, wg, bg, lam)


def _head_rms(x, ones_ref):
  parts = []
  for j in range(D // MXU_N):
    sq = x[:, j * MXU_N:(j + 1) * MXU_N]
    parts.append(_dot((sq * sq).astype(bf16), ones_ref[...]))
  return jnp.concatenate(parts, axis=-1) * (1.0 / HEAD_DIM)


def _inproj_b_kernel(x_ref, g_ref, w_ref, qn_ref, kn_ref, ones_ref, q_ref, k_ref, v_ref):
  h = _rmsnorm(x_ref[...], g_ref[...]).astype(bf16)

  def col(i):
    return _dot(h, w_ref[:, i * D:(i + 1) * D])

  q = col(0)
  q = (q * lax.rsqrt(_head_rms(q, ones_ref) + EPS)) * qn_ref[...] * (HEAD_DIM ** -0.5 * LOG2E)
  q_ref[...] = q.astype(bf16)
  k = col(1)
  k_ref[...] = (k * lax.rsqrt(_head_rms(k, ones_ref) + EPS)) * kn_ref[...]
  v_ref[...] = col(2)


def _inproj_b(x1, g, w_b, qn, kn, ones_bd):
  spec = pl.BlockSpec((TM_FFN, D), lambda i: (i, 0))
  out = lambda dt: jax.ShapeDtypeStruct((T, D), dt)
  return pl.pallas_call(
      _inproj_b_kernel,
      grid=(T // TM_FFN,),
      in_specs=[
          spec,
          _const_spec((1, D)),
          _const_spec((D, 3 * D)),
          _const_spec((1, D)),
          _const_spec((1, D)),
          _const_spec((MXU_N, MXU_N)),
      ],
      out_specs=[spec, spec, spec],
      out_shape=[out(bf16), out(f32), out(f32)],
      compiler_params=pltpu.CompilerParams(
          dimension_semantics=("parallel",), vmem_limit_bytes=VMEM_LIMIT),
      name="inproj_b",
  )(x1, g, w_b, qn, kn, ones_bd)


def _lru_b_kernel(xc_ref, hf_ref, wg_ref, bg_ref, lam_ref, o_ref, a_ref, b_ref, car_ref):
  @pl.when(pl.program_id(2) == 0)
  def _():
    car_ref[...] = jnp.zeros_like(car_ref)

  for j in range(LRU_SC // LRU_SUB):
    s0 = j * LRU_SUB
    xc = xc_ref[s0:s0 + LRU_SUB].reshape(LRU_SUB * SUBLANES, LRU_WC)
    a_parts, b_parts = _lru_coeffs(xc, wg_ref, bg_ref, lam_ref)
    for c, (a, bx) in enumerate(zip(a_parts, b_parts)):
      cs = slice(c * LRU_GW, (c + 1) * LRU_GW)
      a_ref[s0:s0 + LRU_SUB, :, cs] = a.reshape(LRU_SUB, SUBLANES, LRU_GW)
      b_ref[s0:s0 + LRU_SUB, :, cs] = bx.reshape(LRU_SUB, SUBLANES, LRU_GW)

  def step(s, h):
    t = LRU_SC - 1 - s
    h = a_ref[t] * h + b_ref[t]
    o_ref[t] = hf_ref[t] + h
    return h

  car_ref[...] = lax.fori_loop(0, LRU_SC, step, car_ref[...], unroll=8)


def _lru_b(xc, hf, wg, bg, lam):
  ns = S // LRU_SC
  blk = (LRU_SC, SUBLANES, LRU_WC)
  main_spec = pl.BlockSpec(blk, lambda g, c, s: (ns - 1 - s, g, c))
  chan = lambda g, c, s: (0, c)
  return pl.pallas_call(
      _lru_b_kernel,
      grid=(B // SUBLANES, D // LRU_WC, ns),
      in_specs=[
          main_spec,
          main_spec,
          pl.BlockSpec((LRU_WC // LRU_GW, LRU_GW, 2 * LRU_GW), lambda g, c, s: (c, 0, 0)),
          pl.BlockSpec((2, LRU_WC), chan),
          pl.BlockSpec((1, LRU_WC), chan),
      ],
      out_specs=main_spec,
      out_shape=jax.ShapeDtypeStruct((S, B, D), f32),
      scratch_shapes=[pltpu.VMEM(blk, f32), pltpu.VMEM(blk, f32),
                      pltpu.VMEM((SUBLANES, LRU_WC), f32)],
      compiler_params=pltpu.CompilerParams(
          dimension_semantics=("parallel", "parallel", "arbitrary"),
          vmem_limit_bytes=VMEM_LIMIT),
      name="lru_b",
  )(xc, hf, wg, bg, lam)


def _key_row0(rg):
  return int(np.clip(QR * rg - WIN_ROWS // 2, 0, ROWS - KR))


def _key_col0(n):
  return int(np.clip(QC * n - WIN_COLS // 2, 0, GRID_W - KC))


def _table_kind(rg):
  return 0 if rg == 0 else (2 if rg == N_RG - 1 else 1)


def _natt_kernel(q_ref, k_ref, v_ref, g2_ref, tbl_ref, o_ref, k0_ref, k8_ref, v0_ref, v8_ref):
  half = BF16_ROWS // 2
  for src, dst0, dst8 in ((k_ref, k0_ref, k8_ref), (v_ref, v0_ref, v8_ref)):
    dst0[...] = src[...].astype(bf16)
    dst8[0:S - BF16_ROWS, :] = src[half:S - half, :].astype(bf16)

  def key_block(ref0, ref8, rows):
    parts = []
    for r in rows:
      parts.append(ref0[r:r + KC, :] if r % BF16_ROWS == 0 else ref8[r - half:r - half + KC, :])
    return jnp.concatenate(parts, axis=0)

  lane_head = lax.broadcasted_iota(jnp.int32, (NQ, HW), 1) // HEAD_DIM
  for rg in range(N_RG):
    k0 = _key_row0(rg)
    for n in range(N_CB):
      c0 = _key_col0(n)
      q_rows = [(QR * rg + rl) * GRID_W + QC * n for rl in range(QR)]
      k_rows = [(k0 + i) * GRID_W + c0 for i in range(KR)]
      qs = jnp.concatenate([q_ref[r:r + QC, :] for r in q_rows], axis=0)
      xq = jnp.concatenate(
          [jnp.where(lane_head == h, qs, jnp.zeros_like(qs)) for h in range(HG)], axis=0)
      kb = key_block(k0_ref, k8_ref, k_rows)
      vb = key_block(v0_ref, v8_ref, k_rows)
      s = lax.dot_general(xq, kb, (((1,), (1,)), ((), ())), preferred_element_type=f32)
      s = s + tbl_ref[0, _table_kind(rg), n]
      p = jnp.exp2(s - jnp.max(s, axis=1, keepdims=True))
      inv = 1.0 / jnp.sum(p, axis=1, keepdims=True)
      o = _dot(p.astype(bf16), vb) * inv
      acc = jnp.where(lane_head == 0, o[0:NQ], 0.0)
      for h in range(1, HG):
        acc = jnp.where(lane_head == h, o[h * NQ:(h + 1) * NQ], acc)
      for rl, r in enumerate(q_rows):
        o_ref[r:r + QC, :] = g2_ref[r:r + QC, :] * acc[rl * QC:(rl + 1) * QC]


def _natt(q, k, v, g2, tbl):
  ng = N_HEADS // HG
  spec = pl.BlockSpec((None, S, HW), lambda g, b: (b, 0, g))
  return pl.pallas_call(
      _natt_kernel,
      grid=(ng, B),
      in_specs=[
          spec, spec, spec, spec,
          pl.BlockSpec((1, 3, N_CB, HG * NQ, NKEY), lambda g, b: (g, 0, 0, 0, 0)),
      ],
      out_specs=spec,
      out_shape=jax.ShapeDtypeStruct((B, S, D), f32),
      scratch_shapes=[pltpu.VMEM((S, HW), bf16)] * 4,
      compiler_params=pltpu.CompilerParams(
          dimension_semantics=("parallel", "parallel"), vmem_limit_bytes=VMEM_LIMIT),
      name="natt",
  )(q, k, v, g2, tbl)


def _bias_tables(rpb):
  n_dr, n_dc = 2 * WIN_ROWS - 1, 2 * WIN_COLS - 1
  i = np.arange(KR)[:, None]
  rl = np.arange(QR)[None, :]
  row_hot, row_ok = [], []
  for rg in (0, 1, N_RG - 1):
    r = QR * rg + rl
    rs = np.clip(r - WIN_ROWS // 2, 0, ROWS - WIN_ROWS)
    krow = _key_row0(rg) + i
    row_ok.append((krow >= rs) & (krow < rs + WIN_ROWS))
    row_hot.append(np.eye(n_dr)[np.clip(krow - r + WIN_ROWS - 1, 0, n_dr - 1)])
  c = np.arange(KC)[:, None]
  ql = np.arange(QC)[None, :]
  col_hot, col_ok = [], []
  for n in range(N_CB):
    qc = QC * n + ql
    kc = _key_col0(n) + c
    cs = np.clip(qc - WIN_COLS // 2, 0, GRID_W - WIN_COLS)
    col_ok.append((kc >= cs) & (kc < cs + WIN_COLS))
    col_hot.append(np.eye(n_dc)[np.clip(kc - qc, -(WIN_COLS - 1), WIN_COLS - 1) + WIN_COLS - 1])
  row_ok, col_ok = np.stack(row_ok), np.stack(col_ok)
  ok = (row_ok.transpose(0, 2, 1)[:, None, None, :, None, :, None]
        & col_ok.transpose(0, 2, 1)[None, :, None, None, :, None, :])
  ng = N_HEADS // HG
  bias = jnp.einsum(
      "kira,ghab,ncqb->gknhrqic", jnp.asarray(np.stack(row_hot), f32),
      rpb.astype(f32).reshape(ng, HG, n_dr, n_dc), jnp.asarray(np.stack(col_hot), f32),
      precision=lax.Precision.HIGHEST)
  bias = jnp.where(ok[None], LOG2E * bias, NEG)
  return bias.reshape(ng, 3, N_CB, HG * NQ, NKEY)


def _ffn2_kernel(x1_ref, hs_ref, g1_ref, ya_ref, wo_ref, g_ref, wgu_ref, wdn_ref,
                 o_ref, act_ref, perm_ref):
  rows = B * TS_OUT
  y_lru = (g1_ref[...] * hs_ref[...]).reshape(rows, D)
  y_lru = _swap_rows(y_lru, perm_ref, TS_OUT, B, PITCH_OUT)
  y = y_lru + ya_ref[...].reshape(rows, D)
  x2 = x1_ref[...].reshape(rows, D) + _dot(y.astype(bf16), wo_ref[...])
  out = x2 + FFN_RES * _swiglu(x2, g_ref, wgu_ref, wdn_ref, act_ref)
  o_ref[...] = out.reshape(B, TS_OUT, D)


def _ffn2(x1, hs, g1, ya, wo, g, wgu, wdn):
  bm_spec = pl.BlockSpec((B, TS_OUT, D), lambda j: (0, j, 0))
  tm_spec = pl.BlockSpec((TS_OUT, B, D), lambda j: (j, 0, 0))
  return pl.pallas_call(
      _ffn2_kernel,
      grid=(S // TS_OUT,),
      in_specs=[bm_spec, tm_spec, tm_spec, bm_spec,
                _const_spec((D, D)),
                _const_spec((1, D)),
                _const_spec((D, 2 * D_FF)),
                _const_spec((D_FF, D))],
      out_specs=bm_spec,
      out_shape=jax.ShapeDtypeStruct((B, S, D), f32),
      scratch_shapes=[pltpu.VMEM((B * TS_OUT, D_FF), bf16),
                      pltpu.VMEM((D // LANES, TS_OUT * PITCH_OUT, LANES), f32)],
      compiler_params=pltpu.CompilerParams(
          dimension_semantics=("parallel",), vmem_limit_bytes=VMEM_LIMIT),
      name="ffn2",
  )(x1, hs, g1, ya, wo, g, wgu, wdn)


def kernel(x, norm_ffn1, w_ffn1_gu, w_ffn1_down, norm_mix, w_in, conv_w, conv_b,
           lru_w_gates, lru_b_gates, lru_lambda, q_norm, k_norm, rel_pos_bias, w_out,
           norm_ffn2, w_ffn2_gu, w_ffn2_down):
  assert x.shape == (B, S, D) and norm_ffn1.shape[0] == 1
  l = 0
  row = lambda a: a.reshape(1, D).astype(f32)
  ones_bd = jnp.asarray(
      np.kron(np.eye(MXU_N // HEAD_DIM), np.ones((HEAD_DIM, HEAD_DIM))), dtype=bf16)
  wi = w_in[l].astype(bf16)
  w_a = jnp.concatenate([wi[:, :2 * D], wi[:, 5 * D:]], axis=1)
  w_b = wi[:, 2 * D:5 * D]

  x1 = _ffn1(x.reshape(T, D), row(norm_ffn1[l]), w_ffn1_gu[l].astype(bf16),
             w_ffn1_down[l].astype(bf16))

  gates = lambda d: (_gate_weights(lru_w_gates[l, d]), lru_b_gates[l, d].astype(f32),
                     row(lru_lambda[l, d]))
  xc, hf, g1, g2 = _inproj_a(x1.reshape(B, S, D), row(norm_mix[l]), w_a,
                             conv_w[l].astype(f32), row(conv_b[l]), *gates(0))
  q, k, v = _inproj_b(x1, row(norm_mix[l]), w_b,
                      row(jnp.tile(q_norm[l], N_HEADS)), row(jnp.tile(k_norm[l], N_HEADS)), ones_bd)
  hs = _lru_b(xc, hf, *gates(1))

  bsd = lambda a: a.reshape(B, S, D)
  ya = _natt(bsd(q), bsd(k), bsd(v), g2, _bias_tables(rel_pos_bias[l]))

  return _ffn2(bsd(x1), hs, g1, ya, w_out[l].astype(bf16),
               row(norm_ffn2[l]), w_ffn2_gu[l].astype(bf16), w_ffn2_down[l].astype(bf16))
```

```python
import jax
import jax.numpy as jnp
import numpy as np
from jax import lax
from jax.experimental import pallas as pl
from jax.experimental.pallas import tpu as pltpu

D = 1024
B = 32
S = 2048
T = B * S
GRID_W = 64
ROWS = S // GRID_W
N_HEADS = 16
HEAD_DIM = 64
WIN_ROWS = 8
WIN_COLS = 16
LRU_BLOCK = 64
LRU_C = 8.0
CONV_W = 4
CONV_LEFT = 2
D_FF = 2816
FFN_RES = 0.5
EPS = 1e-6
NEG = -1e30
LOG2E = float(np.log2(np.e))
TINY = 1e-37

SUBLANES = 8
BF16_ROWS = 16
LANES = 128
MXU_N = 256
VMEM_LIMIT = 56 * 1024 * 1024

TM_FFN = 512
TS_PROJ = 16
PITCH_PROJ = 24
TS_OUT = 16
PITCH_OUT = 40
FF_CHUNK = MXU_N
LRU_GW = MXU_N
KV_CHUNK = 256
HG = 4
HW = HG * HEAD_DIM
QR = 4
QC = 16
KR = 12
KC = 32
NQ = QR * QC
NKEY = KR * KC
N_CB = GRID_W // QC
N_RG = ROWS // QR

f32 = jnp.float32
bf16 = jnp.bfloat16


def _sigmoid(x):
  return 0.5 * jnp.tanh(0.5 * x) + 0.5


def _softplus(x):
  return jnp.maximum(x, 0.0) + jnp.log1p(jnp.exp(-jnp.abs(x)))


def _gelu_tanh(x, scale=1.0):
  c = float(np.sqrt(2.0 / np.pi))
  return (0.5 * scale) * x * (1.0 + jnp.tanh(c * (x + 0.044715 * (x * x * x))))


def _rmsnorm(x, g):
  ms = jnp.mean(x * x, axis=-1, keepdims=True)
  return (x * lax.rsqrt(ms + EPS)) * g


def _dot(a, b):
  return jnp.dot(a, b, preferred_element_type=f32)


def _const_spec(shape):
  nd = len(shape)
  return pl.BlockSpec(shape, lambda *_: (0,) * nd, pipeline_mode=pl.Buffered(1))


def _swiglu(x, g_ref, wgu_ref, wdn_ref, act_ref):
  h = _rmsnorm(x, g_ref[...]).astype(bf16)
  for c in range(D_FF // FF_CHUNK):
    lo = c * FF_CHUNK
    gate = _dot(h, wgu_ref[:, lo:lo + FF_CHUNK])
    up = _dot(h, wgu_ref[:, D_FF + lo:D_FF + lo + FF_CHUNK])
    act_ref[:, lo:lo + FF_CHUNK] = (gate * _sigmoid(gate) * up).astype(bf16)
  return _dot(act_ref[...], wdn_ref[...])


def _swap_rows(x, perm_ref, m, n, pitch):
  for i in range(m):
    for c in range(D // LANES):
      perm_ref[c, i * pitch:i * pitch + n, :] = x[i * n:(i + 1) * n, c * LANES:(c + 1) * LANES]
  return jnp.concatenate(
      [jnp.concatenate([perm_ref[c, pl.ds(j, m, stride=pitch), :] for c in range(D // LANES)], axis=-1)
       for j in range(n)], axis=0)


def _ffn1_kernel(x_ref, g_ref, wgu_ref, wdn_ref, o_ref, act_ref):
  x = x_ref[...]
  o_ref[...] = x + FFN_RES * _swiglu(x, g_ref, wgu_ref, wdn_ref, act_ref)


def _ffn1(x2d, g, wgu, wdn):
  row = lambda i: (i, 0)
  return pl.pallas_call(
      _ffn1_kernel,
      grid=(T // TM_FFN,),
      in_specs=[
          pl.BlockSpec((TM_FFN, D), row),
          _const_spec((1, D)),
          _const_spec((D, 2 * D_FF)),
          _const_spec((D_FF, D)),
      ],
      out_specs=pl.BlockSpec((TM_FFN, D), row),
      out_shape=jax.ShapeDtypeStruct((T, D), f32),
      scratch_shapes=[pltpu.VMEM((TM_FFN, D_FF), bf16)],
      compiler_params=pltpu.CompilerParams(
          dimension_semantics=("parallel",), vmem_limit_bytes=VMEM_LIMIT),
      name="ffn1",
  )(x2d, g, wgu, wdn)


def _lru_coeffs(xc, wg_ref, bg_ref, lam_ref):
  kexp = (-0.5 * LRU_C * LOG2E) * _softplus(-lam_ref[...])
  b_r = 0.5 * bg_ref[0:1, :]
  b_i = 0.5 * bg_ref[1:2, :]
  a_parts, b_parts = [], []
  for c in range(xc.shape[1] // LRU_GW):
    cs = slice(c * LRU_GW, (c + 1) * LRU_GW)
    xh = xc[:, cs]
    g = _dot(xh.astype(bf16), wg_ref[c])
    tr = jnp.tanh(g[:, :LRU_GW] + b_r[:, cs])
    ti = jnp.tanh(g[:, LRU_GW:] + b_i[:, cs])
    a = jnp.exp2(tr * kexp[:, cs] + kexp[:, cs])
    v = 1.0 - a * a
    sq = v * lax.rsqrt(jnp.maximum(v, TINY))
    a_parts.append(a)
    b_parts.append(sq * ((ti + 1.0) * xh))
  return a_parts, b_parts


def _gate_weights(w):
  per = LRU_GW // LRU_BLOCK
  ncb = D // LRU_GW
  w = w.reshape(2, ncb, per, LRU_BLOCK, LRU_BLOCK)
  eye = jnp.eye(per, dtype=w.dtype)
  dense = w[:, :, :, :, None, :] * eye[None, None, :, None, :, None]
  dense = dense.reshape(2, ncb, LRU_GW, LRU_GW)
  return (0.5 * jnp.concatenate([dense[0], dense[1]], axis=-1)).astype(bf16)


def _inproj_a_kernel(x_ref, xn_ref, g_ref, w_ref, cw_ref, cb_ref, wg_ref, bg_ref, lam_ref,
                     xc_ref, hf_ref, g1_ref, g2_ref, perm_ref, halo_ref, xprev_ref, car_ref):
  j = pl.program_id(0)
  last = pl.num_programs(0) - 1
  rows = B * TS_PROJ

  @pl.when(j == 0)
  def _():
    xprev_ref[...] = jnp.zeros_like(xprev_ref)
    car_ref[...] = jnp.zeros_like(car_ref)

  h = _rmsnorm(x_ref[...].reshape(rows, D), g_ref[...])
  h_bm = h.astype(bf16)
  h_tm = _swap_rows(h, perm_ref, B, TS_PROJ, PITCH_PROJ)
  hn = _rmsnorm(xn_ref[...].reshape(B * SUBLANES, D), g_ref[...])
  for c in range(D // LANES):
    halo_ref[c] = hn[:, c * LANES:(c + 1) * LANES]
  hn0 = jnp.concatenate(
      [halo_ref[c, pl.ds(0, B, stride=SUBLANES), :] for c in range(D // LANES)], axis=-1)

  def col(lhs, i):
    return _dot(lhs, w_ref[:, i * D:(i + 1) * D])

  xr = col(jnp.concatenate([h_tm, hn0], axis=0).astype(bf16), 0)
  nxt = jnp.where(j == last, 0.0, xr[rows:]).reshape(1, B, D)
  xr = xr[:rows].reshape(TS_PROJ, B, D)
  seq = jnp.concatenate([xprev_ref[...], xr, nxt], axis=0)
  xprev_ref[...] = xr[TS_PROJ - CONV_LEFT:]
  xc = cb_ref[...]
  for k in range(CONV_W):
    xc = xc + seq[k:k + TS_PROJ] * cw_ref[k:k + 1, :]
  xc_ref[...] = xc

  h_tm = h_tm.astype(bf16)
  g1_ref[...] = (_gelu_tanh(col(h_tm, 1), scale=0.5) * _sigmoid(col(h_tm, 2))).reshape(TS_PROJ, B, D)
  g2_ref[...] = _sigmoid(col(h_bm, 3)).reshape(B, TS_PROJ, D)

  a_parts, b_parts = _lru_coeffs(xc.reshape(rows, D), wg_ref, bg_ref, lam_ref)
  for c, (a, bx) in enumerate(zip(a_parts, b_parts)):
    cs = slice(c * LRU_GW, (c + 1) * LRU_GW)
    hcur = car_ref[:, cs]
    for s in range(TS_PROJ):
      hcur = a[s * B:(s + 1) * B] * hcur + bx[s * B:(s + 1) * B]
      hf_ref[s, :, cs] = hcur.astype(bf16)
    car_ref[:, cs] = hcur


def _inproj_a(x1, g, w_a, conv_w, conv_b, wg, bg, lam):
  nh = S // SUBLANES
  bm_spec = pl.BlockSpec((B, TS_PROJ, D), lambda j: (0, j, 0))
  tm_spec = pl.BlockSpec((TS_PROJ, B, D), lambda j: (j, 0, 0))
  halo_spec = pl.BlockSpec(
      (B, SUBLANES, D), lambda j: (0, jnp.minimum((j + 1) * (TS_PROJ // SUBLANES), nh - 1), 0))
  tm_out = jax.ShapeDtypeStruct((S, B, D), f32)
  return pl.pallas_call(
      _inproj_a_kernel,
      grid=(S // TS_PROJ,),
      in_specs=[
          bm_spec,
          halo_spec,
          _const_spec((1, D)),
          _const_spec((D, 4 * D)),
          _const_spec((CONV_W, D)),
          _const_spec((1, D)),
          _const_spec((D // LRU_GW, LRU_GW, 2 * LRU_GW)),
          _const_spec((2, D)),
          _const_spec((1, D)),
      ],
      out_specs=[tm_spec, tm_spec, tm_spec, bm_spec],
      out_shape=[tm_out, jax.ShapeDtypeStruct((S, B, D), bf16), tm_out,
                 jax.ShapeDtypeStruct((B, S, D), f32)],
      scratch_shapes=[pltpu.VMEM((D // LANES, B * PITCH_PROJ, LANES), f32),
                      pltpu.VMEM((D // LANES, B * SUBLANES, LANES), f32),
                      pltpu.VMEM((CONV_LEFT, B, D), f32),
                      pltpu.VMEM((B, D), f32)],
      compiler_params=pltpu.CompilerParams(
          dimension_semantics=("arbitrary",), vmem_limit_bytes=VMEM_LIMIT),
      name="inproj_a",
  )(x1, x1, g, w_a, conv_w, conv_b, wg, bg, lam)


def _head_rms(x, ones_ref):
  parts = []
  for j in range(D // MXU_N):
    sq = x[:, j * MXU_N:(j + 1) * MXU_N]
    parts.append(_dot((sq * sq).astype(bf16), ones_ref[...]))
  return jnp.concatenate(parts, axis=-1) * (1.0 / HEAD_DIM)


def _inproj_b_kernel(x_ref, xc_ref, hf_ref, g_ref, w_ref, qn_ref, kn_ref, ones_ref,
                     wg_ref, bg_ref, lam_ref, q_ref, k_ref, v_ref, hs_ref, car_ref):
  rows = B * TS_PROJ

  @pl.when(pl.program_id(0) == 0)
  def _():
    car_ref[...] = jnp.zeros_like(car_ref)

  a_parts, b_parts = _lru_coeffs(xc_ref[...].reshape(rows, D), wg_ref, bg_ref, lam_ref)
  for c, (a, bx) in enumerate(zip(a_parts, b_parts)):
    cs = slice(c * LRU_GW, (c + 1) * LRU_GW)
    hcur = car_ref[:, cs]
    for s in reversed(range(TS_PROJ)):
      hcur = a[s * B:(s + 1) * B] * hcur + bx[s * B:(s + 1) * B]
      hs_ref[s, :, cs] = (hf_ref[s, :, cs].astype(f32) + hcur).astype(bf16)
    car_ref[:, cs] = hcur

  h = _rmsnorm(x_ref[...].reshape(rows, D), g_ref[...]).astype(bf16)

  def col(i):
    return _dot(h, w_ref[:, i * D:(i + 1) * D])

  bm = lambda a: a.astype(bf16).reshape(B, TS_PROJ, D)
  q = col(0)
  q = (q * lax.rsqrt(_head_rms(q, ones_ref) + EPS)) * qn_ref[...] * (HEAD_DIM ** -0.5 * LOG2E)
  q_ref[...] = bm(q)
  k = col(1)
  k_ref[...] = bm((k * lax.rsqrt(_head_rms(k, ones_ref) + EPS)) * kn_ref[...])
  v_ref[...] = bm(col(2))


def _inproj_b(x1, xc, hf, g, w_b, qn, kn, ones_bd, wg, bg, lam):
  nt = S // TS_PROJ
  bm_spec = pl.BlockSpec((B, TS_PROJ, D), lambda i: (0, nt - 1 - i, 0))
  tm_spec = pl.BlockSpec((TS_PROJ, B, D), lambda i: (nt - 1 - i, 0, 0))
  bm_out = jax.ShapeDtypeStruct((B, S, D), bf16)
  return pl.pallas_call(
      _inproj_b_kernel,
      grid=(nt,),
      in_specs=[
          bm_spec, tm_spec, tm_spec,
          _const_spec((1, D)),
          _const_spec((D, 3 * D)),
          _const_spec((1, D)),
          _const_spec((1, D)),
          _const_spec((MXU_N, MXU_N)),
          _const_spec((D // LRU_GW, LRU_GW, 2 * LRU_GW)),
          _const_spec((2, D)),
          _const_spec((1, D)),
      ],
      out_specs=[bm_spec, bm_spec, bm_spec, tm_spec],
      out_shape=[bm_out, bm_out, bm_out, jax.ShapeDtypeStruct((S, B, D), bf16)],
      scratch_shapes=[pltpu.VMEM((B, D), f32)],
      compiler_params=pltpu.CompilerParams(
          dimension_semantics=("arbitrary",), vmem_limit_bytes=VMEM_LIMIT),
      name="inproj_b",
  )(x1, xc, hf, g, w_b, qn, kn, ones_bd, wg, bg, lam)


def _key_row0(rg):
  return int(np.clip(QR * rg - WIN_ROWS // 2, 0, ROWS - KR))


def _key_col0(n):
  return int(np.clip(QC * n - WIN_COLS // 2, 0, GRID_W - KC))


def _table_kind(rg):
  return 0 if rg == 0 else (2 if rg == N_RG - 1 else 1)


def _natt_kernel(q_ref, k_ref, v_ref, g2_ref, tbl_ref, o_ref, k8_ref, v8_ref):
  half = BF16_ROWS // 2
  for src, dst in ((k_ref, k8_ref), (v_ref, v8_ref)):
    for c0 in range(0, S - BF16_ROWS, KV_CHUNK):
      n = min(KV_CHUNK, S - BF16_ROWS - c0)
      wide = src[c0:c0 + n + BF16_ROWS, :].astype(f32)
      dst[c0:c0 + n, :] = wide[half:half + n].astype(bf16)

  def key_block(ref0, ref8, rows):
    parts = []
    for r in rows:
      parts.append(ref0[r:r + KC, :] if r % BF16_ROWS == 0 else ref8[r - half:r - half + KC, :])
    return jnp.concatenate(parts, axis=0)

  lane_head = lax.broadcasted_iota(jnp.int32, (NQ, HW), 1) // HEAD_DIM
  for rg in range(N_RG):
    k0 = _key_row0(rg)
    for n in range(N_CB):
      c0 = _key_col0(n)
      q_rows = [(QR * rg + rl) * GRID_W + QC * n for rl in range(QR)]
      k_rows = [(k0 + i) * GRID_W + c0 for i in range(KR)]
      qs = jnp.concatenate([q_ref[r:r + QC, :] for r in q_rows], axis=0)
      xq = jnp.concatenate(
          [jnp.where(lane_head == h, qs, jnp.zeros_like(qs)) for h in range(HG)], axis=0)
      kb = key_block(k_ref, k8_ref, k_rows)
      vb = key_block(v_ref, v8_ref, k_rows)
      s = lax.dot_general(xq, kb, (((1,), (1,)), ((), ())), preferred_element_type=f32)
      s = s + tbl_ref[0, _table_kind(rg), n]
      p = jnp.exp2(s - jnp.max(s, axis=1, keepdims=True))
      inv = 1.0 / jnp.sum(p, axis=1, keepdims=True)
      o = _dot(p.astype(bf16), vb) * inv
      acc = jnp.where(lane_head == 0, o[0:NQ], 0.0)
      for h in range(1, HG):
        acc = jnp.where(lane_head == h, o[h * NQ:(h + 1) * NQ], acc)
      for rl, r in enumerate(q_rows):
        o_ref[r:r + QC, :] = g2_ref[r:r + QC, :] * acc[rl * QC:(rl + 1) * QC]


def _natt(q, k, v, g2, tbl):
  ng = N_HEADS // HG
  spec = pl.BlockSpec((None, S, HW), lambda g, b: (b, 0, g))
  return pl.pallas_call(
      _natt_kernel,
      grid=(ng, B),
      in_specs=[
          spec, spec, spec, spec,
          pl.BlockSpec((1, 3, N_CB, HG * NQ, NKEY), lambda g, b: (g, 0, 0, 0, 0)),
      ],
      out_specs=spec,
      out_shape=jax.ShapeDtypeStruct((B, S, D), f32),
      scratch_shapes=[pltpu.VMEM((S, HW), bf16)] * 2,
      compiler_params=pltpu.CompilerParams(
          dimension_semantics=("parallel", "parallel"), vmem_limit_bytes=VMEM_LIMIT),
      name="natt",
  )(q, k, v, g2, tbl)


def _bias_tables(rpb):
  n_dr, n_dc = 2 * WIN_ROWS - 1, 2 * WIN_COLS - 1
  i = np.arange(KR)[:, None]
  rl = np.arange(QR)[None, :]
  row_hot, row_ok = [], []
  for rg in (0, 1, N_RG - 1):
    r = QR * rg + rl
    rs = np.clip(r - WIN_ROWS // 2, 0, ROWS - WIN_ROWS)
    krow = _key_row0(rg) + i
    row_ok.append((krow >= rs) & (krow < rs + WIN_ROWS))
    row_hot.append(np.eye(n_dr)[np.clip(krow - r + WIN_ROWS - 1, 0, n_dr - 1)])
  c = np.arange(KC)[:, None]
  ql = np.arange(QC)[None, :]
  col_hot, col_ok = [], []
  for n in range(N_CB):
    qc = QC * n + ql
    kc = _key_col0(n) + c
    cs = np.clip(qc - WIN_COLS // 2, 0, GRID_W - WIN_COLS)
    col_ok.append((kc >= cs) & (kc < cs + WIN_COLS))
    col_hot.append(np.eye(n_dc)[np.clip(kc - qc, -(WIN_COLS - 1), WIN_COLS - 1) + WIN_COLS - 1])
  row_ok, col_ok = np.stack(row_ok), np.stack(col_ok)
  ok = (row_ok.transpose(0, 2, 1)[:, None, None, :, None, :, None]
        & col_ok.transpose(0, 2, 1)[None, :, None, None, :, None, :])
  ng = N_HEADS // HG
  bias = jnp.einsum(
      "kira,ghab,ncqb->gknhrqic", jnp.asarray(np.stack(row_hot), f32),
      rpb.astype(f32).reshape(ng, HG, n_dr, n_dc), jnp.asarray(np.stack(col_hot), f32),
      precision=lax.Precision.HIGHEST)
  bias = jnp.where(ok[None], LOG2E * bias, NEG)
  return bias.reshape(ng, 3, N_CB, HG * NQ, NKEY)


def _ffn2_kernel(x1_ref, hs_ref, g1_ref, ya_ref, wo_ref, g_ref, wgu_ref, wdn_ref,
                 o_ref, act_ref, perm_ref):
  rows = B * TS_OUT
  y_lru = (g1_ref[...] * hs_ref[...].astype(f32)).reshape(rows, D)
  y_lru = _swap_rows(y_lru, perm_ref, TS_OUT, B, PITCH_OUT)
  y = y_lru + ya_ref[...].reshape(rows, D)
  x2 = x1_ref[...].reshape(rows, D) + _dot(y.astype(bf16), wo_ref[...])
  out = x2 + FFN_RES * _swiglu(x2, g_ref, wgu_ref, wdn_ref, act_ref)
  o_ref[...] = out.reshape(B, TS_OUT, D)


def _ffn2(x1, hs, g1, ya, wo, g, wgu, wdn):
  bm_spec = pl.BlockSpec((B, TS_OUT, D), lambda j: (0, j, 0))
  tm_spec = pl.BlockSpec((TS_OUT, B, D), lambda j: (j, 0, 0))
  return pl.pallas_call(
      _ffn2_kernel,
      grid=(S // TS_OUT,),
      in_specs=[bm_spec, tm_spec, tm_spec, bm_spec,
                _const_spec((D, D)),
                _const_spec((1, D)),
                _const_spec((D, 2 * D_FF)),
                _const_spec((D_FF, D))],
      out_specs=bm_spec,
      out_shape=jax.ShapeDtypeStruct((B, S, D), f32),
      scratch_shapes=[pltpu.VMEM((B * TS_OUT, D_FF), bf16),
                      pltpu.VMEM((D // LANES, TS_OUT * PITCH_OUT, LANES), f32)],
      compiler_params=pltpu.CompilerParams(
          dimension_semantics=("parallel",), vmem_limit_bytes=VMEM_LIMIT),
      name="ffn2",
  )(x1, hs, g1, ya, wo, g, wgu, wdn)


def kernel(x, norm_ffn1, w_ffn1_gu, w_ffn1_down, norm_mix, w_in, conv_w, conv_b,
           lru_w_gates, lru_b_gates, lru_lambda, q_norm, k_norm, rel_pos_bias, w_out,
           norm_ffn2, w_ffn2_gu, w_ffn2_down):
  assert x.shape == (B, S, D) and norm_ffn1.shape[0] == 1
  l = 0
  row = lambda a: a.reshape(1, D).astype(f32)
  ones_bd = jnp.asarray(
      np.kron(np.eye(MXU_N // HEAD_DIM), np.ones((HEAD_DIM, HEAD_DIM))), dtype=bf16)
  wi = w_in[l].astype(bf16)
  w_a = jnp.concatenate([wi[:, :2 * D], wi[:, 5 * D:]], axis=1)
  w_b = wi[:, 2 * D:5 * D]

  x1 = _ffn1(x.reshape(T, D), row(norm_ffn1[l]), w_ffn1_gu[l].astype(bf16),
             w_ffn1_down[l].astype(bf16)).reshape(B, S, D)

  gates = lambda d: (_gate_weights(lru_w_gates[l, d]), lru_b_gates[l, d].astype(f32),
                     row(lru_lambda[l, d]))
  xc, hf, g1, g2 = _inproj_a(x1, row(norm_mix[l]), w_a,
                             conv_w[l].astype(f32), row(conv_b[l]), *gates(0))
  q, k, v, hs = _inproj_b(x1, xc, hf, row(norm_mix[l]), w_b,
                          row(jnp.tile(q_norm[l], N_HEADS)), row(jnp.tile(k_norm[l], N_HEADS)),
                          ones_bd, *gates(1))

  ya = _natt(q, k, v, g2, _bias_tables(rel_pos_bias[l]))

  return _ffn2(x1, hs, g1, ya, w_out[l].astype(bf16),
               row(norm_ffn2[l]), w_ffn2_gu[l].astype(bf16), w_ffn2_down[l].astype(bf16))
```

```python
import jax
import jax.numpy as jnp
import numpy as np
from jax import lax
from jax.experimental import pallas as pl
from jax.experimental.pallas import tpu as pltpu

D = 1024
B = 32
S = 2048
T = B * S
GRID_W = 64
ROWS = S // GRID_W
N_HEADS = 16
HEAD_DIM = 64
WIN_ROWS = 8
WIN_COLS = 16
LRU_BLOCK = 64
LRU_C = 8.0
CONV_W = 4
CONV_LEFT = 2
D_FF = 2816
FFN_RES = 0.5
EPS = 1e-6
NEG = -1e30
LOG2E = float(np.log2(np.e))
TINY = 1e-37

SUBLANES = 8
BF16_ROWS = 16
LANES = 128
MXU_N = 256
VMEM_LIMIT = 56 * 1024 * 1024

TM_FFN = 512
TS_PROJ = 16
PITCH_PROJ = 24
TS_OUT = 16
PITCH_OUT = 40
FF_CHUNK = MXU_N
LRU_GW = MXU_N
KV_CHUNK = 256
HG = 4
HW = HG * HEAD_DIM
QR = 4
QC = 16
KR = 12
KC = 32
NQ = QR * QC
NKEY = KR * KC
N_CB = GRID_W // QC
N_RG = ROWS // QR

f32 = jnp.float32
bf16 = jnp.bfloat16


def _sigmoid(x):
  return 0.5 * jnp.tanh(0.5 * x) + 0.5


def _softplus(x):
  return jnp.maximum(x, 0.0) + jnp.log1p(jnp.exp(-jnp.abs(x)))


def _gelu_tanh(x, scale=1.0):
  c = float(np.sqrt(2.0 / np.pi))
  return (0.5 * scale) * x * (1.0 + jnp.tanh(c * (x + 0.044715 * (x * x * x))))


def _rmsnorm(x, g):
  ms = jnp.mean(x * x, axis=-1, keepdims=True)
  return (x * lax.rsqrt(ms + EPS)) * g


def _dot(a, b):
  return jnp.dot(a, b, preferred_element_type=f32)


def _const_spec(shape):
  nd = len(shape)
  return pl.BlockSpec(shape, lambda *_: (0,) * nd, pipeline_mode=pl.Buffered(1))


def _swiglu_hidden(x, g_ref, wgu_ref, act_ref):
  h = _rmsnorm(x, g_ref[...]).astype(bf16)
  for c in range(D_FF // FF_CHUNK):
    lo = c * FF_CHUNK
    gate = _dot(h, wgu_ref[:, lo:lo + FF_CHUNK])
    up = _dot(h, wgu_ref[:, D_FF + lo:D_FF + lo + FF_CHUNK])
    act_ref[:, lo:lo + FF_CHUNK] = (gate * _sigmoid(gate) * up).astype(bf16)


def _swiglu(x, g_ref, wgu_ref, wdn_ref, act_ref):
  _swiglu_hidden(x, g_ref, wgu_ref, act_ref)
  return _dot(act_ref[...], wdn_ref[...])


def _swap_rows(x, perm_ref, m, n, pitch):
  for i in range(m):
    for c in range(D // LANES):
      perm_ref[c, i * pitch:i * pitch + n, :] = x[i * n:(i + 1) * n, c * LANES:(c + 1) * LANES]
  return jnp.concatenate(
      [jnp.concatenate([perm_ref[c, pl.ds(j, m, stride=pitch), :] for c in range(D // LANES)], axis=-1)
       for j in range(n)], axis=0)


def _ffn1_kernel(x_ref, g_ref, wgu_ref, wdn_ref, gm_ref, o_ref, hm_ref, act_ref):
  _swiglu_hidden(x_ref[...], g_ref, wgu_ref, act_ref)
  half = TM_FFN // 2
  for lo in (0, half):
    x1 = x_ref[lo:lo + half, :] + FFN_RES * _dot(act_ref[lo:lo + half, :], wdn_ref[...])
    o_ref[lo:lo + half, :] = x1
    hm_ref[lo:lo + half, :] = _rmsnorm(x1, gm_ref[...]).astype(bf16)


def _ffn1(x2d, g, wgu, wdn, g_mix):
  spec = pl.BlockSpec((TM_FFN, D), lambda i: (i, 0))
  return pl.pallas_call(
      _ffn1_kernel,
      grid=(T // TM_FFN,),
      in_specs=[
          spec,
          _const_spec((1, D)),
          _const_spec((D, 2 * D_FF)),
          _const_spec((D_FF, D)),
          _const_spec((1, D)),
      ],
      out_specs=[spec, spec],
      out_shape=[jax.ShapeDtypeStruct((T, D), f32), jax.ShapeDtypeStruct((T, D), bf16)],
      scratch_shapes=[pltpu.VMEM((TM_FFN, D_FF), bf16)],
      compiler_params=pltpu.CompilerParams(
          dimension_semantics=("parallel",), vmem_limit_bytes=VMEM_LIMIT),
      name="ffn1",
  )(x2d, g, wgu, wdn, g_mix)


def _lru_consts(bg_ref, lam_ref):
  kexp = (-0.5 * LRU_C * LOG2E) * _softplus(-lam_ref[...])
  return kexp, 0.5 * bg_ref[0:1, :], 0.5 * bg_ref[1:2, :]


def _lru_block(xh, wg, kexp, b_r, b_i):
  g = _dot(xh.astype(bf16), wg)
  tr = jnp.tanh(g[:, :LRU_GW] + b_r)
  ti = jnp.tanh(g[:, LRU_GW:] + b_i)
  a = jnp.exp2(tr * kexp + kexp)
  v = 1.0 - a * a
  sq = v * lax.rsqrt(jnp.maximum(v, TINY))
  return a, sq * ((ti + 1.0) * xh)


def _gate_weights(w):
  per = LRU_GW // LRU_BLOCK
  ncb = D // LRU_GW
  w = w.reshape(2, ncb, per, LRU_BLOCK, LRU_BLOCK)
  eye = jnp.eye(per, dtype=w.dtype)
  dense = w[:, :, :, :, None, :] * eye[None, None, :, None, :, None]
  dense = dense.reshape(2, ncb, LRU_GW, LRU_GW)
  return (0.5 * jnp.concatenate([dense[0], dense[1]], axis=-1)).astype(bf16)


def _inproj_a_kernel(h_ref, hn_ref, w_ref, cw_ref, cb_ref, wg_ref, bg_ref, lam_ref,
                     xc_ref, hf_ref, g1_ref, g2_ref, perm_ref, halo_ref, xprev_ref, car_ref):
  j = pl.program_id(0)
  last = pl.num_programs(0) - 1
  rows = B * TS_PROJ

  @pl.when(j == 0)
  def _():
    xprev_ref[...] = jnp.zeros_like(xprev_ref)
    car_ref[...] = jnp.zeros_like(car_ref)

  h_bm = h_ref[...].reshape(rows, D)
  g2_ref[...] = _sigmoid(_dot(h_bm, w_ref[:, 3 * D:])).reshape(B, TS_PROJ, D)
  h_tm = _swap_rows(h_bm.astype(f32), perm_ref, B, TS_PROJ, PITCH_PROJ)
  hn = hn_ref[...].reshape(rows, D).astype(f32)
  for c in range(D // LANES):
    halo_ref[c] = hn[:, c * LANES:(c + 1) * LANES]
  hn0 = jnp.concatenate(
      [halo_ref[c, pl.ds(0, B, stride=TS_PROJ), :] for c in range(D // LANES)], axis=-1)

  lhs_x = jnp.concatenate([h_tm, hn0], axis=0).astype(bf16)
  h_tm = lhs_x[:rows]
  kexp, b_r, b_i = _lru_consts(bg_ref, lam_ref)

  for c in range(D // LRU_GW):
    cs = slice(c * LRU_GW, (c + 1) * LRU_GW)
    wcol = lambda i: w_ref[:, i * D + c * LRU_GW:i * D + (c + 1) * LRU_GW]
    xr = _dot(lhs_x, wcol(0))
    gr, ga = _dot(h_tm, wcol(1)), _dot(h_tm, wcol(2))
    nxt = jnp.where(j == last, 0.0, xr[rows:]).reshape(1, B, LRU_GW)
    xr = xr[:rows].reshape(TS_PROJ, B, LRU_GW)
    seq = jnp.concatenate([xprev_ref[:, :, cs], xr, nxt], axis=0)
    xprev_ref[:, :, cs] = xr[TS_PROJ - CONV_LEFT:]
    xc = cb_ref[:, cs]
    for k in range(CONV_W):
      xc = xc + seq[k:k + TS_PROJ] * cw_ref[k:k + 1, cs]
    xc_ref[:, :, cs] = xc

    a, bx = _lru_block(xc.reshape(rows, LRU_GW), wg_ref[c], kexp[:, cs], b_r[:, cs], b_i[:, cs])
    hcur = car_ref[:, cs]
    for s in range(TS_PROJ):
      hcur = a[s * B:(s + 1) * B] * hcur + bx[s * B:(s + 1) * B]
      hf_ref[s, :, cs] = hcur.astype(bf16)
    car_ref[:, cs] = hcur

    g1_ref[:, :, cs] = (_gelu_tanh(gr, scale=0.5) * _sigmoid(ga)).reshape(TS_PROJ, B, LRU_GW)


def _inproj_a(hm, w_a, conv_w, conv_b, wg, bg, lam):
  nt = S // TS_PROJ
  bm_spec = pl.BlockSpec((B, TS_PROJ, D), lambda j: (0, j, 0))
  tm_spec = pl.BlockSpec((TS_PROJ, B, D), lambda j: (j, 0, 0))
  next_spec = pl.BlockSpec((B, TS_PROJ, D), lambda j: (0, jnp.minimum(j + 1, nt - 1), 0))
  tm_out = jax.ShapeDtypeStruct((S, B, D), f32)
  return pl.pallas_call(
      _inproj_a_kernel,
      grid=(nt,),
      in_specs=[
          bm_spec,
          next_spec,
          _const_spec((D, 4 * D)),
          _const_spec((CONV_W, D)),
          _const_spec((1, D)),
          _const_spec((D // LRU_GW, LRU_GW, 2 * LRU_GW)),
          _const_spec((2, D)),
          _const_spec((1, D)),
      ],
      out_specs=[tm_spec, tm_spec, tm_spec, bm_spec],
      out_shape=[tm_out, jax.ShapeDtypeStruct((S, B, D), bf16), tm_out,
                 jax.ShapeDtypeStruct((B, S, D), f32)],
      scratch_shapes=[pltpu.VMEM((D // LANES, B * PITCH_PROJ, LANES), f32),
                      pltpu.VMEM((D // LANES, B * TS_PROJ, LANES), f32),
                      pltpu.VMEM((CONV_LEFT, B, D), f32),
                      pltpu.VMEM((B, D), f32)],
      compiler_params=pltpu.CompilerParams(
          dimension_semantics=("arbitrary",), vmem_limit_bytes=VMEM_LIMIT),
      name="inproj_a",
  )(hm, hm, w_a, conv_w, conv_b, wg, bg, lam)


def _inproj_b_kernel(h_ref, xc_ref, hf_ref, w_ref, qn_ref, kn_ref, ones_ref,
                     wg_ref, bg_ref, lam_ref, q_ref, k_ref, v_ref, hs_ref, car_ref):
  rows = B * TS_PROJ

  @pl.when(pl.program_id(0) == 0)
  def _():
    car_ref[...] = jnp.zeros_like(car_ref)

  h = h_ref[...].reshape(rows, D)
  kexp, b_r, b_i = _lru_consts(bg_ref, lam_ref)
  bm = lambda a: a.astype(bf16).reshape(B, TS_PROJ, MXU_N)
  head_ms = lambda t: _dot((t * t).astype(bf16), ones_ref[...]) * (1.0 / HEAD_DIM)

  for c in range(D // MXU_N):
    cs = slice(c * MXU_N, (c + 1) * MXU_N)
    wcol = lambda i: w_ref[:, i * D + c * MXU_N:i * D + (c + 1) * MXU_N]
    q, k, v = _dot(h, wcol(0)), _dot(h, wcol(1)), _dot(h, wcol(2))
    v_ref[:, :, cs] = bm(v)
    q = (q * lax.rsqrt(head_ms(q) + EPS)) * qn_ref[:, cs] * (HEAD_DIM ** -0.5 * LOG2E)
    q_ref[:, :, cs] = bm(q)
    k_ref[:, :, cs] = bm((k * lax.rsqrt(head_ms(k) + EPS)) * kn_ref[:, cs])

    xc = xc_ref[:, :, cs].reshape(rows, LRU_GW)
    a, bx = _lru_block(xc, wg_ref[c], kexp[:, cs], b_r[:, cs], b_i[:, cs])
    hcur = car_ref[:, cs]
    for s in reversed(range(TS_PROJ)):
      hcur = a[s * B:(s + 1) * B] * hcur + bx[s * B:(s + 1) * B]
      hs_ref[s, :, cs] = (hf_ref[s, :, cs].astype(f32) + hcur).astype(bf16)
    car_ref[:, cs] = hcur


def _inproj_b(hm, xc, hf, w_b, qn, kn, ones_bd, wg, bg, lam):
  nt = S // TS_PROJ
  bm_spec = pl.BlockSpec((B, TS_PROJ, D), lambda i: (0, nt - 1 - i, 0))
  tm_spec = pl.BlockSpec((TS_PROJ, B, D), lambda i: (nt - 1 - i, 0, 0))
  bm_out = jax.ShapeDtypeStruct((B, S, D), bf16)
  return pl.pallas_call(
      _inproj_b_kernel,
      grid=(nt,),
      in_specs=[
          bm_spec, tm_spec, tm_spec,
          _const_spec((D, 3 * D)),
          _const_spec((1, D)),
          _const_spec((1, D)),
          _const_spec((MXU_N, MXU_N)),
          _const_spec((D // LRU_GW, LRU_GW, 2 * LRU_GW)),
          _const_spec((2, D)),
          _const_spec((1, D)),
      ],
      out_specs=[bm_spec, bm_spec, bm_spec, tm_spec],
      out_shape=[bm_out, bm_out, bm_out, jax.ShapeDtypeStruct((S, B, D), bf16)],
      scratch_shapes=[pltpu.VMEM((B, D), f32)],
      compiler_params=pltpu.CompilerParams(
          dimension_semantics=("arbitrary",), vmem_limit_bytes=VMEM_LIMIT),
      name="inproj_b",
  )(hm, xc, hf, w_b, qn, kn, ones_bd, wg, bg, lam)


def _key_row0(rg):
  return int(np.clip(QR * rg - WIN_ROWS // 2, 0, ROWS - KR))


def _key_col0(n):
  return int(np.clip(QC * n - WIN_COLS // 2, 0, GRID_W - KC))


def _table_kind(rg):
  return 0 if rg == 0 else (2 if rg == N_RG - 1 else 1)


def _natt_kernel(q_ref, k_ref, v_ref, g2_ref, tbl_ref, o_ref, k8_ref, v8_ref):
  half = BF16_ROWS // 2
  for src, dst in ((k_ref, k8_ref), (v_ref, v8_ref)):
    for c0 in range(0, S - BF16_ROWS, KV_CHUNK):
      n = min(KV_CHUNK, S - BF16_ROWS - c0)
      wide = src[c0:c0 + n + BF16_ROWS, :].astype(f32)
      dst[c0:c0 + n, :] = wide[half:half + n].astype(bf16)

  def key_block(ref0, ref8, rows):
    parts = []
    for r in rows:
      parts.append(ref0[r:r + KC, :] if r % BF16_ROWS == 0 else ref8[r - half:r - half + KC, :])
    return jnp.concatenate(parts, axis=0)

  lane_head = lax.broadcasted_iota(jnp.int32, (NQ, HW), 1) // HEAD_DIM
  for rg in range(N_RG):
    k0 = _key_row0(rg)
    for n in range(N_CB):
      c0 = _key_col0(n)
      q_rows = [(QR * rg + rl) * GRID_W + QC * n for rl in range(QR)]
      k_rows = [(k0 + i) * GRID_W + c0 for i in range(KR)]
      qs = jnp.concatenate([q_ref[r:r + QC, :] for r in q_rows], axis=0)
      xq = jnp.concatenate(
          [jnp.where(lane_head == h, qs, jnp.zeros_like(qs)) for h in range(HG)], axis=0)
      kb = key_block(k_ref, k8_ref, k_rows)
      vb = key_block(v_ref, v8_ref, k_rows)
      s = lax.dot_general(xq, kb, (((1,), (1,)), ((), ())), preferred_element_type=f32)
      s = s + tbl_ref[0, _table_kind(rg), n]
      p = jnp.exp2(s - jnp.max(s, axis=1, keepdims=True))
      inv = 1.0 / jnp.sum(p, axis=1, keepdims=True)
      o = _dot(p.astype(bf16), vb) * inv
      acc = jnp.where(lane_head == 0, o[0:NQ], 0.0)
      for h in range(1, HG):
        acc = jnp.where(lane_head == h, o[h * NQ:(h + 1) * NQ], acc)
      for rl, r in enumerate(q_rows):
        o_ref[r:r + QC, :] = g2_ref[r:r + QC, :] * acc[rl * QC:(rl + 1) * QC]


def _natt(q, k, v, g2, tbl):
  ng = N_HEADS // HG
  spec = pl.BlockSpec((None, S, HW), lambda g, b: (b, 0, g))
  return pl.pallas_call(
      _natt_kernel,
      grid=(ng, B),
      in_specs=[
          spec, spec, spec, spec,
          pl.BlockSpec((1, 3, N_CB, HG * NQ, NKEY), lambda g, b: (g, 0, 0, 0, 0)),
      ],
      out_specs=spec,
      out_shape=jax.ShapeDtypeStruct((B, S, D), f32),
      scratch_shapes=[pltpu.VMEM((S, HW), bf16)] * 2,
      compiler_params=pltpu.CompilerParams(
          dimension_semantics=("parallel", "parallel"), vmem_limit_bytes=VMEM_LIMIT),
      name="natt",
  )(q, k, v, g2, tbl)


def _bias_tables(rpb):
  n_dr, n_dc = 2 * WIN_ROWS - 1, 2 * WIN_COLS - 1
  i = np.arange(KR)[:, None]
  rl = np.arange(QR)[None, :]
  row_hot, row_ok = [], []
  for rg in (0, 1, N_RG - 1):
    r = QR * rg + rl
    rs = np.clip(r - WIN_ROWS // 2, 0, ROWS - WIN_ROWS)
    krow = _key_row0(rg) + i
    row_ok.append((krow >= rs) & (krow < rs + WIN_ROWS))
    row_hot.append(np.eye(n_dr)[np.clip(krow - r + WIN_ROWS - 1, 0, n_dr - 1)])
  c = np.arange(KC)[:, None]
  ql = np.arange(QC)[None, :]
  col_hot, col_ok = [], []
  for n in range(N_CB):
    qc = QC * n + ql
    kc = _key_col0(n) + c
    cs = np.clip(qc - WIN_COLS // 2, 0, GRID_W - WIN_COLS)
    col_ok.append((kc >= cs) & (kc < cs + WIN_COLS))
    col_hot.append(np.eye(n_dc)[np.clip(kc - qc, -(WIN_COLS - 1), WIN_COLS - 1) + WIN_COLS - 1])
  row_ok, col_ok = np.stack(row_ok), np.stack(col_ok)
  ok = (row_ok.transpose(0, 2, 1)[:, None, None, :, None, :, None]
        & col_ok.transpose(0, 2, 1)[None, :, None, None, :, None, :])
  ng = N_HEADS // HG
  bias = jnp.einsum(
      "kira,ghab,ncqb->gknhrqic", jnp.asarray(np.stack(row_hot), f32),
      rpb.astype(f32).reshape(ng, HG, n_dr, n_dc), jnp.asarray(np.stack(col_hot), f32),
      precision=lax.Precision.HIGHEST)
  bias = jnp.where(ok[None], LOG2E * bias, NEG)
  return bias.reshape(ng, 3, N_CB, HG * NQ, NKEY)


def _ffn2_kernel(x1_ref, hs_ref, g1_ref, ya_ref, wo_ref, g_ref, wgu_ref, wdn_ref,
                 o_ref, act_ref, perm_ref):
  rows = B * TS_OUT
  y_lru = (g1_ref[...] * hs_ref[...].astype(f32)).reshape(rows, D)
  y_lru = _swap_rows(y_lru, perm_ref, TS_OUT, B, PITCH_OUT)
  y = y_lru + ya_ref[...].reshape(rows, D)
  x2 = x1_ref[...].reshape(rows, D) + _dot(y.astype(bf16), wo_ref[...])
  out = x2 + FFN_RES * _swiglu(x2, g_ref, wgu_ref, wdn_ref, act_ref)
  o_ref[...] = out.reshape(B, TS_OUT, D)


def _ffn2(x1, hs, g1, ya, wo, g, wgu, wdn):
  bm_spec = pl.BlockSpec((B, TS_OUT, D), lambda j: (0, j, 0))
  tm_spec = pl.BlockSpec((TS_OUT, B, D), lambda j: (j, 0, 0))
  return pl.pallas_call(
      _ffn2_kernel,
      grid=(S // TS_OUT,),
      in_specs=[bm_spec, tm_spec, tm_spec, bm_spec,
                _const_spec((D, D)),
                _const_spec((1, D)),
                _const_spec((D, 2 * D_FF)),
                _const_spec((D_FF, D))],
      out_specs=bm_spec,
      out_shape=jax.ShapeDtypeStruct((B, S, D), f32),
      scratch_shapes=[pltpu.VMEM((B * TS_OUT, D_FF), bf16),
                      pltpu.VMEM((D // LANES, TS_OUT * PITCH_OUT, LANES), f32)],
      compiler_params=pltpu.CompilerParams(
          dimension_semantics=("parallel",), vmem_limit_bytes=VMEM_LIMIT),
      name="ffn2",
  )(x1, hs, g1, ya, wo, g, wgu, wdn)


def kernel(x, norm_ffn1, w_ffn1_gu, w_ffn1_down, norm_mix, w_in, conv_w, conv_b,
           lru_w_gates, lru_b_gates, lru_lambda, q_norm, k_norm, rel_pos_bias, w_out,
           norm_ffn2, w_ffn2_gu, w_ffn2_down):
  assert x.shape == (B, S, D) and norm_ffn1.shape[0] == 1
  l = 0
  row = lambda a: a.reshape(1, D).astype(f32)
  ones_bd = jnp.asarray(
      np.kron(np.eye(MXU_N // HEAD_DIM), np.ones((HEAD_DIM, HEAD_DIM))), dtype=bf16)
  wi = w_in[l].astype(bf16)
  w_a = jnp.concatenate([wi[:, :2 * D], wi[:, 5 * D:]], axis=1)
  w_b = wi[:, 2 * D:5 * D]

  x1, hm = _ffn1(x.reshape(T, D), row(norm_ffn1[l]), w_ffn1_gu[l].astype(bf16),
                 w_ffn1_down[l].astype(bf16), row(norm_mix[l]))
  x1, hm = x1.reshape(B, S, D), hm.reshape(B, S, D)

  gates = lambda d: (_gate_weights(lru_w_gates[l, d]), lru_b_gates[l, d].astype(f32),
                     row(lru_lambda[l, d]))
  xc, hf, g1, g2 = _inproj_a(hm, w_a, conv_w[l].astype(f32), row(conv_b[l]), *gates(0))
  q, k, v, hs = _inproj_b(hm, xc, hf, w_b,
                          row(jnp.tile(q_norm[l], N_HEADS)), row(jnp.tile(k_norm[l], N_HEADS)),
                          ones_bd, *gates(1))

  ya = _natt(q, k, v, g2, _bias_tables(rel_pos_bias[l]))

  return _ffn2(x1, hs, g1, ya, w_out[l].astype(bf16),
               row(norm_ffn2[l]), w_ffn2_gu[l].astype(bf16), w_ffn2_down[l].astype(bf16))
```

```python
import jax
import jax.numpy as jnp
import numpy as np
from jax import lax
from jax.experimental import pallas as pl
from jax.experimental.pallas import tpu as pltpu

D = 1024
B = 32
S = 2048
T = B * S
GRID_W = 64
ROWS = S // GRID_W
N_HEADS = 16
HEAD_DIM = 64
WIN_ROWS = 8
WIN_COLS = 16
LRU_BLOCK = 64
LRU_C = 8.0
CONV_W = 4
CONV_LEFT = 2
D_FF = 2816
FFN_RES = 0.5
EPS = 1e-6
NEG = -1e30
LOG2E = float(np.log2(np.e))
TINY = 1e-37

SUBLANES = 8
BF16_ROWS = 16
LANES = 128
MXU_N = 256
VMEM_LIMIT = 56 * 1024 * 1024

TM_FFN = 512
TS_PROJ = 16
PITCH_PROJ = 24
TS_OUT = 16
PITCH_OUT = 40
FF_CHUNK = MXU_N
LRU_GW = MXU_N
KV_CHUNK = 256
HG = 4
HW = HG * HEAD_DIM
QR = 4
QC = 16
KR = 12
KC = 32
NQ = QR * QC
NKEY = KR * KC
N_CB = GRID_W // QC
N_RG = ROWS // QR
BIAS_PAD = 4
BIAS_LANES = 768

f32 = jnp.float32
bf16 = jnp.bfloat16


def _sigmoid(x):
  return 0.5 * jnp.tanh(0.5 * x) + 0.5


def _softplus(x):
  return jnp.maximum(x, 0.0) + jnp.log1p(jnp.exp(-jnp.abs(x)))


def _gelu_tanh(x, scale=1.0):
  c = float(np.sqrt(2.0 / np.pi))
  return (0.5 * scale) * x * (1.0 + jnp.tanh(c * (x + 0.044715 * (x * x * x))))


def _rmsnorm(x, g):
  ms = jnp.mean(x * x, axis=-1, keepdims=True)
  return (x * lax.rsqrt(ms + EPS)) * g


def _dot(a, b):
  return jnp.dot(a, b, preferred_element_type=f32)


def _const_spec(shape):
  nd = len(shape)
  return pl.BlockSpec(shape, lambda *_: (0,) * nd, pipeline_mode=pl.Buffered(1))


def _swiglu_hidden(x, g_ref, wgu_ref, act_ref):
  h = _rmsnorm(x, g_ref[...]).astype(bf16)
  for c in range(D_FF // FF_CHUNK):
    lo = c * FF_CHUNK
    gate = _dot(h, wgu_ref[:, lo:lo + FF_CHUNK])
    up = _dot(h, wgu_ref[:, D_FF + lo:D_FF + lo + FF_CHUNK])
    act_ref[:, lo:lo + FF_CHUNK] = (gate * _sigmoid(gate) * up).astype(bf16)


def _swiglu(x, g_ref, wgu_ref, wdn_ref, act_ref):
  _swiglu_hidden(x, g_ref, wgu_ref, act_ref)
  return _dot(act_ref[...], wdn_ref[...])


def _swap_rows(x, perm_ref, m, n, pitch):
  for i in range(m):
    for c in range(D // LANES):
      perm_ref[c, i * pitch:i * pitch + n, :] = x[i * n:(i + 1) * n, c * LANES:(c + 1) * LANES]
  return jnp.concatenate(
      [jnp.concatenate([perm_ref[c, pl.ds(j, m, stride=pitch), :] for c in range(D // LANES)], axis=-1)
       for j in range(n)], axis=0)


def _ffn1_kernel(x_ref, g_ref, wgu_ref, wdn_ref, gm_ref, o_ref, hm_ref, act_ref):
  _swiglu_hidden(x_ref[...], g_ref, wgu_ref, act_ref)
  half = TM_FFN // 2
  for lo in (0, half):
    x1 = x_ref[lo:lo + half, :] + FFN_RES * _dot(act_ref[lo:lo + half, :], wdn_ref[...])
    o_ref[lo:lo + half, :] = x1
    hm_ref[lo:lo + half, :] = _rmsnorm(x1, gm_ref[...]).astype(bf16)


def _ffn1(x2d, g, wgu, wdn, g_mix):
  spec = pl.BlockSpec((TM_FFN, D), lambda i: (i, 0))
  return pl.pallas_call(
      _ffn1_kernel,
      grid=(T // TM_FFN,),
      in_specs=[
          spec,
          _const_spec((1, D)),
          _const_spec((D, 2 * D_FF)),
          _const_spec((D_FF, D)),
          _const_spec((1, D)),
      ],
      out_specs=[spec, spec],
      out_shape=[jax.ShapeDtypeStruct((T, D), f32), jax.ShapeDtypeStruct((T, D), bf16)],
      scratch_shapes=[pltpu.VMEM((TM_FFN, D_FF), bf16)],
      compiler_params=pltpu.CompilerParams(
          dimension_semantics=("parallel",), vmem_limit_bytes=VMEM_LIMIT),
      name="ffn1",
  )(x2d, g, wgu, wdn, g_mix)


def _lru_consts(bg_ref, lam_ref):
  kexp = (-0.5 * LRU_C * LOG2E) * _softplus(-lam_ref[...])
  return kexp, 0.5 * bg_ref[0:1, :], 0.5 * bg_ref[1:2, :]


def _lru_block(xh, wg, kexp, b_r, b_i):
  g = _dot(xh.astype(bf16), wg)
  tr = jnp.tanh(g[:, :LRU_GW] + b_r)
  ti = jnp.tanh(g[:, LRU_GW:] + b_i)
  a = jnp.exp2(tr * kexp + kexp)
  v = 1.0 - a * a
  sq = v * lax.rsqrt(jnp.maximum(v, TINY))
  return a, sq * ((ti + 1.0) * xh)


def _gate_weights(w):
  per = LRU_GW // LRU_BLOCK
  ncb = D // LRU_GW
  w = w.reshape(2, ncb, per, LRU_BLOCK, LRU_BLOCK)
  eye = jnp.eye(per, dtype=w.dtype)
  dense = w[:, :, :, :, None, :] * eye[None, None, :, None, :, None]
  dense = dense.reshape(2, ncb, LRU_GW, LRU_GW)
  return (0.5 * jnp.concatenate([dense[0], dense[1]], axis=-1)).astype(bf16)


def _inproj_a_kernel(h_ref, hn_ref, w_ref, cw_ref, cb_ref, wg_ref, bg_ref, lam_ref,
                     xc_ref, hf_ref, g1_ref, g2_ref, perm_ref, halo_ref, xprev_ref, car_ref):
  j = pl.program_id(0)
  last = pl.num_programs(0) - 1
  rows = B * TS_PROJ

  @pl.when(j == 0)
  def _():
    xprev_ref[...] = jnp.zeros_like(xprev_ref)
    car_ref[...] = jnp.zeros_like(car_ref)

  h_bm = h_ref[...].reshape(rows, D)
  g2_ref[...] = _sigmoid(_dot(h_bm, w_ref[:, 3 * D:])).reshape(B, TS_PROJ, D)
  h_tm = _swap_rows(h_bm.astype(f32), perm_ref, B, TS_PROJ, PITCH_PROJ)
  hn = hn_ref[...].reshape(rows, D).astype(f32)
  for c in range(D // LANES):
    halo_ref[c] = hn[:, c * LANES:(c + 1) * LANES]
  hn0 = jnp.concatenate(
      [halo_ref[c, pl.ds(0, B, stride=TS_PROJ), :] for c in range(D // LANES)], axis=-1)

  lhs_x = jnp.concatenate([h_tm, hn0], axis=0).astype(bf16)
  h_tm = lhs_x[:rows]
  kexp, b_r, b_i = _lru_consts(bg_ref, lam_ref)

  for c in range(D // LRU_GW):
    cs = slice(c * LRU_GW, (c + 1) * LRU_GW)
    wcol = lambda i: w_ref[:, i * D + c * LRU_GW:i * D + (c + 1) * LRU_GW]
    xr = _dot(lhs_x, wcol(0))
    gr, ga = _dot(h_tm, wcol(1)), _dot(h_tm, wcol(2))
    nxt = jnp.where(j == last, 0.0, xr[rows:]).reshape(1, B, LRU_GW)
    xr = xr[:rows].reshape(TS_PROJ, B, LRU_GW)
    seq = jnp.concatenate([xprev_ref[:, :, cs], xr, nxt], axis=0)
    xprev_ref[:, :, cs] = xr[TS_PROJ - CONV_LEFT:]
    xc = cb_ref[:, cs]
    for k in range(CONV_W):
      xc = xc + seq[k:k + TS_PROJ] * cw_ref[k:k + 1, cs]
    xc_ref[:, :, cs] = xc

    a, bx = _lru_block(xc.reshape(rows, LRU_GW), wg_ref[c], kexp[:, cs], b_r[:, cs], b_i[:, cs])
    hcur = car_ref[:, cs]
    for s in range(TS_PROJ):
      hcur = a[s * B:(s + 1) * B] * hcur + bx[s * B:(s + 1) * B]
      hf_ref[s, :, cs] = hcur.astype(bf16)
    car_ref[:, cs] = hcur

    g1_ref[:, :, cs] = (_gelu_tanh(gr, scale=0.5) * _sigmoid(ga)).reshape(TS_PROJ, B, LRU_GW)


def _inproj_a(hm, w_a, conv_w, conv_b, wg, bg, lam):
  nt = S // TS_PROJ
  bm_spec = pl.BlockSpec((B, TS_PROJ, D), lambda j: (0, j, 0))
  tm_spec = pl.BlockSpec((TS_PROJ, B, D), lambda j: (j, 0, 0))
  next_spec = pl.BlockSpec((B, TS_PROJ, D), lambda j: (0, jnp.minimum(j + 1, nt - 1), 0))
  tm_out = jax.ShapeDtypeStruct((S, B, D), f32)
  return pl.pallas_call(
      _inproj_a_kernel,
      grid=(nt,),
      in_specs=[
          bm_spec,
          next_spec,
          _const_spec((D, 4 * D)),
          _const_spec((CONV_W, D)),
          _const_spec((1, D)),
          _const_spec((D // LRU_GW, LRU_GW, 2 * LRU_GW)),
          _const_spec((2, D)),
          _const_spec((1, D)),
      ],
      out_specs=[tm_spec, tm_spec, tm_spec, bm_spec],
      out_shape=[tm_out, jax.ShapeDtypeStruct((S, B, D), bf16), tm_out,
                 jax.ShapeDtypeStruct((B, S, D), f32)],
      scratch_shapes=[pltpu.VMEM((D // LANES, B * PITCH_PROJ, LANES), f32),
                      pltpu.VMEM((D // LANES, B * TS_PROJ, LANES), f32),
                      pltpu.VMEM((CONV_LEFT, B, D), f32),
                      pltpu.VMEM((B, D), f32)],
      compiler_params=pltpu.CompilerParams(
          dimension_semantics=("arbitrary",), vmem_limit_bytes=VMEM_LIMIT),
      name="inproj_a",
  )(hm, hm, w_a, conv_w, conv_b, wg, bg, lam)


def _inproj_b_kernel(h_ref, xc_ref, hf_ref, w_ref, qn_ref, kn_ref, ones_ref,
                     wg_ref, bg_ref, lam_ref, q_ref, k_ref, v_ref, hs_ref, car_ref):
  rows = B * TS_PROJ

  @pl.when(pl.program_id(0) == 0)
  def _():
    car_ref[...] = jnp.zeros_like(car_ref)

  h = h_ref[...].reshape(rows, D)
  kexp, b_r, b_i = _lru_consts(bg_ref, lam_ref)
  bm = lambda a: a.astype(bf16).reshape(B, TS_PROJ, MXU_N)
  head_ms = lambda t: _dot((t * t).astype(bf16), ones_ref[...]) * (1.0 / HEAD_DIM)

  for c in range(D // MXU_N):
    cs = slice(c * MXU_N, (c + 1) * MXU_N)
    wcol = lambda i: w_ref[:, i * D + c * MXU_N:i * D + (c + 1) * MXU_N]
    q, k, v = _dot(h, wcol(0)), _dot(h, wcol(1)), _dot(h, wcol(2))
    v_ref[:, :, cs] = bm(v)
    q = (q * lax.rsqrt(head_ms(q) + EPS)) * qn_ref[:, cs] * (HEAD_DIM ** -0.5 * LOG2E)
    q_ref[:, :, cs] = bm(q)
    k_ref[:, :, cs] = bm((k * lax.rsqrt(head_ms(k) + EPS)) * kn_ref[:, cs])

    xc = xc_ref[:, :, cs].reshape(rows, LRU_GW)
    a, bx = _lru_block(xc, wg_ref[c], kexp[:, cs], b_r[:, cs], b_i[:, cs])
    hcur = car_ref[:, cs]
    for s in reversed(range(TS_PROJ)):
      hcur = a[s * B:(s + 1) * B] * hcur + bx[s * B:(s + 1) * B]
      hs_ref[s, :, cs] = (hf_ref[s, :, cs].astype(f32) + hcur).astype(bf16)
    car_ref[:, cs] = hcur


def _inproj_b(hm, xc, hf, w_b, qn, kn, ones_bd, wg, bg, lam):
  nt = S // TS_PROJ
  bm_spec = pl.BlockSpec((B, TS_PROJ, D), lambda i: (0, nt - 1 - i, 0))
  tm_spec = pl.BlockSpec((TS_PROJ, B, D), lambda i: (nt - 1 - i, 0, 0))
  bm_out = jax.ShapeDtypeStruct((B, S, D), bf16)
  return pl.pallas_call(
      _inproj_b_kernel,
      grid=(nt,),
      in_specs=[
          bm_spec, tm_spec, tm_spec,
          _const_spec((D, 3 * D)),
          _const_spec((1, D)),
          _const_spec((1, D)),
          _const_spec((MXU_N, MXU_N)),
          _const_spec((D // LRU_GW, LRU_GW, 2 * LRU_GW)),
          _const_spec((2, D)),
          _const_spec((1, D)),
      ],
      out_specs=[bm_spec, bm_spec, bm_spec, tm_spec],
      out_shape=[bm_out, bm_out, bm_out, jax.ShapeDtypeStruct((S, B, D), bf16)],
      scratch_shapes=[pltpu.VMEM((B, D), f32)],
      compiler_params=pltpu.CompilerParams(
          dimension_semantics=("arbitrary",), vmem_limit_bytes=VMEM_LIMIT),
      name="inproj_b",
  )(hm, xc, hf, w_b, qn, kn, ones_bd, wg, bg, lam)


def _key_row0(rg):
  return int(np.clip(QR * rg - WIN_ROWS // 2, 0, ROWS - KR))


def _key_col0(n):
  return int(np.clip(QC * n - WIN_COLS // 2, 0, GRID_W - KC))


_KIND_ROWGROUPS = (0, 1, N_RG - 1)


def _table_kind(rg):
  return 0 if rg == 0 else (2 if rg == N_RG - 1 else 1)


def _natt_kernel(q_ref, k_ref, v_ref, g2_ref, cb_ref, ok_ref, o_ref, k8_ref, v8_ref, tbl_ref):
  @pl.when(pl.program_id(1) == 0)
  def _():
    for kind, rg in enumerate(_KIND_ROWGROUPS):
      for n in range(N_CB):
        for h in range(HG):
          wide = cb_ref[0, h, n]
          for r in range(QR):
            slot0 = _key_row0(rg) - (QR * rg + r) + WIN_ROWS - 1 + BIAS_PAD
            strip = pltpu.roll(wide, (BIAS_LANES - slot0 * KC) % BIAS_LANES, axis=1)[:, :NKEY]
            row0 = h * NQ + r * QC
            tbl_ref[kind, n, row0:row0 + QC, :] = jnp.where(
                ok_ref[kind, r:r + 1, :] != 0, strip, NEG)

  half = BF16_ROWS // 2
  for src, dst in ((k_ref, k8_ref), (v_ref, v8_ref)):
    for c0 in range(0, S - BF16_ROWS, KV_CHUNK):
      n = min(KV_CHUNK, S - BF16_ROWS - c0)
      wide = src[c0:c0 + n + BF16_ROWS, :].astype(f32)
      dst[c0:c0 + n, :] = wide[half:half + n].astype(bf16)

  def key_block(ref0, ref8, rows):
    parts = []
    for r in rows:
      parts.append(ref0[r:r + KC, :] if r % BF16_ROWS == 0 else ref8[r - half:r - half + KC, :])
    return jnp.concatenate(parts, axis=0)

  lane_head = lax.broadcasted_iota(jnp.int32, (NQ, HW), 1) // HEAD_DIM
  for rg in range(N_RG):
    k0 = _key_row0(rg)
    for n in range(N_CB):
      c0 = _key_col0(n)
      q_rows = [(QR * rg + rl) * GRID_W + QC * n for rl in range(QR)]
      k_rows = [(k0 + i) * GRID_W + c0 for i in range(KR)]
      qs = jnp.concatenate([q_ref[r:r + QC, :] for r in q_rows], axis=0)
      xq = jnp.concatenate(
          [jnp.where(lane_head == h, qs, jnp.zeros_like(qs)) for h in range(HG)], axis=0)
      kb = key_block(k_ref, k8_ref, k_rows)
      vb = key_block(v_ref, v8_ref, k_rows)
      s = lax.dot_general(xq, kb, (((1,), (1,)), ((), ())), preferred_element_type=f32)
      s = s + tbl_ref[_table_kind(rg), n]
      p = jnp.exp2(s - jnp.max(s, axis=1, keepdims=True))
      inv = 1.0 / jnp.sum(p, axis=1, keepdims=True)
      o = _dot(p.astype(bf16), vb) * inv
      acc = jnp.where(lane_head == 0, o[0:NQ], 0.0)
      for h in range(1, HG):
        acc = jnp.where(lane_head == h, o[h * NQ:(h + 1) * NQ], acc)
      for rl, r in enumerate(q_rows):
        o_ref[r:r + QC, :] = g2_ref[r:r + QC, :] * acc[rl * QC:(rl + 1) * QC]


def _natt(q, k, v, g2, col_bias, row_ok):
  ng = N_HEADS // HG
  spec = pl.BlockSpec((None, S, HW), lambda g, b: (b, 0, g))
  return pl.pallas_call(
      _natt_kernel,
      grid=(ng, B),
      in_specs=[
          spec, spec, spec, spec,
          pl.BlockSpec((1, HG, N_CB, QC, BIAS_LANES), lambda g, b: (g, 0, 0, 0, 0)),
          pl.BlockSpec((len(_KIND_ROWGROUPS), QR, NKEY), lambda g, b: (0, 0, 0)),
      ],
      out_specs=spec,
      out_shape=jax.ShapeDtypeStruct((B, S, D), f32),
      scratch_shapes=[pltpu.VMEM((S, HW), bf16), pltpu.VMEM((S, HW), bf16),
                      pltpu.VMEM((len(_KIND_ROWGROUPS), N_CB, HG * NQ, NKEY), f32)],
      compiler_params=pltpu.CompilerParams(
          dimension_semantics=("arbitrary", "arbitrary"), vmem_limit_bytes=VMEM_LIMIT),
      name="natt",
  )(q, k, v, g2, col_bias, row_ok)


def _bias_columns(rpb):
  n_dr, n_dc = 2 * WIN_ROWS - 1, 2 * WIN_COLS - 1
  c = np.arange(KC)[:, None]
  ql = np.arange(QC)[None, :]
  col_hot, col_ok = [], []
  for n in range(N_CB):
    qc = QC * n + ql
    kc = _key_col0(n) + c
    cs = np.clip(qc - WIN_COLS // 2, 0, GRID_W - WIN_COLS)
    col_ok.append((kc >= cs) & (kc < cs + WIN_COLS))
    col_hot.append(np.eye(n_dc)[np.clip(kc - qc, -(WIN_COLS - 1), WIN_COLS - 1) + WIN_COLS - 1])
  col_ok = np.stack(col_ok).transpose(0, 2, 1)
  ng = N_HEADS // HG
  bias = jnp.einsum("ghab,ncqb->ghnqac", rpb.astype(f32).reshape(ng, HG, n_dr, n_dc),
                    jnp.asarray(np.stack(col_hot), f32), precision=lax.Precision.HIGHEST)
  bias = jnp.where(col_ok[None, None, :, :, None, :], LOG2E * bias, NEG)
  slots = BIAS_LANES // KC
  bias = jnp.pad(bias, ((0, 0),) * 4 + ((BIAS_PAD, slots - BIAS_PAD - n_dr), (0, 0)))
  return bias.reshape(ng, HG, N_CB, QC, BIAS_LANES)


def _row_window_mask():
  i = np.arange(KR)[:, None]
  rl = np.arange(QR)[None, :]
  out = []
  for rg in _KIND_ROWGROUPS:
    r = QR * rg + rl
    rs = np.clip(r - WIN_ROWS // 2, 0, ROWS - WIN_ROWS)
    krow = _key_row0(rg) + i
    ok = (krow >= rs) & (krow < rs + WIN_ROWS)
    out.append(np.repeat(ok.T[:, :, None], KC, axis=2).reshape(QR, NKEY))
  return jnp.asarray(np.stack(out), jnp.int32)


def _ffn2_kernel(x1_ref, hs_ref, g1_ref, ya_ref, wo_ref, g_ref, wgu_ref, wdn_ref,
                 o_ref, act_ref, perm_ref):
  rows = B * TS_OUT
  y_lru = (g1_ref[...] * hs_ref[...].astype(f32)).reshape(rows, D)
  y_lru = _swap_rows(y_lru, perm_ref, TS_OUT, B, PITCH_OUT)
  y = y_lru + ya_ref[...].reshape(rows, D)
  x2 = x1_ref[...].reshape(rows, D) + _dot(y.astype(bf16), wo_ref[...])
  out = x2 + FFN_RES * _swiglu(x2, g_ref, wgu_ref, wdn_ref, act_ref)
  o_ref[...] = out.reshape(B, TS_OUT, D)


def _ffn2(x1, hs, g1, ya, wo, g, wgu, wdn):
  bm_spec = pl.BlockSpec((B, TS_OUT, D), lambda j: (0, j, 0))
  tm_spec = pl.BlockSpec((TS_OUT, B, D), lambda j: (j, 0, 0))
  return pl.pallas_call(
      _ffn2_kernel,
      grid=(S // TS_OUT,),
      in_specs=[bm_spec, tm_spec, tm_spec, bm_spec,
                _const_spec((D, D)),
                _const_spec((1, D)),
                _const_spec((D, 2 * D_FF)),
                _const_spec((D_FF, D))],
      out_specs=bm_spec,
      out_shape=jax.ShapeDtypeStruct((B, S, D), f32),
      scratch_shapes=[pltpu.VMEM((B * TS_OUT, D_FF), bf16),
                      pltpu.VMEM((D // LANES, TS_OUT * PITCH_OUT, LANES), f32)],
      compiler_params=pltpu.CompilerParams(
          dimension_semantics=("parallel",), vmem_limit_bytes=VMEM_LIMIT),
      name="ffn2",
  )(x1, hs, g1, ya, wo, g, wgu, wdn)


def kernel(x, norm_ffn1, w_ffn1_gu, w_ffn1_down, norm_mix, w_in, conv_w, conv_b,
           lru_w_gates, lru_b_gates, lru_lambda, q_norm, k_norm, rel_pos_bias, w_out,
           norm_ffn2, w_ffn2_gu, w_ffn2_down):
  assert x.shape == (B, S, D) and norm_ffn1.shape[0] == 1
  l = 0
  row = lambda a: a.reshape(1, D).astype(f32)
  ones_bd = jnp.asarray(
      np.kron(np.eye(MXU_N // HEAD_DIM), np.ones((HEAD_DIM, HEAD_DIM))), dtype=bf16)
  wi = w_in[l].astype(bf16)
  w_a = jnp.concatenate([wi[:, :2 * D], wi[:, 5 * D:]], axis=1)
  w_b = wi[:, 2 * D:5 * D]

  x1, hm = _ffn1(x.reshape(T, D), row(norm_ffn1[l]), w_ffn1_gu[l].astype(bf16),
                 w_ffn1_down[l].astype(bf16), row(norm_mix[l]))
  x1, hm = x1.reshape(B, S, D), hm.reshape(B, S, D)

  gates = lambda d: (_gate_weights(lru_w_gates[l, d]), lru_b_gates[l, d].astype(f32),
                     row(lru_lambda[l, d]))
  xc, hf, g1, g2 = _inproj_a(hm, w_a, conv_w[l].astype(f32), row(conv_b[l]), *gates(0))
  q, k, v, hs = _inproj_b(hm, xc, hf, w_b,
                          row(jnp.tile(q_norm[l], N_HEADS)), row(jnp.tile(k_norm[l], N_HEADS)),
                          ones_bd, *gates(1))

  ya = _natt(q, k, v, g2, _bias_columns(rel_pos_bias[l]), _row_window_mask())

  return _ffn2(x1, hs, g1, ya, w_out[l].astype(bf16),
               row(norm_ffn2[l]), w_ffn2_gu[l].astype(bf16), w_ffn2_down[l].astype(bf16))
```

```python
import jax
import jax.numpy as jnp
import numpy as np
from jax import lax
from jax.experimental import pallas as pl
from jax.experimental.pallas import tpu as pltpu

D = 1024
B = 32
S = 2048
T = B * S
GRID_W = 64
ROWS = S // GRID_W
N_HEADS = 16
HEAD_DIM = 64
WIN_ROWS = 8
WIN_COLS = 16
LRU_BLOCK = 64
LRU_C = 8.0
CONV_W = 4
CONV_LEFT = 2
D_FF = 2816
FFN_RES = 0.5
EPS = 1e-6
NEG = -1e30
LOG2E = float(np.log2(np.e))
TINY = 1e-37

SUBLANES = 8
BF16_ROWS = 16
LANES = 128
MXU_N = 256
VMEM_LIMIT = 56 * 1024 * 1024

TM_FFN = 1024
TS_PROJ = 16
TS_B = 32
PITCH_PROJ = 24
TS_OUT = 16
PITCH_OUT = 40
FF_CHUNK = MXU_N
LRU_GW = MXU_N
KV_CHUNK = 256
HG = 4
HW = HG * HEAD_DIM
QR = 4
QC = 16
KR = 12
KC = 32
NQ = QR * QC
NKEY = KR * KC
N_CB = GRID_W // QC
N_RG = ROWS // QR
BIAS_PAD = 4
BIAS_LANES = 768

f32 = jnp.float32
bf16 = jnp.bfloat16


def _sigmoid(x):
  return 0.5 * jnp.tanh(0.5 * x) + 0.5


def _softplus(x):
  return jnp.maximum(x, 0.0) + jnp.log1p(jnp.exp(-jnp.abs(x)))


def _gelu_tanh(x, scale=1.0):
  c = float(np.sqrt(2.0 / np.pi))
  return (0.5 * scale) * x * (1.0 + jnp.tanh(c * (x + 0.044715 * (x * x * x))))


def _rmsnorm(x, g):
  ms = jnp.mean(x * x, axis=-1, keepdims=True)
  return (x * lax.rsqrt(ms + EPS)) * g


def _dot(a, b):
  return jnp.dot(a, b, preferred_element_type=f32)


def _const_spec(shape):
  nd = len(shape)
  return pl.BlockSpec(shape, lambda *_: (0,) * nd, pipeline_mode=pl.Buffered(1))


def _swiglu_hidden(x, g_ref, wgu_ref, act_ref):
  h = _rmsnorm(x, g_ref[...]).astype(bf16)
  for c in range(D_FF // FF_CHUNK):
    lo = c * FF_CHUNK
    gate = _dot(h, wgu_ref[:, lo:lo + FF_CHUNK])
    up = _dot(h, wgu_ref[:, D_FF + lo:D_FF + lo + FF_CHUNK])
    act_ref[:, lo:lo + FF_CHUNK] = (gate * _sigmoid(gate) * up).astype(bf16)


def _swiglu(x, g_ref, wgu_ref, wdn_ref, act_ref):
  _swiglu_hidden(x, g_ref, wgu_ref, act_ref)
  return _dot(act_ref[...], wdn_ref[...])


def _swap_rows(x, perm_ref, m, n, pitch):
  for i in range(m):
    for c in range(D // LANES):
      perm_ref[c, i * pitch:i * pitch + n, :] = x[i * n:(i + 1) * n, c * LANES:(c + 1) * LANES]
  return jnp.concatenate(
      [jnp.concatenate([perm_ref[c, pl.ds(j, m, stride=pitch), :] for c in range(D // LANES)], axis=-1)
       for j in range(n)], axis=0)


def _ffn1_kernel(x_ref, g_ref, wgu_ref, wdn_ref, gm_ref, o_ref, hm_ref, act_ref):
  _swiglu_hidden(x_ref[...], g_ref, wgu_ref, act_ref)
  half = TM_FFN // 2
  for lo in (0, half):
    x1 = x_ref[lo:lo + half, :] + FFN_RES * _dot(act_ref[lo:lo + half, :], wdn_ref[...])
    o_ref[lo:lo + half, :] = x1
    hm_ref[lo:lo + half, :] = _rmsnorm(x1, gm_ref[...]).astype(bf16)


def _ffn1(x2d, g, wgu, wdn, g_mix):
  spec = pl.BlockSpec((TM_FFN, D), lambda i: (i, 0))
  return pl.pallas_call(
      _ffn1_kernel,
      grid=(T // TM_FFN,),
      in_specs=[
          spec,
          _const_spec((1, D)),
          _const_spec((D, 2 * D_FF)),
          _const_spec((D_FF, D)),
          _const_spec((1, D)),
      ],
      out_specs=[spec, spec],
      out_shape=[jax.ShapeDtypeStruct((T, D), f32), jax.ShapeDtypeStruct((T, D), bf16)],
      scratch_shapes=[pltpu.VMEM((TM_FFN, D_FF), bf16)],
      compiler_params=pltpu.CompilerParams(
          dimension_semantics=("parallel",), vmem_limit_bytes=VMEM_LIMIT),
      name="ffn1",
  )(x2d, g, wgu, wdn, g_mix)


def _lru_consts(bg_ref, lam_ref):
  kexp = (-0.5 * LRU_C * LOG2E) * _softplus(-lam_ref[...])
  return kexp, 0.5 * bg_ref[0:1, :], 0.5 * bg_ref[1:2, :]


def _lru_block(xh, wg, kexp, b_r, b_i):
  g = _dot(xh.astype(bf16), wg)
  tr = jnp.tanh(g[:, :LRU_GW] + b_r)
  ti = jnp.tanh(g[:, LRU_GW:] + b_i)
  a = jnp.exp2(tr * kexp + kexp)
  v = 1.0 - a * a
  sq = v * lax.rsqrt(jnp.maximum(v, TINY))
  return a, sq * ((ti + 1.0) * xh)


def _gate_weights(w):
  per = LRU_GW // LRU_BLOCK
  ncb = D // LRU_GW
  w = w.reshape(2, ncb, per, LRU_BLOCK, LRU_BLOCK)
  eye = jnp.eye(per, dtype=w.dtype)
  dense = w[:, :, :, :, None, :] * eye[None, None, :, None, :, None]
  dense = dense.reshape(2, ncb, LRU_GW, LRU_GW)
  return (0.5 * jnp.concatenate([dense[0], dense[1]], axis=-1)).astype(bf16)


def _inproj_a_kernel(h_ref, hn_ref, w_ref, cw_ref, cb_ref, wg_ref, bg_ref, lam_ref,
                     xc_ref, hf_ref, g1_ref, g2_ref, perm_ref, halo_ref, xprev_ref, car_ref):
  j = pl.program_id(0)
  last = pl.num_programs(0) - 1
  rows = B * TS_PROJ

  @pl.when(j == 0)
  def _():
    xprev_ref[...] = jnp.zeros_like(xprev_ref)
    car_ref[...] = jnp.zeros_like(car_ref)

  h_bm = h_ref[...].reshape(rows, D)
  g2_ref[...] = _sigmoid(_dot(h_bm, w_ref[:, 3 * D:])).reshape(B, TS_PROJ, D)
  h_tm = _swap_rows(h_bm.astype(f32), perm_ref, B, TS_PROJ, PITCH_PROJ)
  hn = hn_ref[...].reshape(rows, D).astype(f32)
  for c in range(D // LANES):
    halo_ref[c] = hn[:, c * LANES:(c + 1) * LANES]
  hn0 = jnp.concatenate(
      [halo_ref[c, pl.ds(0, B, stride=TS_PROJ), :] for c in range(D // LANES)], axis=-1)

  lhs_x = jnp.concatenate([h_tm, hn0], axis=0).astype(bf16)
  h_tm = lhs_x[:rows]
  kexp, b_r, b_i = _lru_consts(bg_ref, lam_ref)

  for c in range(D // LRU_GW):
    cs = slice(c * LRU_GW, (c + 1) * LRU_GW)
    wcol = lambda i: w_ref[:, i * D + c * LRU_GW:i * D + (c + 1) * LRU_GW]
    xr = _dot(lhs_x, wcol(0))
    gr, ga = _dot(h_tm, wcol(1)), _dot(h_tm, wcol(2))
    nxt = jnp.where(j == last, 0.0, xr[rows:]).reshape(1, B, LRU_GW)
    xr = xr[:rows].reshape(TS_PROJ, B, LRU_GW)
    seq = jnp.concatenate([xprev_ref[:, :, cs], xr, nxt], axis=0)
    xprev_ref[:, :, cs] = xr[TS_PROJ - CONV_LEFT:]
    xc = cb_ref[:, cs]
    for k in range(CONV_W):
      xc = xc + seq[k:k + TS_PROJ] * cw_ref[k:k + 1, cs]
    xc_ref[:, :, cs] = xc

    a, bx = _lru_block(xc.reshape(rows, LRU_GW), wg_ref[c], kexp[:, cs], b_r[:, cs], b_i[:, cs])
    hcur = car_ref[:, cs]
    for s in range(TS_PROJ):
      hcur = a[s * B:(s + 1) * B] * hcur + bx[s * B:(s + 1) * B]
      hf_ref[s, :, cs] = hcur.astype(bf16)
    car_ref[:, cs] = hcur

    g1_ref[:, :, cs] = (_gelu_tanh(gr, scale=0.5) * _sigmoid(ga)).reshape(TS_PROJ, B, LRU_GW)


def _inproj_a(hm, w_a, conv_w, conv_b, wg, bg, lam):
  nt = S // TS_PROJ
  bm_spec = pl.BlockSpec((B, TS_PROJ, D), lambda j: (0, j, 0))
  tm_spec = pl.BlockSpec((TS_PROJ, B, D), lambda j: (j, 0, 0))
  next_spec = pl.BlockSpec((B, TS_PROJ, D), lambda j: (0, jnp.minimum(j + 1, nt - 1), 0))
  tm_out = jax.ShapeDtypeStruct((S, B, D), f32)
  return pl.pallas_call(
      _inproj_a_kernel,
      grid=(nt,),
      in_specs=[
          bm_spec,
          next_spec,
          _const_spec((D, 4 * D)),
          _const_spec((CONV_W, D)),
          _const_spec((1, D)),
          _const_spec((D // LRU_GW, LRU_GW, 2 * LRU_GW)),
          _const_spec((2, D)),
          _const_spec((1, D)),
      ],
      out_specs=[tm_spec, tm_spec, tm_spec, bm_spec],
      out_shape=[tm_out, jax.ShapeDtypeStruct((S, B, D), bf16), tm_out,
                 jax.ShapeDtypeStruct((B, S, D), f32)],
      scratch_shapes=[pltpu.VMEM((D // LANES, B * PITCH_PROJ, LANES), f32),
                      pltpu.VMEM((D // LANES, B * TS_PROJ, LANES), f32),
                      pltpu.VMEM((CONV_LEFT, B, D), f32),
                      pltpu.VMEM((B, D), f32)],
      compiler_params=pltpu.CompilerParams(
          dimension_semantics=("arbitrary",), vmem_limit_bytes=VMEM_LIMIT),
      name="inproj_a",
  )(hm, hm, w_a, conv_w, conv_b, wg, bg, lam)


def _inproj_b_kernel(h_ref, xc_ref, hf_ref, w_ref, qn_ref, kn_ref, ones_ref,
                     wg_ref, bg_ref, lam_ref, q_ref, k_ref, v_ref, hs_ref, car_ref):
  rows = B * TS_B

  @pl.when(pl.program_id(0) == 0)
  def _():
    car_ref[...] = jnp.zeros_like(car_ref)

  h = h_ref[...].reshape(rows, D)
  kexp, b_r, b_i = _lru_consts(bg_ref, lam_ref)
  bm = lambda a: a.astype(bf16).reshape(B, TS_B, MXU_N)
  head_ms = lambda t: _dot((t * t).astype(bf16), ones_ref[...]) * (1.0 / HEAD_DIM)

  for c in range(D // MXU_N):
    cs = slice(c * MXU_N, (c + 1) * MXU_N)
    wcol = lambda i: w_ref[:, i * D + c * MXU_N:i * D + (c + 1) * MXU_N]
    q, k, v = _dot(h, wcol(0)), _dot(h, wcol(1)), _dot(h, wcol(2))
    v_ref[:, :, cs] = bm(v)
    q = (q * lax.rsqrt(head_ms(q) + EPS)) * qn_ref[:, cs] * (HEAD_DIM ** -0.5 * LOG2E)
    q_ref[:, :, cs] = bm(q)
    k_ref[:, :, cs] = bm((k * lax.rsqrt(head_ms(k) + EPS)) * kn_ref[:, cs])

    xc = xc_ref[:, :, cs].reshape(rows, LRU_GW)
    a, bx = _lru_block(xc, wg_ref[c], kexp[:, cs], b_r[:, cs], b_i[:, cs])
    hcur = car_ref[:, cs]
    for s in reversed(range(TS_B)):
      hcur = a[s * B:(s + 1) * B] * hcur + bx[s * B:(s + 1) * B]
      hs_ref[s, :, cs] = (hf_ref[s, :, cs].astype(f32) + hcur).astype(bf16)
    car_ref[:, cs] = hcur


def _inproj_b(hm, xc, hf, w_b, qn, kn, ones_bd, wg, bg, lam):
  nt = S // TS_B
  bm_spec = pl.BlockSpec((B, TS_B, D), lambda i: (0, nt - 1 - i, 0))
  tm_spec = pl.BlockSpec((TS_B, B, D), lambda i: (nt - 1 - i, 0, 0))
  bm_out = jax.ShapeDtypeStruct((B, S, D), bf16)
  return pl.pallas_call(
      _inproj_b_kernel,
      grid=(nt,),
      in_specs=[
          bm_spec, tm_spec, tm_spec,
          _const_spec((D, 3 * D)),
          _const_spec((1, D)),
          _const_spec((1, D)),
          _const_spec((MXU_N, MXU_N)),
          _const_spec((D // LRU_GW, LRU_GW, 2 * LRU_GW)),
          _const_spec((2, D)),
          _const_spec((1, D)),
      ],
      out_specs=[bm_spec, bm_spec, bm_spec, tm_spec],
      out_shape=[bm_out, bm_out, bm_out, jax.ShapeDtypeStruct((S, B, D), bf16)],
      scratch_shapes=[pltpu.VMEM((B, D), f32)],
      compiler_params=pltpu.CompilerParams(
          dimension_semantics=("arbitrary",), vmem_limit_bytes=VMEM_LIMIT),
      name="inproj_b",
  )(hm, xc, hf, w_b, qn, kn, ones_bd, wg, bg, lam)


def _key_row0(rg):
  return int(np.clip(QR * rg - WIN_ROWS // 2, 0, ROWS - KR))


def _key_col0(n):
  return int(np.clip(QC * n - WIN_COLS // 2, 0, GRID_W - KC))


_KIND_ROWGROUPS = (0, 1, N_RG - 1)


def _table_kind(rg):
  return 0 if rg == 0 else (2 if rg == N_RG - 1 else 1)


def _natt_kernel(q_ref, k_ref, v_ref, g2_ref, cb_ref, ok_ref, o_ref, k8_ref, v8_ref, tbl_ref):
  @pl.when(pl.program_id(1) == 0)
  def _():
    for kind, rg in enumerate(_KIND_ROWGROUPS):
      for n in range(N_CB):
        for h in range(HG):
          wide = cb_ref[0, h, n]
          for r in range(QR):
            slot0 = _key_row0(rg) - (QR * rg + r) + WIN_ROWS - 1 + BIAS_PAD
            strip = pltpu.roll(wide, (BIAS_LANES - slot0 * KC) % BIAS_LANES, axis=1)[:, :NKEY]
            row0 = h * NQ + r * QC
            tbl_ref[kind, n, row0:row0 + QC, :] = jnp.where(
                ok_ref[kind, r:r + 1, :] != 0, strip, NEG)

  half = BF16_ROWS // 2
  for src, dst in ((k_ref, k8_ref), (v_ref, v8_ref)):
    for c0 in range(0, S - BF16_ROWS, KV_CHUNK):
      n = min(KV_CHUNK, S - BF16_ROWS - c0)
      wide = src[c0:c0 + n + BF16_ROWS, :].astype(f32)
      dst[c0:c0 + n, :] = wide[half:half + n].astype(bf16)

  def key_block(ref0, ref8, rows):
    parts = []
    for r in rows:
      parts.append(ref0[r:r + KC, :] if r % BF16_ROWS == 0 else ref8[r - half:r - half + KC, :])
    return jnp.concatenate(parts, axis=0)

  lane_head = lax.broadcasted_iota(jnp.int32, (NQ, HW), 1) // HEAD_DIM
  for rg in range(N_RG):
    k0 = _key_row0(rg)
    for n in range(N_CB):
      c0 = _key_col0(n)
      q_rows = [(QR * rg + rl) * GRID_W + QC * n for rl in range(QR)]
      k_rows = [(k0 + i) * GRID_W + c0 for i in range(KR)]
      qs = jnp.concatenate([q_ref[r:r + QC, :] for r in q_rows], axis=0)
      xq = jnp.concatenate(
          [jnp.where(lane_head == h, qs, jnp.zeros_like(qs)) for h in range(HG)], axis=0)
      kb = key_block(k_ref, k8_ref, k_rows)
      vb = key_block(v_ref, v8_ref, k_rows)
      s = lax.dot_general(xq, kb, (((1,), (1,)), ((), ())), preferred_element_type=f32)
      s = s + tbl_ref[_table_kind(rg), n]
      p = jnp.exp2(s - jnp.max(s, axis=1, keepdims=True))
      inv = 1.0 / jnp.sum(p, axis=1, keepdims=True)
      o = _dot(p.astype(bf16), vb) * inv
      acc = jnp.where(lane_head == 0, o[0:NQ], 0.0)
      for h in range(1, HG):
        acc = jnp.where(lane_head == h, o[h * NQ:(h + 1) * NQ], acc)
      for rl, r in enumerate(q_rows):
        o_ref[r:r + QC, :] = g2_ref[r:r + QC, :] * acc[rl * QC:(rl + 1) * QC]


def _natt(q, k, v, g2, col_bias, row_ok):
  ng = N_HEADS // HG
  spec = pl.BlockSpec((None, S, HW), lambda g, b: (b, 0, g))
  return pl.pallas_call(
      _natt_kernel,
      grid=(ng, B),
      in_specs=[
          spec, spec, spec, spec,
          pl.BlockSpec((1, HG, N_CB, QC, BIAS_LANES), lambda g, b: (g, 0, 0, 0, 0)),
          pl.BlockSpec((len(_KIND_ROWGROUPS), QR, NKEY), lambda g, b: (0, 0, 0)),
      ],
      out_specs=spec,
      out_shape=jax.ShapeDtypeStruct((B, S, D), f32),
      scratch_shapes=[pltpu.VMEM((S, HW), bf16), pltpu.VMEM((S, HW), bf16),
                      pltpu.VMEM((len(_KIND_ROWGROUPS), N_CB, HG * NQ, NKEY), f32)],
      compiler_params=pltpu.CompilerParams(
          dimension_semantics=("arbitrary", "arbitrary"), vmem_limit_bytes=VMEM_LIMIT),
      name="natt",
  )(q, k, v, g2, col_bias, row_ok)


def _bias_columns(rpb):
  n_dr, n_dc = 2 * WIN_ROWS - 1, 2 * WIN_COLS - 1
  c = np.arange(KC)[:, None]
  ql = np.arange(QC)[None, :]
  col_hot, col_ok = [], []
  for n in range(N_CB):
    qc = QC * n + ql
    kc = _key_col0(n) + c
    cs = np.clip(qc - WIN_COLS // 2, 0, GRID_W - WIN_COLS)
    col_ok.append((kc >= cs) & (kc < cs + WIN_COLS))
    col_hot.append(np.eye(n_dc)[np.clip(kc - qc, -(WIN_COLS - 1), WIN_COLS - 1) + WIN_COLS - 1])
  col_ok = np.stack(col_ok).transpose(0, 2, 1)
  ng = N_HEADS // HG
  bias = jnp.einsum("ghab,ncqb->ghnqac", rpb.astype(f32).reshape(ng, HG, n_dr, n_dc),
                    jnp.asarray(np.stack(col_hot), f32), precision=lax.Precision.HIGHEST)
  bias = jnp.where(col_ok[None, None, :, :, None, :], LOG2E * bias, NEG)
  slots = BIAS_LANES // KC
  bias = jnp.pad(bias, ((0, 0),) * 4 + ((BIAS_PAD, slots - BIAS_PAD - n_dr), (0, 0)))
  return bias.reshape(ng, HG, N_CB, QC, BIAS_LANES)


def _row_window_mask():
  i = np.arange(KR)[:, None]
  rl = np.arange(QR)[None, :]
  out = []
  for rg in _KIND_ROWGROUPS:
    r = QR * rg + rl
    rs = np.clip(r - WIN_ROWS // 2, 0, ROWS - WIN_ROWS)
    krow = _key_row0(rg) + i
    ok = (krow >= rs) & (krow < rs + WIN_ROWS)
    out.append(np.repeat(ok.T[:, :, None], KC, axis=2).reshape(QR, NKEY))
  return jnp.asarray(np.stack(out), jnp.int32)


def _ffn2_kernel(x1_ref, hs_ref, g1_ref, ya_ref, wo_ref, g_ref, wgu_ref, wdn_ref,
                 o_ref, act_ref, perm_ref):
  rows = B * TS_OUT
  y_lru = (g1_ref[...] * hs_ref[...].astype(f32)).reshape(rows, D)
  y_lru = _swap_rows(y_lru, perm_ref, TS_OUT, B, PITCH_OUT)
  y = y_lru + ya_ref[...].reshape(rows, D)
  x2 = x1_ref[...].reshape(rows, D) + _dot(y.astype(bf16), wo_ref[...])
  out = x2 + FFN_RES * _swiglu(x2, g_ref, wgu_ref, wdn_ref, act_ref)
  o_ref[...] = out.reshape(B, TS_OUT, D)


def _ffn2(x1, hs, g1, ya, wo, g, wgu, wdn):
  bm_spec = pl.BlockSpec((B, TS_OUT, D), lambda j: (0, j, 0))
  tm_spec = pl.BlockSpec((TS_OUT, B, D), lambda j: (j, 0, 0))
  return pl.pallas_call(
      _ffn2_kernel,
      grid=(S // TS_OUT,),
      in_specs=[bm_spec, tm_spec, tm_spec, bm_spec,
                _const_spec((D, D)),
                _const_spec((1, D)),
                _const_spec((D, 2 * D_FF)),
                _const_spec((D_FF, D))],
      out_specs=bm_spec,
      out_shape=jax.ShapeDtypeStruct((B, S, D), f32),
      scratch_shapes=[pltpu.VMEM((B * TS_OUT, D_FF), bf16),
                      pltpu.VMEM((D // LANES, TS_OUT * PITCH_OUT, LANES), f32)],
      compiler_params=pltpu.CompilerParams(
          dimension_semantics=("parallel",), vmem_limit_bytes=VMEM_LIMIT),
      name="ffn2",
  )(x1, hs, g1, ya, wo, g, wgu, wdn)


def kernel(x, norm_ffn1, w_ffn1_gu, w_ffn1_down, norm_mix, w_in, conv_w, conv_b,
           lru_w_gates, lru_b_gates, lru_lambda, q_norm, k_norm, rel_pos_bias, w_out,
           norm_ffn2, w_ffn2_gu, w_ffn2_down):
  assert x.shape == (B, S, D) and norm_ffn1.shape[0] == 1
  l = 0
  row = lambda a: a.reshape(1, D).astype(f32)
  ones_bd = jnp.asarray(
      np.kron(np.eye(MXU_N // HEAD_DIM), np.ones((HEAD_DIM, HEAD_DIM))), dtype=bf16)
  wi = w_in[l].astype(bf16)
  w_a = jnp.concatenate([wi[:, :2 * D], wi[:, 5 * D:]], axis=1)
  w_b = wi[:, 2 * D:5 * D]

  x1, hm = _ffn1(x.reshape(T, D), row(norm_ffn1[l]), w_ffn1_gu[l].astype(bf16),
                 w_ffn1_down[l].astype(bf16), row(norm_mix[l]))
  x1, hm = x1.reshape(B, S, D), hm.reshape(B, S, D)

  gates = lambda d: (_gate_weights(lru_w_gates[l, d]), lru_b_gates[l, d].astype(f32),
                     row(lru_lambda[l, d]))
  xc, hf, g1, g2 = _inproj_a(hm, w_a, conv_w[l].astype(f32), row(conv_b[l]), *gates(0))
  q, k, v, hs = _inproj_b(hm, xc, hf, w_b,
                          row(jnp.tile(q_norm[l], N_HEADS)), row(jnp.tile(k_norm[l], N_HEADS)),
                          ones_bd, *gates(1))

  ya = _natt(q, k, v, g2, _bias_columns(rel_pos_bias[l]), _row_window_mask())

  return _ffn2(x1, hs, g1, ya, w_out[l].astype(bf16),
               row(norm_ffn2[l]), w_ffn2_gu[l].astype(bf16), w_ffn2_down[l].astype(bf16))
```

```python
import jax
import jax.numpy as jnp
import numpy as np
from jax import lax
from jax.experimental import pallas as pl
from jax.experimental.pallas import tpu as pltpu

D = 1024
B = 32
S = 2048
T = B * S
GRID_W = 64
ROWS = S // GRID_W
N_HEADS = 16
HEAD_DIM = 64
WIN_ROWS = 8
WIN_COLS = 16
LRU_BLOCK = 64
LRU_C = 8.0
CONV_W = 4
CONV_LEFT = 2
D_FF = 2816
FFN_RES = 0.5
EPS = 1e-6
NEG = -1e30
LOG2E = float(np.log2(np.e))
TINY = 1e-37

SUBLANES = 8
BF16_ROWS = 16
LANES = 128
MXU_N = 256
VMEM_LIMIT = 56 * 1024 * 1024

TM_FFN = 1024
TS_PROJ = 32
TS_HALO = BF16_ROWS
TS_B = 32
PITCH_PROJ = 40
TS_OUT = 16
PITCH_OUT = 40
FF_CHUNK = MXU_N
LRU_GW = MXU_N
KV_CHUNK = 256
HG = 4
HW = HG * HEAD_DIM
QR = 4
QC = 16
KR = 12
KC = 32
NQ = QR * QC
NKEY = KR * KC
N_CB = GRID_W // QC
N_RG = ROWS // QR
BIAS_PAD = 4
BIAS_LANES = 768

f32 = jnp.float32
bf16 = jnp.bfloat16


def _sigmoid(x):
  return 0.5 * jnp.tanh(0.5 * x) + 0.5


def _softplus(x):
  return jnp.maximum(x, 0.0) + jnp.log1p(jnp.exp(-jnp.abs(x)))


def _gelu_tanh(x, scale=1.0):
  c = float(np.sqrt(2.0 / np.pi))
  return (0.5 * scale) * x * (1.0 + jnp.tanh(c * (x + 0.044715 * (x * x * x))))


def _rmsnorm(x, g):
  ms = jnp.mean(x * x, axis=-1, keepdims=True)
  return (x * lax.rsqrt(ms + EPS)) * g


def _dot(a, b):
  return jnp.dot(a, b, preferred_element_type=f32)


def _const_spec(shape):
  nd = len(shape)
  return pl.BlockSpec(shape, lambda *_: (0,) * nd, pipeline_mode=pl.Buffered(1))


def _swiglu_hidden(x, g_ref, wgu_ref, act_ref):
  h = _rmsnorm(x, g_ref[...]).astype(bf16)
  for c in range(D_FF // FF_CHUNK):
    lo = c * FF_CHUNK
    gate = _dot(h, wgu_ref[:, lo:lo + FF_CHUNK])
    up = _dot(h, wgu_ref[:, D_FF + lo:D_FF + lo + FF_CHUNK])
    act_ref[:, lo:lo + FF_CHUNK] = (gate * _sigmoid(gate) * up).astype(bf16)


def _swiglu(x, g_ref, wgu_ref, wdn_ref, act_ref):
  _swiglu_hidden(x, g_ref, wgu_ref, act_ref)
  return _dot(act_ref[...], wdn_ref[...])


def _swap_rows(x, perm_ref, m, n, pitch):
  for i in range(m):
    for c in range(D // LANES):
      perm_ref[c, i * pitch:i * pitch + n, :] = x[i * n:(i + 1) * n, c * LANES:(c + 1) * LANES]
  return jnp.concatenate(
      [jnp.concatenate([perm_ref[c, pl.ds(j, m, stride=pitch), :] for c in range(D // LANES)], axis=-1)
       for j in range(n)], axis=0)


def _ffn1_kernel(x_ref, g_ref, wgu_ref, wdn_ref, gm_ref, o_ref, hm_ref, act_ref):
  _swiglu_hidden(x_ref[...], g_ref, wgu_ref, act_ref)
  half = TM_FFN // 2
  for lo in (0, half):
    x1 = x_ref[lo:lo + half, :] + FFN_RES * _dot(act_ref[lo:lo + half, :], wdn_ref[...])
    o_ref[lo:lo + half, :] = x1
    hm_ref[lo:lo + half, :] = _rmsnorm(x1, gm_ref[...]).astype(bf16)


def _ffn1(x2d, g, wgu, wdn, g_mix):
  spec = pl.BlockSpec((TM_FFN, D), lambda i: (i, 0))
  return pl.pallas_call(
      _ffn1_kernel,
      grid=(T // TM_FFN,),
      in_specs=[
          spec,
          _const_spec((1, D)),
          _const_spec((D, 2 * D_FF)),
          _const_spec((D_FF, D)),
          _const_spec((1, D)),
      ],
      out_specs=[spec, spec],
      out_shape=[jax.ShapeDtypeStruct((T, D), f32), jax.ShapeDtypeStruct((T, D), bf16)],
      scratch_shapes=[pltpu.VMEM((TM_FFN, D_FF), bf16)],
      compiler_params=pltpu.CompilerParams(
          dimension_semantics=("parallel",), vmem_limit_bytes=VMEM_LIMIT),
      name="ffn1",
  )(x2d, g, wgu, wdn, g_mix)


def _lru_consts(bg_ref, lam_ref):
  kexp = (-0.5 * LRU_C * LOG2E) * _softplus(-lam_ref[...])
  return kexp, 0.5 * bg_ref[0:1, :], 0.5 * bg_ref[1:2, :]


def _lru_block(xh, wg, kexp, b_r, b_i):
  g = _dot(xh.astype(bf16), wg)
  tr = jnp.tanh(g[:, :LRU_GW] + b_r)
  ti = jnp.tanh(g[:, LRU_GW:] + b_i)
  a = jnp.exp2(tr * kexp + kexp)
  v = 1.0 - a * a
  sq = v * lax.rsqrt(jnp.maximum(v, TINY))
  return a, sq * ((ti + 1.0) * xh)


def _gate_weights(w):
  per = LRU_GW // LRU_BLOCK
  ncb = D // LRU_GW
  w = w.reshape(2, ncb, per, LRU_BLOCK, LRU_BLOCK)
  eye = jnp.eye(per, dtype=w.dtype)
  dense = w[:, :, :, :, None, :] * eye[None, None, :, None, :, None]
  dense = dense.reshape(2, ncb, LRU_GW, LRU_GW)
  return (0.5 * jnp.concatenate([dense[0], dense[1]], axis=-1)).astype(bf16)


def _inproj_a_kernel(h_ref, hn_ref, w_ref, cw_ref, cb_ref, wg_ref, bg_ref, lam_ref,
                     xc_ref, hf_ref, g1_ref, g2_ref, perm_ref, halo_ref, xprev_ref, car_ref):
  j = pl.program_id(0)
  last = pl.num_programs(0) - 1
  rows = B * TS_PROJ

  @pl.when(j == 0)
  def _():
    xprev_ref[...] = jnp.zeros_like(xprev_ref)
    car_ref[...] = jnp.zeros_like(car_ref)

  h_bm = h_ref[...].reshape(rows, D)
  g2_ref[...] = _sigmoid(_dot(h_bm, w_ref[:, 3 * D:])).astype(bf16).reshape(B, TS_PROJ, D)
  h_tm = _swap_rows(h_bm.astype(f32), perm_ref, B, TS_PROJ, PITCH_PROJ)
  hn = hn_ref[...].reshape(B * TS_HALO, D).astype(f32)
  for c in range(D // LANES):
    halo_ref[c] = hn[:, c * LANES:(c + 1) * LANES]
  hn0 = jnp.concatenate(
      [halo_ref[c, pl.ds(0, B, stride=TS_HALO), :] for c in range(D // LANES)], axis=-1)

  lhs_x = jnp.concatenate([h_tm, hn0], axis=0).astype(bf16)
  h_tm = lhs_x[:rows]
  kexp, b_r, b_i = _lru_consts(bg_ref, lam_ref)

  for c in range(D // LRU_GW):
    cs = slice(c * LRU_GW, (c + 1) * LRU_GW)
    wcol = lambda i: w_ref[:, i * D + c * LRU_GW:i * D + (c + 1) * LRU_GW]
    xr = _dot(lhs_x, wcol(0))
    gr, ga = _dot(h_tm, wcol(1)), _dot(h_tm, wcol(2))
    nxt = jnp.where(j == last, 0.0, xr[rows:]).reshape(1, B, LRU_GW)
    xr = xr[:rows].reshape(TS_PROJ, B, LRU_GW)
    seq = jnp.concatenate([xprev_ref[:, :, cs], xr, nxt], axis=0)
    xprev_ref[:, :, cs] = xr[TS_PROJ - CONV_LEFT:]
    xc = cb_ref[:, cs]
    for k in range(CONV_W):
      xc = xc + seq[k:k + TS_PROJ] * cw_ref[k:k + 1, cs]
    xc_ref[:, :, cs] = xc

    a, bx = _lru_block(xc.reshape(rows, LRU_GW), wg_ref[c], kexp[:, cs], b_r[:, cs], b_i[:, cs])
    hcur = car_ref[:, cs]
    for s in range(TS_PROJ):
      hcur = a[s * B:(s + 1) * B] * hcur + bx[s * B:(s + 1) * B]
      hf_ref[s, :, cs] = hcur.astype(bf16)
    car_ref[:, cs] = hcur

    g1 = _gelu_tanh(gr, scale=0.5) * _sigmoid(ga)
    g1_ref[:, :, cs] = g1.astype(bf16).reshape(TS_PROJ, B, LRU_GW)


def _inproj_a(hm, w_a, conv_w, conv_b, wg, bg, lam):
  nt = S // TS_PROJ
  bm_spec = pl.BlockSpec((B, TS_PROJ, D), lambda j: (0, j, 0))
  tm_spec = pl.BlockSpec((TS_PROJ, B, D), lambda j: (j, 0, 0))
  next_spec = pl.BlockSpec(
      (B, TS_HALO, D),
      lambda j: (0, jnp.minimum((j + 1) * (TS_PROJ // TS_HALO), S // TS_HALO - 1), 0))
  tm_out = lambda dt: jax.ShapeDtypeStruct((S, B, D), dt)
  return pl.pallas_call(
      _inproj_a_kernel,
      grid=(nt,),
      in_specs=[
          bm_spec,
          next_spec,
          _const_spec((D, 4 * D)),
          _const_spec((CONV_W, D)),
          _const_spec((1, D)),
          _const_spec((D // LRU_GW, LRU_GW, 2 * LRU_GW)),
          _const_spec((2, D)),
          _const_spec((1, D)),
      ],
      out_specs=[tm_spec, tm_spec, tm_spec, bm_spec],
      out_shape=[tm_out(f32), tm_out(bf16), tm_out(bf16), jax.ShapeDtypeStruct((B, S, D), bf16)],
      scratch_shapes=[pltpu.VMEM((D // LANES, B * PITCH_PROJ, LANES), f32),
                      pltpu.VMEM((D // LANES, B * TS_HALO, LANES), f32),
                      pltpu.VMEM((CONV_LEFT, B, D), f32),
                      pltpu.VMEM((B, D), f32)],
      compiler_params=pltpu.CompilerParams(
          dimension_semantics=("arbitrary",), vmem_limit_bytes=VMEM_LIMIT),
      name="inproj_a",
  )(hm, hm, w_a, conv_w, conv_b, wg, bg, lam)


def _inproj_b_kernel(h_ref, xc_ref, hf_ref, w_ref, qn_ref, kn_ref, ones_ref,
                     wg_ref, bg_ref, lam_ref, q_ref, k_ref, v_ref, hs_ref, car_ref):
  rows = B * TS_B

  @pl.when(pl.program_id(0) == 0)
  def _():
    car_ref[...] = jnp.zeros_like(car_ref)

  h = h_ref[...].reshape(rows, D)
  kexp, b_r, b_i = _lru_consts(bg_ref, lam_ref)
  bm = lambda a: a.astype(bf16).reshape(B, TS_B, MXU_N)
  head_ms = lambda t: _dot((t * t).astype(bf16), ones_ref[...]) * (1.0 / HEAD_DIM)

  for c in range(D // MXU_N):
    cs = slice(c * MXU_N, (c + 1) * MXU_N)
    wcol = lambda i: w_ref[:, i * D + c * MXU_N:i * D + (c + 1) * MXU_N]
    q, k, v = _dot(h, wcol(0)), _dot(h, wcol(1)), _dot(h, wcol(2))
    v_ref[:, :, cs] = bm(v)
    q = (q * lax.rsqrt(head_ms(q) + EPS)) * qn_ref[:, cs] * (HEAD_DIM ** -0.5 * LOG2E)
    q_ref[:, :, cs] = bm(q)
    k_ref[:, :, cs] = bm((k * lax.rsqrt(head_ms(k) + EPS)) * kn_ref[:, cs])

    xc = xc_ref[:, :, cs].reshape(rows, LRU_GW)
    a, bx = _lru_block(xc, wg_ref[c], kexp[:, cs], b_r[:, cs], b_i[:, cs])
    hcur = car_ref[:, cs]
    for s in reversed(range(TS_B)):
      hcur = a[s * B:(s + 1) * B] * hcur + bx[s * B:(s + 1) * B]
      hs_ref[s, :, cs] = (hf_ref[s, :, cs].astype(f32) + hcur).astype(bf16)
    car_ref[:, cs] = hcur


def _inproj_b(hm, xc, hf, w_b, qn, kn, ones_bd, wg, bg, lam):
  nt = S // TS_B
  bm_spec = pl.BlockSpec((B, TS_B, D), lambda i: (0, nt - 1 - i, 0))
  tm_spec = pl.BlockSpec((TS_B, B, D), lambda i: (nt - 1 - i, 0, 0))
  bm_out = jax.ShapeDtypeStruct((B, S, D), bf16)
  return pl.pallas_call(
      _inproj_b_kernel,
      grid=(nt,),
      in_specs=[
          bm_spec, tm_spec, tm_spec,
          _const_spec((D, 3 * D)),
          _const_spec((1, D)),
          _const_spec((1, D)),
          _const_spec((MXU_N, MXU_N)),
          _const_spec((D // LRU_GW, LRU_GW, 2 * LRU_GW)),
          _const_spec((2, D)),
          _const_spec((1, D)),
      ],
      out_specs=[bm_spec, bm_spec, bm_spec, tm_spec],
      out_shape=[bm_out, bm_out, bm_out, jax.ShapeDtypeStruct((S, B, D), bf16)],
      scratch_shapes=[pltpu.VMEM((B, D), f32)],
      compiler_params=pltpu.CompilerParams(
          dimension_semantics=("arbitrary",), vmem_limit_bytes=VMEM_LIMIT),
      name="inproj_b",
  )(hm, xc, hf, w_b, qn, kn, ones_bd, wg, bg, lam)


def _key_row0(rg):
  return int(np.clip(QR * rg - WIN_ROWS // 2, 0, ROWS - KR))


def _key_col0(n):
  return int(np.clip(QC * n - WIN_COLS // 2, 0, GRID_W - KC))


_KIND_ROWGROUPS = (0, 1, N_RG - 1)


def _table_kind(rg):
  return 0 if rg == 0 else (2 if rg == N_RG - 1 else 1)


def _natt_kernel(q_ref, k_ref, v_ref, g2_ref, cb_ref, ok_ref, o_ref, k8_ref, v8_ref, tbl_ref):
  @pl.when(pl.program_id(1) == 0)
  def _():
    for kind, rg in enumerate(_KIND_ROWGROUPS):
      for n in range(N_CB):
        for h in range(HG):
          wide = cb_ref[0, h, n]
          for r in range(QR):
            slot0 = _key_row0(rg) - (QR * rg + r) + WIN_ROWS - 1 + BIAS_PAD
            strip = pltpu.roll(wide, (BIAS_LANES - slot0 * KC) % BIAS_LANES, axis=1)[:, :NKEY]
            row0 = h * NQ + r * QC
            tbl_ref[kind, n, row0:row0 + QC, :] = jnp.where(
                ok_ref[kind, r:r + 1, :] != 0, strip, NEG)

  half = BF16_ROWS // 2
  for src, dst in ((k_ref, k8_ref), (v_ref, v8_ref)):
    for c0 in range(0, S - BF16_ROWS, KV_CHUNK):
      n = min(KV_CHUNK, S - BF16_ROWS - c0)
      wide = src[c0:c0 + n + BF16_ROWS, :].astype(f32)
      dst[c0:c0 + n, :] = wide[half:half + n].astype(bf16)

  def key_block(ref0, ref8, rows):
    parts = []
    for r in rows:
      parts.append(ref0[r:r + KC, :] if r % BF16_ROWS == 0 else ref8[r - half:r - half + KC, :])
    return jnp.concatenate(parts, axis=0)

  lane_head = lax.broadcasted_iota(jnp.int32, (NQ, HW), 1) // HEAD_DIM
  for rg in range(N_RG):
    k0 = _key_row0(rg)
    for n in range(N_CB):
      c0 = _key_col0(n)
      q_rows = [(QR * rg + rl) * GRID_W + QC * n for rl in range(QR)]
      k_rows = [(k0 + i) * GRID_W + c0 for i in range(KR)]
      qs = jnp.concatenate([q_ref[r:r + QC, :] for r in q_rows], axis=0)
      xq = jnp.concatenate(
          [jnp.where(lane_head == h, qs, jnp.zeros_like(qs)) for h in range(HG)], axis=0)
      kb = key_block(k_ref, k8_ref, k_rows)
      vb = key_block(v_ref, v8_ref, k_rows)
      s = lax.dot_general(xq, kb, (((1,), (1,)), ((), ())), preferred_element_type=f32)
      s = s + tbl_ref[_table_kind(rg), n]
      p = jnp.exp2(s - jnp.max(s, axis=1, keepdims=True))
      inv = 1.0 / jnp.sum(p, axis=1, keepdims=True)
      o = _dot(p.astype(bf16), vb) * inv
      acc = jnp.where(lane_head == 0, o[0:NQ], 0.0)
      for h in range(1, HG):
        acc = jnp.where(lane_head == h, o[h * NQ:(h + 1) * NQ], acc)
      for rl, r in enumerate(q_rows):
        o_ref[r:r + QC, :] = g2_ref[r:r + QC, :].astype(f32) * acc[rl * QC:(rl + 1) * QC]


def _natt(q, k, v, g2, col_bias, row_ok):
  ng = N_HEADS // HG
  spec = pl.BlockSpec((None, S, HW), lambda g, b: (b, 0, g))
  return pl.pallas_call(
      _natt_kernel,
      grid=(ng, B),
      in_specs=[
          spec, spec, spec, spec,
          pl.BlockSpec((1, HG, N_CB, QC, BIAS_LANES), lambda g, b: (g, 0, 0, 0, 0)),
          pl.BlockSpec((len(_KIND_ROWGROUPS), QR, NKEY), lambda g, b: (0, 0, 0)),
      ],
      out_specs=spec,
      out_shape=jax.ShapeDtypeStruct((B, S, D), f32),
      scratch_shapes=[pltpu.VMEM((S, HW), bf16), pltpu.VMEM((S, HW), bf16),
                      pltpu.VMEM((len(_KIND_ROWGROUPS), N_CB, HG * NQ, NKEY), f32)],
      compiler_params=pltpu.CompilerParams(
          dimension_semantics=("arbitrary", "arbitrary"), vmem_limit_bytes=VMEM_LIMIT),
      name="natt",
  )(q, k, v, g2, col_bias, row_ok)


def _bias_columns(rpb):
  n_dr, n_dc = 2 * WIN_ROWS - 1, 2 * WIN_COLS - 1
  c = np.arange(KC)[:, None]
  ql = np.arange(QC)[None, :]
  col_hot, col_ok = [], []
  for n in range(N_CB):
    qc = QC * n + ql
    kc = _key_col0(n) + c
    cs = np.clip(qc - WIN_COLS // 2, 0, GRID_W - WIN_COLS)
    col_ok.append((kc >= cs) & (kc < cs + WIN_COLS))
    col_hot.append(np.eye(n_dc)[np.clip(kc - qc, -(WIN_COLS - 1), WIN_COLS - 1) + WIN_COLS - 1])
  col_ok = np.stack(col_ok).transpose(0, 2, 1)
  ng = N_HEADS // HG
  bias = jnp.einsum("ghab,ncqb->ghnqac", rpb.astype(f32).reshape(ng, HG, n_dr, n_dc),
                    jnp.asarray(np.stack(col_hot), f32), precision=lax.Precision.HIGHEST)
  bias = jnp.where(col_ok[None, None, :, :, None, :], LOG2E * bias, NEG)
  slots = BIAS_LANES // KC
  bias = jnp.pad(bias, ((0, 0),) * 4 + ((BIAS_PAD, slots - BIAS_PAD - n_dr), (0, 0)))
  return bias.reshape(ng, HG, N_CB, QC, BIAS_LANES)


def _row_window_mask():
  i = np.arange(KR)[:, None]
  rl = np.arange(QR)[None, :]
  out = []
  for rg in _KIND_ROWGROUPS:
    r = QR * rg + rl
    rs = np.clip(r - WIN_ROWS // 2, 0, ROWS - WIN_ROWS)
    krow = _key_row0(rg) + i
    ok = (krow >= rs) & (krow < rs + WIN_ROWS)
    out.append(np.repeat(ok.T[:, :, None], KC, axis=2).reshape(QR, NKEY))
  return jnp.asarray(np.stack(out), jnp.int32)


def _ffn2_kernel(x1_ref, hs_ref, g1_ref, ya_ref, wo_ref, g_ref, wgu_ref, wdn_ref,
                 o_ref, act_ref, perm_ref):
  rows = B * TS_OUT
  y_lru = (g1_ref[...].astype(f32) * hs_ref[...].astype(f32)).reshape(rows, D)
  y_lru = _swap_rows(y_lru, perm_ref, TS_OUT, B, PITCH_OUT)
  y = y_lru + ya_ref[...].reshape(rows, D)
  x2 = x1_ref[...].reshape(rows, D) + _dot(y.astype(bf16), wo_ref[...])
  out = x2 + FFN_RES * _swiglu(x2, g_ref, wgu_ref, wdn_ref, act_ref)
  o_ref[...] = out.reshape(B, TS_OUT, D)


def _ffn2(x1, hs, g1, ya, wo, g, wgu, wdn):
  bm_spec = pl.BlockSpec((B, TS_OUT, D), lambda j: (0, j, 0))
  tm_spec = pl.BlockSpec((TS_OUT, B, D), lambda j: (j, 0, 0))
  return pl.pallas_call(
      _ffn2_kernel,
      grid=(S // TS_OUT,),
      in_specs=[bm_spec, tm_spec, tm_spec, bm_spec,
                _const_spec((D, D)),
                _const_spec((1, D)),
                _const_spec((D, 2 * D_FF)),
                _const_spec((D_FF, D))],
      out_specs=bm_spec,
      out_shape=jax.ShapeDtypeStruct((B, S, D), f32),
      scratch_shapes=[pltpu.VMEM((B * TS_OUT, D_FF), bf16),
                      pltpu.VMEM((D // LANES, TS_OUT * PITCH_OUT, LANES), f32)],
      compiler_params=pltpu.CompilerParams(
          dimension_semantics=("parallel",), vmem_limit_bytes=VMEM_LIMIT),
      name="ffn2",
  )(x1, hs, g1, ya, wo, g, wgu, wdn)


def kernel(x, norm_ffn1, w_ffn1_gu, w_ffn1_down, norm_mix, w_in, conv_w, conv_b,
           lru_w_gates, lru_b_gates, lru_lambda, q_norm, k_norm, rel_pos_bias, w_out,
           norm_ffn2, w_ffn2_gu, w_ffn2_down):
  assert x.shape == (B, S, D) and norm_ffn1.shape[0] == 1
  l = 0
  row = lambda a: a.reshape(1, D).astype(f32)
  ones_bd = jnp.asarray(
      np.kron(np.eye(MXU_N // HEAD_DIM), np.ones((HEAD_DIM, HEAD_DIM))), dtype=bf16)
  wi = w_in[l].astype(bf16)
  w_a = jnp.concatenate([wi[:, :2 * D], wi[:, 5 * D:]], axis=1)
  w_b = wi[:, 2 * D:5 * D]

  x1, hm = _ffn1(x.reshape(T, D), row(norm_ffn1[l]), w_ffn1_gu[l].astype(bf16),
                 w_ffn1_down[l].astype(bf16), row(norm_mix[l]))
  x1, hm = x1.reshape(B, S, D), hm.reshape(B, S, D)

  gates = lambda d: (_gate_weights(lru_w_gates[l, d]), lru_b_gates[l, d].astype(f32),
                     row(lru_lambda[l, d]))
  xc, hf, g1, g2 = _inproj_a(hm, w_a, conv_w[l].astype(f32), row(conv_b[l]), *gates(0))
  q, k, v, hs = _inproj_b(hm, xc, hf, w_b,
                          row(jnp.tile(q_norm[l], N_HEADS)), row(jnp.tile(k_norm[l], N_HEADS)),
                          ones_bd, *gates(1))

  ya = _natt(q, k, v, g2, _bias_columns(rel_pos_bias[l]), _row_window_mask())

  return _ffn2(x1, hs, g1, ya, w_out[l].astype(bf16),
               row(norm_ffn2[l]), w_ffn2_gu[l].astype(bf16), w_ffn2_down[l].astype(bf16))
```

```python
import jax
import jax.numpy as jnp
import numpy as np
from jax import lax
from jax.experimental import pallas as pl
from jax.experimental.pallas import tpu as pltpu

D = 1024
B = 32
S = 2048
T = B * S
GRID_W = 64
ROWS = S // GRID_W
N_HEADS = 16
HEAD_DIM = 64
WIN_ROWS = 8
WIN_COLS = 16
LRU_BLOCK = 64
LRU_C = 8.0
CONV_W = 4
CONV_LEFT = 2
D_FF = 2816
FFN_RES = 0.5
EPS = 1e-6
NEG = -1e30
LOG2E = float(np.log2(np.e))
TINY = 1e-37

SUBLANES = 8
BF16_ROWS = 16
LANES = 128
MXU_N = 256
VMEM_LIMIT = 56 * 1024 * 1024

TM_FFN = 1024
TS_PROJ = 16
TS_B = 32
PITCH_PROJ = 24
TS_OUT = 16
PITCH_OUT = 40
FF_CHUNK = MXU_N
LRU_GW = MXU_N
KV_CHUNK = 256
HG = 4
HW = HG * HEAD_DIM
QR = 4
QC = 16
KR = 12
KC = 32
NQ = QR * QC
NKEY = KR * KC
N_CB = GRID_W // QC
N_RG = ROWS // QR
BIAS_LANES = 640

f32 = jnp.float32
bf16 = jnp.bfloat16


def _sigmoid(x):
  return 0.5 * jnp.tanh(0.5 * x) + 0.5


def _softplus(x):
  return jnp.maximum(x, 0.0) + jnp.log1p(jnp.exp(-jnp.abs(x)))


def _gelu_tanh(x, scale=1.0):
  c = float(np.sqrt(2.0 / np.pi))
  return (0.5 * scale) * x * (1.0 + jnp.tanh(c * (x + 0.044715 * (x * x * x))))


def _rmsnorm(x, g):
  ms = jnp.mean(x * x, axis=-1, keepdims=True)
  return (x * lax.rsqrt(ms + EPS)) * g


def _dot(a, b):
  return jnp.dot(a, b, preferred_element_type=f32)


def _const_spec(shape):
  nd = len(shape)
  return pl.BlockSpec(shape, lambda *_: (0,) * nd, pipeline_mode=pl.Buffered(1))


def _swiglu_hidden(x, g_ref, wgu_ref, act_ref):
  h = _rmsnorm(x, g_ref[...]).astype(bf16)
  for c in range(D_FF // FF_CHUNK):
    lo = c * FF_CHUNK
    gate = _dot(h, wgu_ref[:, lo:lo + FF_CHUNK])
    up = _dot(h, wgu_ref[:, D_FF + lo:D_FF + lo + FF_CHUNK])
    act_ref[:, lo:lo + FF_CHUNK] = (gate * _sigmoid(gate) * up).astype(bf16)


def _swiglu(x, g_ref, wgu_ref, wdn_ref, act_ref):
  _swiglu_hidden(x, g_ref, wgu_ref, act_ref)
  return _dot(act_ref[...], wdn_ref[...])


def _swap_rows(x, perm_ref, m, n, pitch):
  for i in range(m):
    for c in range(D // LANES):
      perm_ref[c, i * pitch:i * pitch + n, :] = x[i * n:(i + 1) * n, c * LANES:(c + 1) * LANES]
  return jnp.concatenate(
      [jnp.concatenate([perm_ref[c, pl.ds(j, m, stride=pitch), :] for c in range(D // LANES)], axis=-1)
       for j in range(n)], axis=0)


def _ffn1_kernel(x_ref, g_ref, wgu_ref, wdn_ref, gm_ref, o_ref, hm_ref, act_ref):
  _swiglu_hidden(x_ref[...], g_ref, wgu_ref, act_ref)
  half = TM_FFN // 2
  for lo in (0, half):
    x1 = x_ref[lo:lo + half, :] + FFN_RES * _dot(act_ref[lo:lo + half, :], wdn_ref[...])
    o_ref[lo:lo + half, :] = x1
    hm_ref[lo:lo + half, :] = _rmsnorm(x1, gm_ref[...]).astype(bf16)


def _ffn1(x2d, g, wgu, wdn, g_mix):
  spec = pl.BlockSpec((TM_FFN, D), lambda i: (i, 0))
  return pl.pallas_call(
      _ffn1_kernel,
      grid=(T // TM_FFN,),
      in_specs=[
          spec,
          _const_spec((1, D)),
          _const_spec((D, 2 * D_FF)),
          _const_spec((D_FF, D)),
          _const_spec((1, D)),
      ],
      out_specs=[spec, spec],
      out_shape=[jax.ShapeDtypeStruct((T, D), f32), jax.ShapeDtypeStruct((T, D), bf16)],
      scratch_shapes=[pltpu.VMEM((TM_FFN, D_FF), bf16)],
      compiler_params=pltpu.CompilerParams(
          dimension_semantics=("parallel",), vmem_limit_bytes=VMEM_LIMIT),
      name="ffn1",
  )(x2d, g, wgu, wdn, g_mix)


def _lru_consts(bg_ref, lam_ref):
  kexp = (-0.5 * LRU_C * LOG2E) * _softplus(-lam_ref[...])
  return kexp, 0.5 * bg_ref[0:1, :], 0.5 * bg_ref[1:2, :]


def _lru_block(xh, wg, kexp, b_r, b_i):
  g = _dot(xh.astype(bf16), wg)
  tr = jnp.tanh(g[:, :LRU_GW] + b_r)
  ti = jnp.tanh(g[:, LRU_GW:] + b_i)
  a = jnp.exp2(tr * kexp + kexp)
  v = 1.0 - a * a
  sq = v * lax.rsqrt(jnp.maximum(v, TINY))
  return a, sq * ((ti + 1.0) * xh)


def _gate_weights(w):
  per = LRU_GW // LRU_BLOCK
  ncb = D // LRU_GW
  w = w.reshape(2, ncb, per, LRU_BLOCK, LRU_BLOCK)
  eye = jnp.eye(per, dtype=w.dtype)
  dense = w[:, :, :, :, None, :] * eye[None, None, :, None, :, None]
  dense = dense.reshape(2, ncb, LRU_GW, LRU_GW)
  return (0.5 * jnp.concatenate([dense[0], dense[1]], axis=-1)).astype(bf16)


def _inproj_a_kernel(h_ref, hn_ref, w_ref, cw_ref, cb_ref, wg_ref, bg_ref, lam_ref,
                     xc_ref, hf_ref, g1_ref, g2_ref, perm_ref, halo_ref, xprev_ref, car_ref):
  j = pl.program_id(0)
  last = pl.num_programs(0) - 1
  rows = B * TS_PROJ

  @pl.when(j == 0)
  def _():
    xprev_ref[...] = jnp.zeros_like(xprev_ref)
    car_ref[...] = jnp.zeros_like(car_ref)

  h_bm = h_ref[...].reshape(rows, D)
  g2_ref[...] = _sigmoid(_dot(h_bm, w_ref[:, 3 * D:])).reshape(B, TS_PROJ, D)
  h_tm = _swap_rows(h_bm.astype(f32), perm_ref, B, TS_PROJ, PITCH_PROJ)
  hn = hn_ref[...].reshape(rows, D).astype(f32)
  for c in range(D // LANES):
    halo_ref[c] = hn[:, c * LANES:(c + 1) * LANES]
  hn0 = jnp.concatenate(
      [halo_ref[c, pl.ds(0, B, stride=TS_PROJ), :] for c in range(D // LANES)], axis=-1)

  lhs_x = jnp.concatenate([h_tm, hn0], axis=0).astype(bf16)
  h_tm = lhs_x[:rows]
  kexp, b_r, b_i = _lru_consts(bg_ref, lam_ref)

  for c in range(D // LRU_GW):
    cs = slice(c * LRU_GW, (c + 1) * LRU_GW)
    wcol = lambda i: w_ref[:, i * D + c * LRU_GW:i * D + (c + 1) * LRU_GW]
    xr = _dot(lhs_x, wcol(0))
    gr, ga = _dot(h_tm, wcol(1)), _dot(h_tm, wcol(2))
    nxt = jnp.where(j == last, 0.0, xr[rows:]).reshape(1, B, LRU_GW)
    xr = xr[:rows].reshape(TS_PROJ, B, LRU_GW)
    seq = jnp.concatenate([xprev_ref[:, :, cs], xr, nxt], axis=0)
    xprev_ref[:, :, cs] = xr[TS_PROJ - CONV_LEFT:]
    xc = cb_ref[:, cs]
    for k in range(CONV_W):
      xc = xc + seq[k:k + TS_PROJ] * cw_ref[k:k + 1, cs]
    xc_ref[:, :, cs] = xc

    a, bx = _lru_block(xc.reshape(rows, LRU_GW), wg_ref[c], kexp[:, cs], b_r[:, cs], b_i[:, cs])
    hcur = car_ref[:, cs]
    for s in range(TS_PROJ):
      hcur = a[s * B:(s + 1) * B] * hcur + bx[s * B:(s + 1) * B]
      hf_ref[s, :, cs] = hcur.astype(bf16)
    car_ref[:, cs] = hcur

    g1_ref[:, :, cs] = (_gelu_tanh(gr, scale=0.5) * _sigmoid(ga)).reshape(TS_PROJ, B, LRU_GW)


def _inproj_a(hm, w_a, conv_w, conv_b, wg, bg, lam):
  nt = S // TS_PROJ
  bm_spec = pl.BlockSpec((B, TS_PROJ, D), lambda j: (0, j, 0))
  tm_spec = pl.BlockSpec((TS_PROJ, B, D), lambda j: (j, 0, 0))
  next_spec = pl.BlockSpec((B, TS_PROJ, D), lambda j: (0, jnp.minimum(j + 1, nt - 1), 0))
  tm_out = jax.ShapeDtypeStruct((S, B, D), f32)
  return pl.pallas_call(
      _inproj_a_kernel,
      grid=(nt,),
      in_specs=[
          bm_spec,
          next_spec,
          _const_spec((D, 4 * D)),
          _const_spec((CONV_W, D)),
          _const_spec((1, D)),
          _const_spec((D // LRU_GW, LRU_GW, 2 * LRU_GW)),
          _const_spec((2, D)),
          _const_spec((1, D)),
      ],
      out_specs=[tm_spec, tm_spec, tm_spec, bm_spec],
      out_shape=[tm_out, jax.ShapeDtypeStruct((S, B, D), bf16), tm_out,
                 jax.ShapeDtypeStruct((B, S, D), f32)],
      scratch_shapes=[pltpu.VMEM((D // LANES, B * PITCH_PROJ, LANES), f32),
                      pltpu.VMEM((D // LANES, B * TS_PROJ, LANES), f32),
                      pltpu.VMEM((CONV_LEFT, B, D), f32),
                      pltpu.VMEM((B, D), f32)],
      compiler_params=pltpu.CompilerParams(
          dimension_semantics=("arbitrary",), vmem_limit_bytes=VMEM_LIMIT),
      name="inproj_a",
  )(hm, hm, w_a, conv_w, conv_b, wg, bg, lam)


def _inproj_b_kernel(h_ref, xc_ref, hf_ref, w_ref, qn_ref, kn_ref, ones_ref,
                     wg_ref, bg_ref, lam_ref, q_ref, k_ref, v_ref, hs_ref, car_ref):
  rows = B * TS_B

  @pl.when(pl.program_id(0) == 0)
  def _():
    car_ref[...] = jnp.zeros_like(car_ref)

  h = h_ref[...].reshape(rows, D)
  kexp, b_r, b_i = _lru_consts(bg_ref, lam_ref)
  bm = lambda a: a.astype(bf16).reshape(B, TS_B, MXU_N)
  head_ms = lambda t: _dot((t * t).astype(bf16), ones_ref[...]) * (1.0 / HEAD_DIM)

  for c in range(D // MXU_N):
    cs = slice(c * MXU_N, (c + 1) * MXU_N)
    wcol = lambda i: w_ref[:, i * D + c * MXU_N:i * D + (c + 1) * MXU_N]
    q, k, v = _dot(h, wcol(0)), _dot(h, wcol(1)), _dot(h, wcol(2))
    v_ref[:, :, cs] = bm(v)
    q = (q * lax.rsqrt(head_ms(q) + EPS)) * qn_ref[:, cs] * (HEAD_DIM ** -0.5 * LOG2E)
    q_ref[:, :, cs] = bm(q)
    k_ref[:, :, cs] = bm((k * lax.rsqrt(head_ms(k) + EPS)) * kn_ref[:, cs])

    xc = xc_ref[:, :, cs].reshape(rows, LRU_GW)
    a, bx = _lru_block(xc, wg_ref[c], kexp[:, cs], b_r[:, cs], b_i[:, cs])
    hcur = car_ref[:, cs]
    for s in reversed(range(TS_B)):
      hcur = a[s * B:(s + 1) * B] * hcur + bx[s * B:(s + 1) * B]
      hs_ref[s, :, cs] = (hf_ref[s, :, cs].astype(f32) + hcur).astype(bf16)
    car_ref[:, cs] = hcur


def _inproj_b(hm, xc, hf, w_b, qn, kn, ones_bd, wg, bg, lam):
  nt = S // TS_B
  bm_spec = pl.BlockSpec((B, TS_B, D), lambda i: (0, nt - 1 - i, 0))
  tm_spec = pl.BlockSpec((TS_B, B, D), lambda i: (nt - 1 - i, 0, 0))
  bm_out = jax.ShapeDtypeStruct((B, S, D), bf16)
  return pl.pallas_call(
      _inproj_b_kernel,
      grid=(nt,),
      in_specs=[
          bm_spec, tm_spec, tm_spec,
          _const_spec((D, 3 * D)),
          _const_spec((1, D)),
          _const_spec((1, D)),
          _const_spec((MXU_N, MXU_N)),
          _const_spec((D // LRU_GW, LRU_GW, 2 * LRU_GW)),
          _const_spec((2, D)),
          _const_spec((1, D)),
      ],
      out_specs=[bm_spec, bm_spec, bm_spec, tm_spec],
      out_shape=[bm_out, bm_out, bm_out, jax.ShapeDtypeStruct((S, B, D), bf16)],
      scratch_shapes=[pltpu.VMEM((B, D), f32)],
      compiler_params=pltpu.CompilerParams(
          dimension_semantics=("arbitrary",), vmem_limit_bytes=VMEM_LIMIT),
      name="inproj_b",
  )(hm, xc, hf, w_b, qn, kn, ones_bd, wg, bg, lam)


def _key_rows(rg):
  if rg == 0:
    return 0, WIN_ROWS
  if rg == N_RG - 1:
    return ROWS - WIN_ROWS, WIN_ROWS
  return QR * rg - WIN_ROWS // 2, KR


def _key_row0(rg):
  return _key_rows(rg)[0]


def _key_col0(n):
  return int(np.clip(QC * n - WIN_COLS // 2, 0, GRID_W - KC))


_KIND_ROWGROUPS = (0, 1, N_RG - 1)


def _table_kind(rg):
  return 0 if rg == 0 else (2 if rg == N_RG - 1 else 1)


def _natt_kernel(q_ref, k_ref, v_ref, g2_ref, cb_ref, ok_ref, o_ref, k8_ref, v8_ref, tbl_ref):
  @pl.when(pl.program_id(1) == 0)
  def _():
    for kind, rg in enumerate(_KIND_ROWGROUPS):
      for n in range(N_CB):
        for h in range(HG):
          wide = cb_ref[0, h, n]
          for r in range(QR):
            slot0 = _key_row0(rg) - (QR * rg + r) + WIN_ROWS - 1
            strip = pltpu.roll(wide, (BIAS_LANES - slot0 * KC) % BIAS_LANES, axis=1)[:, :NKEY]
            row0 = h * NQ + r * QC
            tbl_ref[kind, n, row0:row0 + QC, :] = jnp.where(
                ok_ref[kind, r:r + 1, :] != 0, strip, NEG)

  half = BF16_ROWS // 2
  for src, dst in ((k_ref, k8_ref), (v_ref, v8_ref)):
    for c0 in range(0, S - BF16_ROWS, KV_CHUNK):
      n = min(KV_CHUNK, S - BF16_ROWS - c0)
      wide = src[c0:c0 + n + BF16_ROWS, :].astype(f32)
      dst[c0:c0 + n, :] = wide[half:half + n].astype(bf16)

  def key_block(ref0, ref8, rows):
    parts = []
    for r in rows:
      parts.append(ref0[r:r + KC, :] if r % BF16_ROWS == 0 else ref8[r - half:r - half + KC, :])
    return jnp.concatenate(parts, axis=0)

  lane_head = lax.broadcasted_iota(jnp.int32, (NQ, HW), 1) // HEAD_DIM
  for rg in range(N_RG):
    k0, kr = _key_rows(rg)
    nkey = kr * KC
    for n in range(N_CB):
      c0 = _key_col0(n)
      q_rows = [(QR * rg + rl) * GRID_W + QC * n for rl in range(QR)]
      k_rows = [(k0 + i) * GRID_W + c0 for i in range(kr)]
      qs = jnp.concatenate([q_ref[r:r + QC, :] for r in q_rows], axis=0)
      xq = jnp.concatenate(
          [jnp.where(lane_head == h, qs, jnp.zeros_like(qs)) for h in range(HG)], axis=0)
      kb = key_block(k_ref, k8_ref, k_rows)
      vb = key_block(v_ref, v8_ref, k_rows)
      s = lax.dot_general(xq, kb, (((1,), (1,)), ((), ())), preferred_element_type=f32)
      s = s + tbl_ref[_table_kind(rg), n, :, 0:nkey]
      p = jnp.exp2(s - jnp.max(s, axis=1, keepdims=True))
      inv = 1.0 / jnp.sum(p, axis=1, keepdims=True)
      o = _dot(p.astype(bf16), vb) * inv
      acc = jnp.where(lane_head == 0, o[0:NQ], 0.0)
      for h in range(1, HG):
        acc = jnp.where(lane_head == h, o[h * NQ:(h + 1) * NQ], acc)
      for rl, r in enumerate(q_rows):
        o_ref[r:r + QC, :] = g2_ref[r:r + QC, :] * acc[rl * QC:(rl + 1) * QC]


def _natt(q, k, v, g2, col_bias, row_ok):
  ng = N_HEADS // HG
  spec = pl.BlockSpec((None, S, HW), lambda g, b: (b, 0, g))
  return pl.pallas_call(
      _natt_kernel,
      grid=(ng, B),
      in_specs=[
          spec, spec, spec, spec,
          pl.BlockSpec((1, HG, N_CB, QC, BIAS_LANES), lambda g, b: (g, 0, 0, 0, 0)),
          pl.BlockSpec((len(_KIND_ROWGROUPS), QR, NKEY), lambda g, b: (0, 0, 0)),
      ],
      out_specs=spec,
      out_shape=jax.ShapeDtypeStruct((B, S, D), f32),
      scratch_shapes=[pltpu.VMEM((S, HW), bf16), pltpu.VMEM((S, HW), bf16),
                      pltpu.VMEM((len(_KIND_ROWGROUPS), N_CB, HG * NQ, NKEY), f32)],
      compiler_params=pltpu.CompilerParams(
          dimension_semantics=("arbitrary", "arbitrary"), vmem_limit_bytes=VMEM_LIMIT),
      name="natt",
  )(q, k, v, g2, col_bias, row_ok)


def _bias_columns(rpb):
  n_dr, n_dc = 2 * WIN_ROWS - 1, 2 * WIN_COLS - 1
  c = np.arange(KC)[:, None]
  ql = np.arange(QC)[None, :]
  col_hot, col_ok = [], []
  for n in range(N_CB):
    qc = QC * n + ql
    kc = _key_col0(n) + c
    cs = np.clip(qc - WIN_COLS // 2, 0, GRID_W - WIN_COLS)
    col_ok.append((kc >= cs) & (kc < cs + WIN_COLS))
    col_hot.append(np.eye(n_dc)[np.clip(kc - qc, -(WIN_COLS - 1), WIN_COLS - 1) + WIN_COLS - 1])
  col_ok = np.stack(col_ok).transpose(0, 2, 1)
  ng = N_HEADS // HG
  bias = jnp.einsum("ghab,ncqb->ghnqac", rpb.astype(f32).reshape(ng, HG, n_dr, n_dc),
                    jnp.asarray(np.stack(col_hot), f32), precision=lax.Precision.HIGHEST)
  bias = jnp.where(col_ok[None, None, :, :, None, :], LOG2E * bias, NEG)
  slots = BIAS_LANES // KC
  bias = jnp.pad(bias, ((0, 0),) * 4 + ((0, slots - n_dr), (0, 0)))
  return bias.reshape(ng, HG, N_CB, QC, BIAS_LANES)


def _row_window_mask():
  i = np.arange(KR)[:, None]
  rl = np.arange(QR)[None, :]
  out = []
  for rg in _KIND_ROWGROUPS:
    r = QR * rg + rl
    rs = np.clip(r - WIN_ROWS // 2, 0, ROWS - WIN_ROWS)
    krow = _key_row0(rg) + i
    ok = (krow >= rs) & (krow < rs + WIN_ROWS)
    out.append(np.repeat(ok.T[:, :, None], KC, axis=2).reshape(QR, NKEY))
  return jnp.asarray(np.stack(out), jnp.int32)


def _ffn2_kernel(x1_ref, hs_ref, g1_ref, ya_ref, wo_ref, g_ref, wgu_ref, wdn_ref,
                 o_ref, act_ref, perm_ref):
  rows = B * TS_OUT
  y_lru = (g1_ref[...] * hs_ref[...].astype(f32)).reshape(rows, D)
  y_lru = _swap_rows(y_lru, perm_ref, TS_OUT, B, PITCH_OUT)
  y = y_lru + ya_ref[...].reshape(rows, D)
  x2 = x1_ref[...].reshape(rows, D) + _dot(y.astype(bf16), wo_ref[...])
  out = x2 + FFN_RES * _swiglu(x2, g_ref, wgu_ref, wdn_ref, act_ref)
  o_ref[...] = out.reshape(B, TS_OUT, D)


def _ffn2(x1, hs, g1, ya, wo, g, wgu, wdn):
  bm_spec = pl.BlockSpec((B, TS_OUT, D), lambda j: (0, j, 0))
  tm_spec = pl.BlockSpec((TS_OUT, B, D), lambda j: (j, 0, 0))
  return pl.pallas_call(
      _ffn2_kernel,
      grid=(S // TS_OUT,),
      in_specs=[bm_spec, tm_spec, tm_spec, bm_spec,
                _const_spec((D, D)),
                _const_spec((1, D)),
                _const_spec((D, 2 * D_FF)),
                _const_spec((D_FF, D))],
      out_specs=bm_spec,
      out_shape=jax.ShapeDtypeStruct((B, S, D), f32),
      scratch_shapes=[pltpu.VMEM((B * TS_OUT, D_FF), bf16),
                      pltpu.VMEM((D // LANES, TS_OUT * PITCH_OUT, LANES), f32)],
      compiler_params=pltpu.CompilerParams(
          dimension_semantics=("parallel",), vmem_limit_bytes=VMEM_LIMIT),
      name="ffn2",
  )(x1, hs, g1, ya, wo, g, wgu, wdn)


def kernel(x, norm_ffn1, w_ffn1_gu, w_ffn1_down, norm_mix, w_in, conv_w, conv_b,
           lru_w_gates, lru_b_gates, lru_lambda, q_norm, k_norm, rel_pos_bias, w_out,
           norm_ffn2, w_ffn2_gu, w_ffn2_down):
  assert x.shape == (B, S, D) and norm_ffn1.shape[0] == 1
  l = 0
  row = lambda a: a.reshape(1, D).astype(f32)
  ones_bd = jnp.asarray(
      np.kron(np.eye(MXU_N // HEAD_DIM), np.ones((HEAD_DIM, HEAD_DIM))), dtype=bf16)
  wi = w_in[l].astype(bf16)
  w_a = jnp.concatenate([wi[:, :2 * D], wi[:, 5 * D:]], axis=1)
  w_b = wi[:, 2 * D:5 * D]

  x1, hm = _ffn1(x.reshape(T, D), row(norm_ffn1[l]), w_ffn1_gu[l].astype(bf16),
                 w_ffn1_down[l].astype(bf16), row(norm_mix[l]))
  x1, hm = x1.reshape(B, S, D), hm.reshape(B, S, D)

  gates = lambda d: (_gate_weights(lru_w_gates[l, d]), lru_b_gates[l, d].astype(f32),
                     row(lru_lambda[l, d]))
  xc, hf, g1, g2 = _inproj_a(hm, w_a, conv_w[l].astype(f32), row(conv_b[l]), *gates(0))
  q, k, v, hs = _inproj_b(hm, xc, hf, w_b,
                          row(jnp.tile(q_norm[l], N_HEADS)), row(jnp.tile(k_norm[l], N_HEADS)),
                          ones_bd, *gates(1))

  ya = _natt(q, k, v, g2, _bias_columns(rel_pos_bias[l]), _row_window_mask())

  return _ffn2(x1, hs, g1, ya, w_out[l].astype(bf16),
               row(norm_ffn2[l]), w_ffn2_gu[l].astype(bf16), w_ffn2_down[l].astype(bf16))
```

```python
import jax
import jax.numpy as jnp
import numpy as np
from jax import lax
from jax.experimental import pallas as pl
from jax.experimental.pallas import tpu as pltpu

D = 1024
B = 32
S = 2048
T = B * S
GRID_W = 64
ROWS = S // GRID_W
N_HEADS = 16
HEAD_DIM = 64
WIN_ROWS = 8
WIN_COLS = 16
LRU_BLOCK = 64
LRU_C = 8.0
CONV_W = 4
CONV_LEFT = 2
D_FF = 2816
FFN_RES = 0.5
EPS = 1e-6
NEG = -1e30
LOG2E = float(np.log2(np.e))
TINY = 1e-37

SUBLANES = 8
BF16_ROWS = 16
LANES = 128
MXU_N = 256
VMEM_LIMIT = 56 * 1024 * 1024

TM_FFN = 1024
TS_PROJ = 16
TS_B = 32
PITCH_PROJ = 24
TS_OUT = 16
PITCH_OUT = 40
FF_CHUNK = MXU_N
LRU_GW = MXU_N
KV_CHUNK = 256
HG = 4
HW = HG * HEAD_DIM
QR = 4
QC = 16
KR = 12
KC = 32
NQ = QR * QC
NKEY = KR * KC
N_CB = GRID_W // QC
N_RG = ROWS // QR
BIAS_LANES = 640

f32 = jnp.float32
bf16 = jnp.bfloat16


def _sigmoid(x):
  return 0.5 * jnp.tanh(0.5 * x) + 0.5


def _softplus(x):
  return jnp.maximum(x, 0.0) + jnp.log1p(jnp.exp(-jnp.abs(x)))


def _gelu_tanh(x, scale=1.0):
  c = float(np.sqrt(2.0 / np.pi))
  return (0.5 * scale) * x * (1.0 + jnp.tanh(c * (x + 0.044715 * (x * x * x))))


def _rmsnorm(x, g):
  ms = jnp.mean(x * x, axis=-1, keepdims=True)
  return (x * lax.rsqrt(ms + EPS)) * g


def _dot(a, b):
  return jnp.dot(a, b, preferred_element_type=f32)


def _const_spec(shape):
  nd = len(shape)
  return pl.BlockSpec(shape, lambda *_: (0,) * nd, pipeline_mode=pl.Buffered(1))


def _swiglu_hidden(x, g_ref, wgu_ref, act_ref):
  h = _rmsnorm(x, g_ref[...]).astype(bf16)
  for c in range(D_FF // FF_CHUNK):
    lo = c * FF_CHUNK
    gate = _dot(h, wgu_ref[:, lo:lo + FF_CHUNK])
    up = _dot(h, wgu_ref[:, D_FF + lo:D_FF + lo + FF_CHUNK])
    act_ref[:, lo:lo + FF_CHUNK] = (gate * _sigmoid(gate) * up).astype(bf16)


def _swiglu(x, g_ref, wgu_ref, wdn_ref, act_ref):
  _swiglu_hidden(x, g_ref, wgu_ref, act_ref)
  return _dot(act_ref[...], wdn_ref[...])


def _swap_rows(x, perm_ref, m, n, pitch):
  for i in range(m):
    for c in range(D // LANES):
      perm_ref[c, i * pitch:i * pitch + n, :] = x[i * n:(i + 1) * n, c * LANES:(c + 1) * LANES]
  return jnp.concatenate(
      [jnp.concatenate([perm_ref[c, pl.ds(j, m, stride=pitch), :] for c in range(D // LANES)], axis=-1)
       for j in range(n)], axis=0)


def _ffn1_kernel(x_ref, g_ref, wgu_ref, wdn_ref, gm_ref, o_ref, hm_ref, act_ref):
  _swiglu_hidden(x_ref[...], g_ref, wgu_ref, act_ref)
  half = TM_FFN // 2
  for lo in (0, half):
    x1 = x_ref[lo:lo + half, :] + FFN_RES * _dot(act_ref[lo:lo + half, :], wdn_ref[...])
    o_ref[lo:lo + half, :] = x1
    hm_ref[lo:lo + half, :] = _rmsnorm(x1, gm_ref[...]).astype(bf16)


def _ffn1(x2d, g, wgu, wdn, g_mix):
  spec = pl.BlockSpec((TM_FFN, D), lambda i: (i, 0))
  return pl.pallas_call(
      _ffn1_kernel,
      grid=(T // TM_FFN,),
      in_specs=[
          spec,
          _const_spec((1, D)),
          _const_spec((D, 2 * D_FF)),
          _const_spec((D_FF, D)),
          _const_spec((1, D)),
      ],
      out_specs=[spec, spec],
      out_shape=[jax.ShapeDtypeStruct((T, D), f32), jax.ShapeDtypeStruct((T, D), bf16)],
      scratch_shapes=[pltpu.VMEM((TM_FFN, D_FF), bf16)],
      compiler_params=pltpu.CompilerParams(
          dimension_semantics=("parallel",), vmem_limit_bytes=VMEM_LIMIT),
      name="ffn1",
  )(x2d, g, wgu, wdn, g_mix)


def _lru_consts(bg_ref, lam_ref):
  kexp = (-0.5 * LRU_C * LOG2E) * _softplus(-lam_ref[...])
  return kexp, 0.5 * bg_ref[0:1, :], 0.5 * bg_ref[1:2, :]


def _lru_block(xh, wg, kexp, b_r, b_i):
  g = _dot(xh.astype(bf16), wg)
  tr = jnp.tanh(g[:, :LRU_GW] + b_r)
  ti = jnp.tanh(g[:, LRU_GW:] + b_i)
  a = jnp.exp2(kexp * (tr + 1.0))
  v = 1.0 - a * a
  sq = v * lax.rsqrt(jnp.maximum(v, TINY))
  return a, sq * ((ti + 1.0) * xh)


def _gate_weights(w):
  per = LRU_GW // LRU_BLOCK
  ncb = D // LRU_GW
  w = w.reshape(2, ncb, per, LRU_BLOCK, LRU_BLOCK)
  eye = jnp.eye(per, dtype=w.dtype)
  dense = w[:, :, :, :, None, :] * eye[None, None, :, None, :, None]
  dense = dense.reshape(2, ncb, LRU_GW, LRU_GW)
  return (0.5 * jnp.concatenate([dense[0], dense[1]], axis=-1)).astype(bf16)


def _inproj_a_kernel(h_ref, hn_ref, w_ref, cw_ref, cb_ref, wg_ref, bg_ref, lam_ref,
                     xc_ref, hf_ref, g1_ref, g2_ref, perm_ref, halo_ref, xprev_ref, car_ref):
  j = pl.program_id(0)
  last = pl.num_programs(0) - 1
  rows = B * TS_PROJ

  @pl.when(j == 0)
  def _():
    xprev_ref[...] = jnp.zeros_like(xprev_ref)
    car_ref[...] = jnp.zeros_like(car_ref)

  h_bm = h_ref[...].reshape(rows, D)
  g2_ref[...] = _sigmoid(_dot(h_bm, w_ref[:, 3 * D:])).reshape(B, TS_PROJ, D)
  h_tm = _swap_rows(h_bm.astype(f32), perm_ref, B, TS_PROJ, PITCH_PROJ)
  hn = hn_ref[...].reshape(rows, D).astype(f32)
  for c in range(D // LANES):
    halo_ref[c] = hn[:, c * LANES:(c + 1) * LANES]
  hn0 = jnp.concatenate(
      [halo_ref[c, pl.ds(0, B, stride=TS_PROJ), :] for c in range(D // LANES)], axis=-1)

  lhs_x = jnp.concatenate([h_tm, hn0], axis=0).astype(bf16)
  h_tm = lhs_x[:rows]
  kexp, b_r, b_i = _lru_consts(bg_ref, lam_ref)

  for c in range(D // LRU_GW):
    cs = slice(c * LRU_GW, (c + 1) * LRU_GW)
    wcol = lambda i: w_ref[:, i * D + c * LRU_GW:i * D + (c + 1) * LRU_GW]
    xr = _dot(lhs_x, wcol(0))
    gr, ga = _dot(h_tm, wcol(1)), _dot(h_tm, wcol(2))
    nxt = jnp.where(j == last, 0.0, xr[rows:]).reshape(1, B, LRU_GW)
    xr = xr[:rows].reshape(TS_PROJ, B, LRU_GW)
    seq = jnp.concatenate([xprev_ref[:, :, cs], xr, nxt], axis=0)
    xprev_ref[:, :, cs] = xr[TS_PROJ - CONV_LEFT:]
    xc = cb_ref[:, cs]
    for k in range(CONV_W):
      xc = xc + seq[k:k + TS_PROJ] * cw_ref[k:k + 1, cs]
    xc_ref[:, :, cs] = xc

    a, bx = _lru_block(xc.reshape(rows, LRU_GW), wg_ref[c], kexp[:, cs], b_r[:, cs], b_i[:, cs])
    hcur = car_ref[:, cs]
    for s in range(TS_PROJ):
      hcur = a[s * B:(s + 1) * B] * hcur + bx[s * B:(s + 1) * B]
      hf_ref[s, :, cs] = hcur.astype(bf16)
    car_ref[:, cs] = hcur

    g1_ref[:, :, cs] = (_gelu_tanh(gr, scale=0.5) * _sigmoid(ga)).reshape(TS_PROJ, B, LRU_GW)


def _inproj_a(hm, w_a, conv_w, conv_b, wg, bg, lam):
  nt = S // TS_PROJ
  bm_spec = pl.BlockSpec((B, TS_PROJ, D), lambda j: (0, j, 0))
  tm_spec = pl.BlockSpec((TS_PROJ, B, D), lambda j: (j, 0, 0))
  next_spec = pl.BlockSpec((B, TS_PROJ, D), lambda j: (0, jnp.minimum(j + 1, nt - 1), 0))
  tm_out = jax.ShapeDtypeStruct((S, B, D), f32)
  return pl.pallas_call(
      _inproj_a_kernel,
      grid=(nt,),
      in_specs=[
          bm_spec,
          next_spec,
          _const_spec((D, 4 * D)),
          _const_spec((CONV_W, D)),
          _const_spec((1, D)),
          _const_spec((D // LRU_GW, LRU_GW, 2 * LRU_GW)),
          _const_spec((2, D)),
          _const_spec((1, D)),
      ],
      out_specs=[tm_spec, tm_spec, tm_spec, bm_spec],
      out_shape=[tm_out, jax.ShapeDtypeStruct((S, B, D), bf16), tm_out,
                 jax.ShapeDtypeStruct((B, S, D), f32)],
      scratch_shapes=[pltpu.VMEM((D // LANES, B * PITCH_PROJ, LANES), f32),
                      pltpu.VMEM((D // LANES, B * TS_PROJ, LANES), f32),
                      pltpu.VMEM((CONV_LEFT, B, D), f32),
                      pltpu.VMEM((B, D), f32)],
      compiler_params=pltpu.CompilerParams(
          dimension_semantics=("arbitrary",), vmem_limit_bytes=VMEM_LIMIT),
      name="inproj_a",
  )(hm, hm, w_a, conv_w, conv_b, wg, bg, lam)


def _inproj_b_kernel(h_ref, xc_ref, hf_ref, w_ref, qn_ref, kn_ref, ones_ref,
                     wg_ref, bg_ref, lam_ref, q_ref, k_ref, v_ref, hs_ref, car_ref):
  rows = B * TS_B

  @pl.when(pl.program_id(0) == 0)
  def _():
    car_ref[...] = jnp.zeros_like(car_ref)

  h = h_ref[...].reshape(rows, D)
  kexp, b_r, b_i = _lru_consts(bg_ref, lam_ref)
  bm = lambda a: a.astype(bf16).reshape(B, TS_B, MXU_N)
  head_ms = lambda t: _dot((t * t).astype(bf16), ones_ref[...]) * (1.0 / HEAD_DIM)

  for c in range(D // MXU_N):
    cs = slice(c * MXU_N, (c + 1) * MXU_N)
    wcol = lambda i: w_ref[:, i * D + c * MXU_N:i * D + (c + 1) * MXU_N]
    q, k, v = _dot(h, wcol(0)), _dot(h, wcol(1)), _dot(h, wcol(2))
    v_ref[:, :, cs] = bm(v)
    q = (q * lax.rsqrt(head_ms(q) + EPS)) * qn_ref[:, cs] * (HEAD_DIM ** -0.5 * LOG2E)
    q_ref[:, :, cs] = bm(q)
    k_ref[:, :, cs] = bm((k * lax.rsqrt(head_ms(k) + EPS)) * kn_ref[:, cs])

    xc = xc_ref[:, :, cs].reshape(rows, LRU_GW)
    a, bx = _lru_block(xc, wg_ref[c], kexp[:, cs], b_r[:, cs], b_i[:, cs])
    hcur = car_ref[:, cs]
    for s in reversed(range(TS_B)):
      hcur = a[s * B:(s + 1) * B] * hcur + bx[s * B:(s + 1) * B]
      hs_ref[s, :, cs] = (hf_ref[s, :, cs].astype(f32) + hcur).astype(bf16)
    car_ref[:, cs] = hcur


def _inproj_b(hm, xc, hf, w_b, qn, kn, ones_bd, wg, bg, lam):
  nt = S // TS_B
  bm_spec = pl.BlockSpec((B, TS_B, D), lambda i: (0, nt - 1 - i, 0))
  tm_spec = pl.BlockSpec((TS_B, B, D), lambda i: (nt - 1 - i, 0, 0))
  bm_out = jax.ShapeDtypeStruct((B, S, D), bf16)
  return pl.pallas_call(
      _inproj_b_kernel,
      grid=(nt,),
      in_specs=[
          bm_spec, tm_spec, tm_spec,
          _const_spec((D, 3 * D)),
          _const_spec((1, D)),
          _const_spec((1, D)),
          _const_spec((MXU_N, MXU_N)),
          _const_spec((D // LRU_GW, LRU_GW, 2 * LRU_GW)),
          _const_spec((2, D)),
          _const_spec((1, D)),
      ],
      out_specs=[bm_spec, bm_spec, bm_spec, tm_spec],
      out_shape=[bm_out, bm_out, bm_out, jax.ShapeDtypeStruct((S, B, D), bf16)],
      scratch_shapes=[pltpu.VMEM((B, D), f32)],
      compiler_params=pltpu.CompilerParams(
          dimension_semantics=("arbitrary",), vmem_limit_bytes=VMEM_LIMIT),
      name="inproj_b",
  )(hm, xc, hf, w_b, qn, kn, ones_bd, wg, bg, lam)


def _key_rows(rg):
  if rg == 0:
    return 0, WIN_ROWS
  if rg == N_RG - 1:
    return ROWS - WIN_ROWS, WIN_ROWS
  return QR * rg - WIN_ROWS // 2, KR


def _key_row0(rg):
  return _key_rows(rg)[0]


def _key_col0(n):
  return int(np.clip(QC * n - WIN_COLS // 2, 0, GRID_W - KC))


_KIND_ROWGROUPS = (0, 1, N_RG - 1)


def _table_kind(rg):
  return 0 if rg == 0 else (2 if rg == N_RG - 1 else 1)


def _natt_kernel(q_ref, k_ref, v_ref, g2_ref, cb_ref, ok_ref, o_ref, k8_ref, v8_ref, tbl_ref):
  @pl.when(pl.program_id(1) == 0)
  def _():
    for kind, rg in enumerate(_KIND_ROWGROUPS):
      for n in range(N_CB):
        for h in range(HG):
          wide = cb_ref[0, h, n]
          for r in range(QR):
            slot0 = _key_row0(rg) - (QR * rg + r) + WIN_ROWS - 1
            strip = pltpu.roll(wide, (BIAS_LANES - slot0 * KC) % BIAS_LANES, axis=1)[:, :NKEY]
            row0 = h * NQ + r * QC
            tbl_ref[kind, n, row0:row0 + QC, :] = jnp.where(
                ok_ref[kind, r:r + 1, :] != 0, strip, NEG)

  half = BF16_ROWS // 2
  for src, dst in ((k_ref, k8_ref), (v_ref, v8_ref)):
    for c0 in range(0, S - BF16_ROWS, KV_CHUNK):
      n = min(KV_CHUNK, S - BF16_ROWS - c0)
      wide = src[c0:c0 + n + BF16_ROWS, :].astype(f32)
      dst[c0:c0 + n, :] = wide[half:half + n].astype(bf16)

  def key_block(ref0, ref8, rows):
    parts = []
    for r in rows:
      parts.append(ref0[r:r + KC, :] if r % BF16_ROWS == 0 else ref8[r - half:r - half + KC, :])
    return jnp.concatenate(parts, axis=0)

  lane_head = lax.broadcasted_iota(jnp.int32, (NQ, HW), 1) // HEAD_DIM
  for rg in range(N_RG):
    k0, kr = _key_rows(rg)
    nkey = kr * KC
    for n in range(N_CB):
      c0 = _key_col0(n)
      q_rows = [(QR * rg + rl) * GRID_W + QC * n for rl in range(QR)]
      k_rows = [(k0 + i) * GRID_W + c0 for i in range(kr)]
      qs = jnp.concatenate([q_ref[r:r + QC, :] for r in q_rows], axis=0)
      xq = jnp.concatenate(
          [jnp.where(lane_head == h, qs, jnp.zeros_like(qs)) for h in range(HG)], axis=0)
      kb = key_block(k_ref, k8_ref, k_rows)
      vb = key_block(v_ref, v8_ref, k_rows)
      s = lax.dot_general(xq, kb, (((1,), (1,)), ((), ())), preferred_element_type=f32)
      s = s + tbl_ref[_table_kind(rg), n, :, 0:nkey]
      p = jnp.exp2(s - jnp.max(s, axis=1, keepdims=True))
      inv = 1.0 / jnp.sum(p, axis=1, keepdims=True)
      o = _dot(p.astype(bf16), vb) * inv
      acc = jnp.where(lane_head == 0, o[0:NQ], 0.0)
      for h in range(1, HG):
        acc = jnp.where(lane_head == h, o[h * NQ:(h + 1) * NQ], acc)
      for rl, r in enumerate(q_rows):
        o_ref[r:r + QC, :] = g2_ref[r:r + QC, :] * acc[rl * QC:(rl + 1) * QC]


def _natt(q, k, v, g2, col_bias, row_ok):
  ng = N_HEADS // HG
  spec = pl.BlockSpec((None, S, HW), lambda g, b: (b, 0, g))
  return pl.pallas_call(
      _natt_kernel,
      grid=(ng, B),
      in_specs=[
          spec, spec, spec, spec,
          pl.BlockSpec((1, HG, N_CB, QC, BIAS_LANES), lambda g, b: (g, 0, 0, 0, 0)),
          pl.BlockSpec((len(_KIND_ROWGROUPS), QR, NKEY), lambda g, b: (0, 0, 0)),
      ],
      out_specs=spec,
      out_shape=jax.ShapeDtypeStruct((B, S, D), f32),
      scratch_shapes=[pltpu.VMEM((S, HW), bf16), pltpu.VMEM((S, HW), bf16),
                      pltpu.VMEM((len(_KIND_ROWGROUPS), N_CB, HG * NQ, NKEY), f32)],
      compiler_params=pltpu.CompilerParams(
          dimension_semantics=("arbitrary", "arbitrary"), vmem_limit_bytes=VMEM_LIMIT),
      name="natt",
  )(q, k, v, g2, col_bias, row_ok)


def _bias_columns(rpb):
  n_dr, n_dc = 2 * WIN_ROWS - 1, 2 * WIN_COLS - 1
  c = np.arange(KC)[:, None]
  ql = np.arange(QC)[None, :]
  col_hot, col_ok = [], []
  for n in range(N_CB):
    qc = QC * n + ql
    kc = _key_col0(n) + c
    cs = np.clip(qc - WIN_COLS // 2, 0, GRID_W - WIN_COLS)
    col_ok.append((kc >= cs) & (kc < cs + WIN_COLS))
    col_hot.append(np.eye(n_dc)[np.clip(kc - qc, -(WIN_COLS - 1), WIN_COLS - 1) + WIN_COLS - 1])
  col_ok = np.stack(col_ok).transpose(0, 2, 1)
  ng = N_HEADS // HG
  bias = jnp.einsum("ghab,ncqb->ghnqac", rpb.astype(f32).reshape(ng, HG, n_dr, n_dc),
                    jnp.asarray(np.stack(col_hot), f32), precision=lax.Precision.HIGHEST)
  bias = jnp.where(col_ok[None, None, :, :, None, :], LOG2E * bias, NEG)
  slots = BIAS_LANES // KC
  bias = jnp.pad(bias, ((0, 0),) * 4 + ((0, slots - n_dr), (0, 0)))
  return bias.reshape(ng, HG, N_CB, QC, BIAS_LANES)


def _row_window_mask():
  i = np.arange(KR)[:, None]
  rl = np.arange(QR)[None, :]
  out = []
  for rg in _KIND_ROWGROUPS:
    r = QR * rg + rl
    rs = np.clip(r - WIN_ROWS // 2, 0, ROWS - WIN_ROWS)
    krow = _key_row0(rg) + i
    ok = (krow >= rs) & (krow < rs + WIN_ROWS)
    out.append(np.repeat(ok.T[:, :, None], KC, axis=2).reshape(QR, NKEY))
  return jnp.asarray(np.stack(out), jnp.int32)


def _ffn2_kernel(x1_ref, hs_ref, g1_ref, ya_ref, wo_ref, g_ref, wgu_ref, wdn_ref,
                 o_ref, act_ref, perm_ref):
  rows = B * TS_OUT
  y_lru = (g1_ref[...] * hs_ref[...].astype(f32)).reshape(rows, D)
  y_lru = _swap_rows(y_lru, perm_ref, TS_OUT, B, PITCH_OUT)
  y = y_lru + ya_ref[...].reshape(rows, D)
  x2 = x1_ref[...].reshape(rows, D) + _dot(y.astype(bf16), wo_ref[...])
  out = x2 + FFN_RES * _swiglu(x2, g_ref, wgu_ref, wdn_ref, act_ref)
  o_ref[...] = out.reshape(B, TS_OUT, D)


def _ffn2(x1, hs, g1, ya, wo, g, wgu, wdn):
  bm_spec = pl.BlockSpec((B, TS_OUT, D), lambda j: (0, j, 0))
  tm_spec = pl.BlockSpec((TS_OUT, B, D), lambda j: (j, 0, 0))
  return pl.pallas_call(
      _ffn2_kernel,
      grid=(S // TS_OUT,),
      in_specs=[bm_spec, tm_spec, tm_spec, bm_spec,
                _const_spec((D, D)),
                _const_spec((1, D)),
                _const_spec((D, 2 * D_FF)),
                _const_spec((D_FF, D))],
      out_specs=bm_spec,
      out_shape=jax.ShapeDtypeStruct((B, S, D), f32),
      scratch_shapes=[pltpu.VMEM((B * TS_OUT, D_FF), bf16),
                      pltpu.VMEM((D // LANES, TS_OUT * PITCH_OUT, LANES), f32)],
      compiler_params=pltpu.CompilerParams(
          dimension_semantics=("parallel",), vmem_limit_bytes=VMEM_LIMIT),
      name="ffn2",
  )(x1, hs, g1, ya, wo, g, wgu, wdn)


def kernel(x, norm_ffn1, w_ffn1_gu, w_ffn1_down, norm_mix, w_in, conv_w, conv_b,
           lru_w_gates, lru_b_gates, lru_lambda, q_norm, k_norm, rel_pos_bias, w_out,
           norm_ffn2, w_ffn2_gu, w_ffn2_down):
  assert x.shape == (B, S, D) and norm_ffn1.shape[0] == 1
  l = 0
  row = lambda a: a.reshape(1, D).astype(f32)
  ones_bd = jnp.asarray(
      np.kron(np.eye(MXU_N // HEAD_DIM), np.ones((HEAD_DIM, HEAD_DIM))), dtype=bf16)
  wi = w_in[l].astype(bf16)
  w_a = jnp.concatenate([wi[:, :2 * D], wi[:, 5 * D:]], axis=1)
  w_b = wi[:, 2 * D:5 * D]

  x1, hm = _ffn1(x.reshape(T, D), row(norm_ffn1[l]), w_ffn1_gu[l].astype(bf16),
                 w_ffn1_down[l].astype(bf16), row(norm_mix[l]))
  x1, hm = x1.reshape(B, S, D), hm.reshape(B, S, D)

  gates = lambda d: (_gate_weights(lru_w_gates[l, d]), lru_b_gates[l, d].astype(f32),
                     row(lru_lambda[l, d]))
  xc, hf, g1, g2 = _inproj_a(hm, w_a, conv_w[l].astype(f32), row(conv_b[l]), *gates(0))
  q, k, v, hs = _inproj_b(hm, xc, hf, w_b,
                          row(jnp.tile(q_norm[l], N_HEADS)), row(jnp.tile(k_norm[l], N_HEADS)),
                          ones_bd, *gates(1))

  ya = _natt(q, k, v, g2, _bias_columns(rel_pos_bias[l]), _row_window_mask())

  return _ffn2(x1, hs, g1, ya, w_out[l].astype(bf16),
               row(norm_ffn2[l]), w_ffn2_gu[l].astype(bf16), w_ffn2_down[l].astype(bf16))
```

```python
import jax
import jax.numpy as jnp
import numpy as np
from jax import lax
from jax.experimental import pallas as pl
from jax.experimental.pallas import tpu as pltpu

D = 1024
B = 32
S = 2048
T = B * S
GRID_W = 64
ROWS = S // GRID_W
N_HEADS = 16
HEAD_DIM = 64
WIN_ROWS = 8
WIN_COLS = 16
LRU_BLOCK = 64
LRU_C = 8.0
CONV_W = 4
CONV_LEFT = 2
D_FF = 2816
FFN_RES = 0.5
EPS = 1e-6
NEG = -1e30
LOG2E = float(np.log2(np.e))
TINY = 1e-37

SUBLANES = 8
BF16_ROWS = 16
LANES = 128
MXU_N = 256
VMEM_LIMIT = 56 * 1024 * 1024

TM_FFN = 1024
TS_PROJ = 16
TS_B = 32
PITCH_PROJ = 24
TS_OUT = 16
PITCH_OUT = 40
FF_CHUNK = MXU_N
LRU_GW = MXU_N
KV_CHUNK = 256
HG = 4
HW = HG * HEAD_DIM
QR = 4
QC = 16
KR = 12
KC = 32
NQ = QR * QC
NKEY = KR * KC
N_CB = GRID_W // QC
N_RG = ROWS // QR
BIAS_LANES = 640

f32 = jnp.float32
bf16 = jnp.bfloat16


def _sigmoid(x):
  return 0.5 * jnp.tanh(0.5 * x) + 0.5


def _softplus(x):
  return jnp.maximum(x, 0.0) + jnp.log1p(jnp.exp(-jnp.abs(x)))


def _gelu_tanh(x, scale=1.0):
  c = float(np.sqrt(2.0 / np.pi))
  return (0.5 * scale) * x * (1.0 + jnp.tanh(c * (x + 0.044715 * (x * x * x))))


def _rmsnorm(x, g):
  ms = jnp.mean(x * x, axis=-1, keepdims=True)
  return (x * lax.rsqrt(ms + EPS)) * g


def _dot(a, b):
  return jnp.dot(a, b, preferred_element_type=f32)


def _const_spec(shape):
  nd = len(shape)
  return pl.BlockSpec(shape, lambda *_: (0,) * nd, pipeline_mode=pl.Buffered(1))


def _swiglu_hidden(x, g_ref, wgu_ref, act_ref):
  h = _rmsnorm(x, g_ref[...]).astype(bf16)
  for c in range(D_FF // FF_CHUNK):
    lo = c * FF_CHUNK
    gate = _dot(h, wgu_ref[:, lo:lo + FF_CHUNK])
    up = _dot(h, wgu_ref[:, D_FF + lo:D_FF + lo + FF_CHUNK])
    act_ref[:, lo:lo + FF_CHUNK] = (gate * _sigmoid(gate) * up).astype(bf16)


def _swiglu(x, g_ref, wgu_ref, wdn_ref, act_ref):
  _swiglu_hidden(x, g_ref, wgu_ref, act_ref)
  return _dot(act_ref[...], wdn_ref[...])


def _swap_rows(x, perm_ref, m, n, pitch):
  for i in range(m):
    for c in range(D // LANES):
      perm_ref[c, i * pitch:i * pitch + n, :] = x[i * n:(i + 1) * n, c * LANES:(c + 1) * LANES]
  return jnp.concatenate(
      [jnp.concatenate([perm_ref[c, pl.ds(j, m, stride=pitch), :] for c in range(D // LANES)], axis=-1)
       for j in range(n)], axis=0)


def _ffn1_kernel(x_ref, g_ref, wgu_ref, wdn_ref, gm_ref, o_ref, hm_ref, act_ref):
  _swiglu_hidden(x_ref[...], g_ref, wgu_ref, act_ref)
  half = TM_FFN // 2
  for lo in (0, half):
    x1 = x_ref[lo:lo + half, :] + FFN_RES * _dot(act_ref[lo:lo + half, :], wdn_ref[...])
    o_ref[lo:lo + half, :] = x1
    hm_ref[lo:lo + half, :] = _rmsnorm(x1, gm_ref[...]).astype(bf16)


def _ffn1(x2d, g, wgu, wdn, g_mix):
  spec = pl.BlockSpec((TM_FFN, D), lambda i: (i, 0))
  return pl.pallas_call(
      _ffn1_kernel,
      grid=(T // TM_FFN,),
      in_specs=[
          spec,
          _const_spec((1, D)),
          _const_spec((D, 2 * D_FF)),
          _const_spec((D_FF, D)),
          _const_spec((1, D)),
      ],
      out_specs=[spec, spec],
      out_shape=[jax.ShapeDtypeStruct((T, D), f32), jax.ShapeDtypeStruct((T, D), bf16)],
      scratch_shapes=[pltpu.VMEM((TM_FFN, D_FF), bf16)],
      compiler_params=pltpu.CompilerParams(
          dimension_semantics=("parallel",), vmem_limit_bytes=VMEM_LIMIT),
      name="ffn1",
  )(x2d, g, wgu, wdn, g_mix)


def _lru_consts(bg_ref, lam_ref):
  kexp = (-0.5 * LRU_C * LOG2E) * _softplus(-lam_ref[...])
  return kexp, 0.5 * bg_ref[0:1, :], 0.5 * bg_ref[1:2, :]


def _lru_coeffs(g, xh, kexp, b_r, b_i):
  tr = jnp.tanh(g[:, :LRU_GW] + b_r)
  ti = jnp.tanh(g[:, LRU_GW:] + b_i)
  a = jnp.exp2(kexp * (tr + 1.0))
  v = 1.0 - a * a
  sq = v * lax.rsqrt(jnp.maximum(v, TINY))
  return a, sq * ((ti + 1.0) * xh)


def _gate_weights(w):
  per = LRU_GW // LRU_BLOCK
  ncb = D // LRU_GW
  w = w.reshape(2, ncb, per, LRU_BLOCK, LRU_BLOCK)
  eye = jnp.eye(per, dtype=w.dtype)
  dense = w[:, :, :, :, None, :] * eye[None, None, :, None, :, None]
  dense = dense.reshape(2, ncb, LRU_GW, LRU_GW)
  return (0.5 * jnp.concatenate([dense[0], dense[1]], axis=-1)).astype(bf16)


def _inproj_a_kernel(h_ref, hn_ref, w_ref, cw_ref, cb_ref, wg_ref, bg_ref, lam_ref,
                     xc_ref, hf_ref, g1_ref, g2_ref, perm_ref, halo_ref, xprev_ref, car_ref):
  j = pl.program_id(0)
  last = pl.num_programs(0) - 1
  rows = B * TS_PROJ

  @pl.when(j == 0)
  def _():
    xprev_ref[...] = jnp.zeros_like(xprev_ref)
    car_ref[...] = jnp.zeros_like(car_ref)

  h_bm = h_ref[...].reshape(rows, D)
  g2_ref[...] = _sigmoid(_dot(h_bm, w_ref[:, 3 * D:])).reshape(B, TS_PROJ, D)
  h_tm = _swap_rows(h_bm.astype(f32), perm_ref, B, TS_PROJ, PITCH_PROJ)
  hn = hn_ref[...].reshape(rows, D).astype(f32)
  for c in range(D // LANES):
    halo_ref[c] = hn[:, c * LANES:(c + 1) * LANES]
  hn0 = jnp.concatenate(
      [halo_ref[c, pl.ds(0, B, stride=TS_PROJ), :] for c in range(D // LANES)], axis=-1)

  lhs_x = jnp.concatenate([h_tm, hn0], axis=0).astype(bf16)
  h_tm = lhs_x[:rows]
  kexp, b_r, b_i = _lru_consts(bg_ref, lam_ref)

  for c in range(D // LRU_GW):
    cs = slice(c * LRU_GW, (c + 1) * LRU_GW)
    wcol = lambda i: w_ref[:, i * D + c * LRU_GW:i * D + (c + 1) * LRU_GW]
    xr = _dot(lhs_x, wcol(0))
    gr, ga = _dot(h_tm, wcol(1)), _dot(h_tm, wcol(2))
    nxt = jnp.where(j == last, 0.0, xr[rows:]).reshape(1, B, LRU_GW)
    xr = xr[:rows].reshape(TS_PROJ, B, LRU_GW)
    seq = jnp.concatenate([xprev_ref[:, :, cs], xr, nxt], axis=0)
    xprev_ref[:, :, cs] = xr[TS_PROJ - CONV_LEFT:]
    xc = cb_ref[:, cs]
    for k in range(CONV_W):
      xc = xc + seq[k:k + TS_PROJ] * cw_ref[k:k + 1, cs]
    xc_ref[:, :, cs] = xc

    xc = xc.reshape(rows, LRU_GW)
    g = _dot(xc.astype(bf16), wg_ref[c])
    g1_ref[:, :, cs] = (_gelu_tanh(gr, scale=0.5) * _sigmoid(ga)).reshape(TS_PROJ, B, LRU_GW)

    a, bx = _lru_coeffs(g, xc, kexp[:, cs], b_r[:, cs], b_i[:, cs])
    hcur = car_ref[:, cs]
    for s in range(TS_PROJ):
      hcur = a[s * B:(s + 1) * B] * hcur + bx[s * B:(s + 1) * B]
      hf_ref[s, :, cs] = hcur.astype(bf16)
    car_ref[:, cs] = hcur


def _inproj_a(hm, w_a, conv_w, conv_b, wg, bg, lam):
  nt = S // TS_PROJ
  bm_spec = pl.BlockSpec((B, TS_PROJ, D), lambda j: (0, j, 0))
  tm_spec = pl.BlockSpec((TS_PROJ, B, D), lambda j: (j, 0, 0))
  next_spec = pl.BlockSpec((B, TS_PROJ, D), lambda j: (0, jnp.minimum(j + 1, nt - 1), 0))
  tm_out = jax.ShapeDtypeStruct((S, B, D), f32)
  return pl.pallas_call(
      _inproj_a_kernel,
      grid=(nt,),
      in_specs=[
          bm_spec,
          next_spec,
          _const_spec((D, 4 * D)),
          _const_spec((CONV_W, D)),
          _const_spec((1, D)),
          _const_spec((D // LRU_GW, LRU_GW, 2 * LRU_GW)),
          _const_spec((2, D)),
          _const_spec((1, D)),
      ],
      out_specs=[tm_spec, tm_spec, tm_spec, bm_spec],
      out_shape=[tm_out, jax.ShapeDtypeStruct((S, B, D), bf16), tm_out,
                 jax.ShapeDtypeStruct((B, S, D), f32)],
      scratch_shapes=[pltpu.VMEM((D // LANES, B * PITCH_PROJ, LANES), f32),
                      pltpu.VMEM((D // LANES, B * TS_PROJ, LANES), f32),
                      pltpu.VMEM((CONV_LEFT, B, D), f32),
                      pltpu.VMEM((B, D), f32)],
      compiler_params=pltpu.CompilerParams(
          dimension_semantics=("arbitrary",), vmem_limit_bytes=VMEM_LIMIT),
      name="inproj_a",
  )(hm, hm, w_a, conv_w, conv_b, wg, bg, lam)


def _inproj_b_kernel(h_ref, xc_ref, hf_ref, w_ref, qn_ref, kn_ref, ones_ref,
                     wg_ref, bg_ref, lam_ref, q_ref, k_ref, v_ref, hs_ref, car_ref):
  rows = B * TS_B

  @pl.when(pl.program_id(0) == 0)
  def _():
    car_ref[...] = jnp.zeros_like(car_ref)

  h = h_ref[...].reshape(rows, D)
  kexp, b_r, b_i = _lru_consts(bg_ref, lam_ref)
  bm = lambda a: a.astype(bf16).reshape(B, TS_B, MXU_N)
  head_ms = lambda t: _dot((t * t).astype(bf16), ones_ref[...]) * (1.0 / HEAD_DIM)

  for c in range(D // MXU_N):
    cs = slice(c * MXU_N, (c + 1) * MXU_N)
    wcol = lambda i: w_ref[:, i * D + c * MXU_N:i * D + (c + 1) * MXU_N]
    xc = xc_ref[:, :, cs].reshape(rows, LRU_GW)
    g = _dot(xc.astype(bf16), wg_ref[c])
    q, k, v = _dot(h, wcol(0)), _dot(h, wcol(1)), _dot(h, wcol(2))
    a, bx = _lru_coeffs(g, xc, kexp[:, cs], b_r[:, cs], b_i[:, cs])
    hcur = car_ref[:, cs]
    for s in reversed(range(TS_B)):
      hcur = a[s * B:(s + 1) * B] * hcur + bx[s * B:(s + 1) * B]
      hs_ref[s, :, cs] = (hf_ref[s, :, cs].astype(f32) + hcur).astype(bf16)
    car_ref[:, cs] = hcur

    v_ref[:, :, cs] = bm(v)
    q = (q * lax.rsqrt(head_ms(q) + EPS)) * qn_ref[:, cs] * (HEAD_DIM ** -0.5 * LOG2E)
    q_ref[:, :, cs] = bm(q)
    k_ref[:, :, cs] = bm((k * lax.rsqrt(head_ms(k) + EPS)) * kn_ref[:, cs])


def _inproj_b(hm, xc, hf, w_b, qn, kn, ones_bd, wg, bg, lam):
  nt = S // TS_B
  bm_spec = pl.BlockSpec((B, TS_B, D), lambda i: (0, nt - 1 - i, 0))
  tm_spec = pl.BlockSpec((TS_B, B, D), lambda i: (nt - 1 - i, 0, 0))
  bm_out = jax.ShapeDtypeStruct((B, S, D), bf16)
  return pl.pallas_call(
      _inproj_b_kernel,
      grid=(nt,),
      in_specs=[
          bm_spec, tm_spec, tm_spec,
          _const_spec((D, 3 * D)),
          _const_spec((1, D)),
          _const_spec((1, D)),
          _const_spec((MXU_N, MXU_N)),
          _const_spec((D // LRU_GW, LRU_GW, 2 * LRU_GW)),
          _const_spec((2, D)),
          _const_spec((1, D)),
      ],
      out_specs=[bm_spec, bm_spec, bm_spec, tm_spec],
      out_shape=[bm_out, bm_out, bm_out, jax.ShapeDtypeStruct((S, B, D), bf16)],
      scratch_shapes=[pltpu.VMEM((B, D), f32)],
      compiler_params=pltpu.CompilerParams(
          dimension_semantics=("arbitrary",), vmem_limit_bytes=VMEM_LIMIT),
      name="inproj_b",
  )(hm, xc, hf, w_b, qn, kn, ones_bd, wg, bg, lam)


def _key_rows(rg):
  if rg == 0:
    return 0, WIN_ROWS
  if rg == N_RG - 1:
    return ROWS - WIN_ROWS, WIN_ROWS
  return QR * rg - WIN_ROWS // 2, KR


def _key_row0(rg):
  return _key_rows(rg)[0]


def _key_col0(n):
  return int(np.clip(QC * n - WIN_COLS // 2, 0, GRID_W - KC))


_KIND_ROWGROUPS = (0, 1, N_RG - 1)


def _table_kind(rg):
  return 0 if rg == 0 else (2 if rg == N_RG - 1 else 1)


def _natt_kernel(q_ref, k_ref, v_ref, g2_ref, cb_ref, ok_ref, o_ref, k8_ref, v8_ref, tbl_ref):
  @pl.when(pl.program_id(1) == 0)
  def _():
    for kind, rg in enumerate(_KIND_ROWGROUPS):
      for n in range(N_CB):
        for h in range(HG):
          wide = cb_ref[0, h, n]
          for r in range(QR):
            slot0 = _key_row0(rg) - (QR * rg + r) + WIN_ROWS - 1
            strip = pltpu.roll(wide, (BIAS_LANES - slot0 * KC) % BIAS_LANES, axis=1)[:, :NKEY]
            row0 = h * NQ + r * QC
            tbl_ref[kind, n, row0:row0 + QC, :] = jnp.where(
                ok_ref[kind, r:r + 1, :] != 0, strip, NEG)

  half = BF16_ROWS // 2
  for src, dst in ((k_ref, k8_ref), (v_ref, v8_ref)):
    for c0 in range(0, S - BF16_ROWS, KV_CHUNK):
      n = min(KV_CHUNK, S - BF16_ROWS - c0)
      wide = src[c0:c0 + n + BF16_ROWS, :].astype(f32)
      dst[c0:c0 + n, :] = wide[half:half + n].astype(bf16)

  def key_block(ref0, ref8, rows):
    parts = []
    for r in rows:
      parts.append(ref0[r:r + KC, :] if r % BF16_ROWS == 0 else ref8[r - half:r - half + KC, :])
    return jnp.concatenate(parts, axis=0)

  lane_head = lax.broadcasted_iota(jnp.int32, (NQ, HW), 1) // HEAD_DIM
  for rg in range(N_RG):
    k0, kr = _key_rows(rg)
    nkey = kr * KC
    for n in range(N_CB):
      c0 = _key_col0(n)
      q_rows = [(QR * rg + rl) * GRID_W + QC * n for rl in range(QR)]
      k_rows = [(k0 + i) * GRID_W + c0 for i in range(kr)]
      qs = jnp.concatenate([q_ref[r:r + QC, :] for r in q_rows], axis=0)
      xq = jnp.concatenate(
          [jnp.where(lane_head == h, qs, jnp.zeros_like(qs)) for h in range(HG)], axis=0)
      kb = key_block(k_ref, k8_ref, k_rows)
      vb = key_block(v_ref, v8_ref, k_rows)
      s = lax.dot_general(xq, kb, (((1,), (1,)), ((), ())), preferred_element_type=f32)
      s = s + tbl_ref[_table_kind(rg), n, :, 0:nkey]
      p = jnp.exp2(s - jnp.max(s, axis=1, keepdims=True))
      inv = 1.0 / jnp.sum(p, axis=1, keepdims=True)
      o = _dot(p.astype(bf16), vb) * inv
      acc = jnp.where(lane_head == 0, o[0:NQ], 0.0)
      for h in range(1, HG):
        acc = jnp.where(lane_head == h, o[h * NQ:(h + 1) * NQ], acc)
      for rl, r in enumerate(q_rows):
        o_ref[r:r + QC, :] = g2_ref[r:r + QC, :] * acc[rl * QC:(rl + 1) * QC]


def _natt(q, k, v, g2, col_bias, row_ok):
  ng = N_HEADS // HG
  spec = pl.BlockSpec((None, S, HW), lambda g, b: (b, 0, g))
  return pl.pallas_call(
      _natt_kernel,
      grid=(ng, B),
      in_specs=[
          spec, spec, spec, spec,
          pl.BlockSpec((1, HG, N_CB, QC, BIAS_LANES), lambda g, b: (g, 0, 0, 0, 0)),
          pl.BlockSpec((len(_KIND_ROWGROUPS), QR, NKEY), lambda g, b: (0, 0, 0)),
      ],
      out_specs=spec,
      out_shape=jax.ShapeDtypeStruct((B, S, D), f32),
      scratch_shapes=[pltpu.VMEM((S, HW), bf16), pltpu.VMEM((S, HW), bf16),
                      pltpu.VMEM((len(_KIND_ROWGROUPS), N_CB, HG * NQ, NKEY), f32)],
      compiler_params=pltpu.CompilerParams(
          dimension_semantics=("arbitrary", "arbitrary"), vmem_limit_bytes=VMEM_LIMIT),
      name="natt",
  )(q, k, v, g2, col_bias, row_ok)


def _bias_columns(rpb):
  n_dr, n_dc = 2 * WIN_ROWS - 1, 2 * WIN_COLS - 1
  c = np.arange(KC)[:, None]
  ql = np.arange(QC)[None, :]
  col_hot, col_ok = [], []
  for n in range(N_CB):
    qc = QC * n + ql
    kc = _key_col0(n) + c
    cs = np.clip(qc - WIN_COLS // 2, 0, GRID_W - WIN_COLS)
    col_ok.append((kc >= cs) & (kc < cs + WIN_COLS))
    col_hot.append(np.eye(n_dc)[np.clip(kc - qc, -(WIN_COLS - 1), WIN_COLS - 1) + WIN_COLS - 1])
  col_ok = np.stack(col_ok).transpose(0, 2, 1)
  ng = N_HEADS // HG
  bias = jnp.einsum("ghab,ncqb->ghnqac", rpb.astype(f32).reshape(ng, HG, n_dr, n_dc),
                    jnp.asarray(np.stack(col_hot), f32), precision=lax.Precision.HIGHEST)
  bias = jnp.where(col_ok[None, None, :, :, None, :], LOG2E * bias, NEG)
  slots = BIAS_LANES // KC
  bias = jnp.pad(bias, ((0, 0),) * 4 + ((0, slots - n_dr), (0, 0)))
  return bias.reshape(ng, HG, N_CB, QC, BIAS_LANES)


def _row_window_mask():
  i = np.arange(KR)[:, None]
  rl = np.arange(QR)[None, :]
  out = []
  for rg in _KIND_ROWGROUPS:
    r = QR * rg + rl
    rs = np.clip(r - WIN_ROWS // 2, 0, ROWS - WIN_ROWS)
    krow = _key_row0(rg) + i
    ok = (krow >= rs) & (krow < rs + WIN_ROWS)
    out.append(np.repeat(ok.T[:, :, None], KC, axis=2).reshape(QR, NKEY))
  return jnp.asarray(np.stack(out), jnp.int32)


def _ffn2_kernel(x1_ref, hs_ref, g1_ref, ya_ref, wo_ref, g_ref, wgu_ref, wdn_ref,
                 o_ref, act_ref, perm_ref):
  rows = B * TS_OUT
  y_lru = (g1_ref[...] * hs_ref[...].astype(f32)).reshape(rows, D)
  y_lru = _swap_rows(y_lru, perm_ref, TS_OUT, B, PITCH_OUT)
  y = y_lru + ya_ref[...].reshape(rows, D)
  x2 = x1_ref[...].reshape(rows, D) + _dot(y.astype(bf16), wo_ref[...])
  out = x2 + FFN_RES * _swiglu(x2, g_ref, wgu_ref, wdn_ref, act_ref)
  o_ref[...] = out.reshape(B, TS_OUT, D)


def _ffn2(x1, hs, g1, ya, wo, g, wgu, wdn):
  bm_spec = pl.BlockSpec((B, TS_OUT, D), lambda j: (0, j, 0))
  tm_spec = pl.BlockSpec((TS_OUT, B, D), lambda j: (j, 0, 0))
  return pl.pallas_call(
      _ffn2_kernel,
      grid=(S // TS_OUT,),
      in_specs=[bm_spec, tm_spec, tm_spec, bm_spec,
                _const_spec((D, D)),
                _const_spec((1, D)),
                _const_spec((D, 2 * D_FF)),
                _const_spec((D_FF, D))],
      out_specs=bm_spec,
      out_shape=jax.ShapeDtypeStruct((B, S, D), f32),
      scratch_shapes=[pltpu.VMEM((B * TS_OUT, D_FF), bf16),
                      pltpu.VMEM((D // LANES, TS_OUT * PITCH_OUT, LANES), f32)],
      compiler_params=pltpu.CompilerParams(
          dimension_semantics=("parallel",), vmem_limit_bytes=VMEM_LIMIT),
      name="ffn2",
  )(x1, hs, g1, ya, wo, g, wgu, wdn)


def kernel(x, norm_ffn1, w_ffn1_gu, w_ffn1_down, norm_mix, w_in, conv_w, conv_b,
           lru_w_gates, lru_b_gates, lru_lambda, q_norm, k_norm, rel_pos_bias, w_out,
           norm_ffn2, w_ffn2_gu, w_ffn2_down):
  assert x.shape == (B, S, D) and norm_ffn1.shape[0] == 1
  l = 0
  row = lambda a: a.reshape(1, D).astype(f32)
  ones_bd = jnp.asarray(
      np.kron(np.eye(MXU_N // HEAD_DIM), np.ones((HEAD_DIM, HEAD_DIM))), dtype=bf16)
  wi = w_in[l].astype(bf16)
  w_a = jnp.concatenate([wi[:, :2 * D], wi[:, 5 * D:]], axis=1)
  w_b = wi[:, 2 * D:5 * D]

  x1, hm = _ffn1(x.reshape(T, D), row(norm_ffn1[l]), w_ffn1_gu[l].astype(bf16),
                 w_ffn1_down[l].astype(bf16), row(norm_mix[l]))
  x1, hm = x1.reshape(B, S, D), hm.reshape(B, S, D)

  gates = lambda d: (_gate_weights(lru_w_gates[l, d]), lru_b_gates[l, d].astype(f32),
                     row(lru_lambda[l, d]))
  xc, hf, g1, g2 = _inproj_a(hm, w_a, conv_w[l].astype(f32), row(conv_b[l]), *gates(0))
  q, k, v, hs = _inproj_b(hm, xc, hf, w_b,
                          row(jnp.tile(q_norm[l], N_HEADS)), row(jnp.tile(k_norm[l], N_HEADS)),
                          ones_bd, *gates(1))

  ya = _natt(q, k, v, g2, _bias_columns(rel_pos_bias[l]), _row_window_mask())

  return _ffn2(x1, hs, g1, ya, w_out[l].astype(bf16),
               row(norm_ffn2[l]), w_ffn2_gu[l].astype(bf16), w_ffn2_down[l].astype(bf16))
```

```python
import jax
import jax.numpy as jnp
import numpy as np
from jax import lax
from jax.experimental import pallas as pl
from jax.experimental.pallas import tpu as pltpu

D = 1024
B = 32
S = 2048
T = B * S
GRID_W = 64
ROWS = S // GRID_W
N_HEADS = 16
HEAD_DIM = 64
WIN_ROWS = 8
WIN_COLS = 16
LRU_BLOCK = 64
LRU_C = 8.0
CONV_W = 4
CONV_LEFT = 2
D_FF = 2816
FFN_RES = 0.5
EPS = 1e-6
NEG = -1e30
LOG2E = float(np.log2(np.e))
TINY = 1e-37

SUBLANES = 8
BF16_ROWS = 16
LANES = 128
MXU_N = 256
VMEM_LIMIT = 56 * 1024 * 1024

TM_FFN = 1024
TS_PROJ = 16
TS_B = 32
PITCH_PROJ = 24
TS_OUT = 16
PITCH_OUT = 40
FF_CHUNK = MXU_N
LRU_GW = MXU_N
KV_CHUNK = 256
HG = 4
NB_ATT = 2
HW = HG * HEAD_DIM
QR = 4
QC = 16
KR = 12
KC = 32
NQ = QR * QC
NKEY = KR * KC
N_CB = GRID_W // QC
N_RG = ROWS // QR
BIAS_LANES = 640

f32 = jnp.float32
bf16 = jnp.bfloat16


def _sigmoid(x):
  return 0.5 * jnp.tanh(0.5 * x) + 0.5


def _softplus(x):
  return jnp.maximum(x, 0.0) + jnp.log1p(jnp.exp(-jnp.abs(x)))


def _gelu_tanh(x, scale=1.0):
  c = float(np.sqrt(2.0 / np.pi))
  return (0.5 * scale) * x * (1.0 + jnp.tanh(c * (x + 0.044715 * (x * x * x))))


def _rmsnorm(x, g):
  ms = jnp.mean(x * x, axis=-1, keepdims=True)
  return (x * lax.rsqrt(ms + EPS)) * g


def _dot(a, b):
  return jnp.dot(a, b, preferred_element_type=f32)


def _const_spec(shape):
  nd = len(shape)
  return pl.BlockSpec(shape, lambda *_: (0,) * nd, pipeline_mode=pl.Buffered(1))


def _swiglu_hidden(x, g_ref, wgu_ref, act_ref):
  h = _rmsnorm(x, g_ref[...]).astype(bf16)
  for c in range(D_FF // FF_CHUNK):
    lo = c * FF_CHUNK
    gate = _dot(h, wgu_ref[:, lo:lo + FF_CHUNK])
    up = _dot(h, wgu_ref[:, D_FF + lo:D_FF + lo + FF_CHUNK])
    act_ref[:, lo:lo + FF_CHUNK] = (gate * _sigmoid(gate) * up).astype(bf16)


def _swiglu(x, g_ref, wgu_ref, wdn_ref, act_ref):
  _swiglu_hidden(x, g_ref, wgu_ref, act_ref)
  return _dot(act_ref[...], wdn_ref[...])


def _swap_rows(x, perm_ref, m, n, pitch):
  for i in range(m):
    for c in range(D // LANES):
      perm_ref[c, i * pitch:i * pitch + n, :] = x[i * n:(i + 1) * n, c * LANES:(c + 1) * LANES]
  return jnp.concatenate(
      [jnp.concatenate([perm_ref[c, pl.ds(j, m, stride=pitch), :] for c in range(D // LANES)], axis=-1)
       for j in range(n)], axis=0)


def _ffn1_kernel(x_ref, g_ref, wgu_ref, wdn_ref, gm_ref, o_ref, hm_ref, act_ref):
  _swiglu_hidden(x_ref[...], g_ref, wgu_ref, act_ref)
  half = TM_FFN // 2
  for lo in (0, half):
    x1 = x_ref[lo:lo + half, :] + FFN_RES * _dot(act_ref[lo:lo + half, :], wdn_ref[...])
    o_ref[lo:lo + half, :] = x1
    hm_ref[lo:lo + half, :] = _rmsnorm(x1, gm_ref[...]).astype(bf16)


def _ffn1(x2d, g, wgu, wdn, g_mix):
  spec = pl.BlockSpec((TM_FFN, D), lambda i: (i, 0))
  return pl.pallas_call(
      _ffn1_kernel,
      grid=(T // TM_FFN,),
      in_specs=[
          spec,
          _const_spec((1, D)),
          _const_spec((D, 2 * D_FF)),
          _const_spec((D_FF, D)),
          _const_spec((1, D)),
      ],
      out_specs=[spec, spec],
      out_shape=[jax.ShapeDtypeStruct((T, D), f32), jax.ShapeDtypeStruct((T, D), bf16)],
      scratch_shapes=[pltpu.VMEM((TM_FFN, D_FF), bf16)],
      compiler_params=pltpu.CompilerParams(
          dimension_semantics=("parallel",), vmem_limit_bytes=VMEM_LIMIT),
      name="ffn1",
  )(x2d, g, wgu, wdn, g_mix)


def _lru_consts(bg_ref, lam_ref):
  kexp = (-0.5 * LRU_C * LOG2E) * _softplus(-lam_ref[...])
  return kexp, 0.5 * bg_ref[0:1, :], 0.5 * bg_ref[1:2, :]


def _lru_coeffs(g, xh, kexp, b_r, b_i):
  tr = jnp.tanh(g[:, :LRU_GW] + b_r)
  ti = jnp.tanh(g[:, LRU_GW:] + b_i)
  a = jnp.exp2(kexp * (tr + 1.0))
  v = 1.0 - a * a
  sq = v * lax.rsqrt(jnp.maximum(v, TINY))
  return a, sq * ((ti + 1.0) * xh)


def _gate_weights(w):
  per = LRU_GW // LRU_BLOCK
  ncb = D // LRU_GW
  w = w.reshape(2, ncb, per, LRU_BLOCK, LRU_BLOCK)
  eye = jnp.eye(per, dtype=w.dtype)
  dense = w[:, :, :, :, None, :] * eye[None, None, :, None, :, None]
  dense = dense.reshape(2, ncb, LRU_GW, LRU_GW)
  return (0.5 * jnp.concatenate([dense[0], dense[1]], axis=-1)).astype(bf16)


def _inproj_a_kernel(h_ref, hn_ref, w_ref, cw_ref, cb_ref, wg_ref, bg_ref, lam_ref,
                     xc_ref, hf_ref, g1_ref, g2_ref, perm_ref, halo_ref, xprev_ref, car_ref):
  j = pl.program_id(0)
  last = pl.num_programs(0) - 1
  rows = B * TS_PROJ

  @pl.when(j == 0)
  def _():
    xprev_ref[...] = jnp.zeros_like(xprev_ref)
    car_ref[...] = jnp.zeros_like(car_ref)

  h_bm = h_ref[...].reshape(rows, D)
  g2_ref[...] = _sigmoid(_dot(h_bm, w_ref[:, 3 * D:])).reshape(B, TS_PROJ, D)
  h_tm = _swap_rows(h_bm.astype(f32), perm_ref, B, TS_PROJ, PITCH_PROJ)
  hn = hn_ref[...].reshape(rows, D).astype(f32)
  for c in range(D // LANES):
    halo_ref[c] = hn[:, c * LANES:(c + 1) * LANES]
  hn0 = jnp.concatenate(
      [halo_ref[c, pl.ds(0, B, stride=TS_PROJ), :] for c in range(D // LANES)], axis=-1)

  lhs_x = jnp.concatenate([h_tm, hn0], axis=0).astype(bf16)
  h_tm = lhs_x[:rows]
  kexp, b_r, b_i = _lru_consts(bg_ref, lam_ref)

  for c in range(D // LRU_GW):
    cs = slice(c * LRU_GW, (c + 1) * LRU_GW)
    wcol = lambda i: w_ref[:, i * D + c * LRU_GW:i * D + (c + 1) * LRU_GW]
    xr = _dot(lhs_x, wcol(0))
    gr, ga = _dot(h_tm, wcol(1)), _dot(h_tm, wcol(2))
    nxt = jnp.where(j == last, 0.0, xr[rows:]).reshape(1, B, LRU_GW)
    xr = xr[:rows].reshape(TS_PROJ, B, LRU_GW)
    seq = jnp.concatenate([xprev_ref[:, :, cs], xr, nxt], axis=0)
    xprev_ref[:, :, cs] = xr[TS_PROJ - CONV_LEFT:]
    xc = cb_ref[:, cs]
    for k in range(CONV_W):
      xc = xc + seq[k:k + TS_PROJ] * cw_ref[k:k + 1, cs]
    xc_ref[:, :, cs] = xc

    xc = xc.reshape(rows, LRU_GW)
    g = _dot(xc.astype(bf16), wg_ref[c])
    g1_ref[:, :, cs] = (_gelu_tanh(gr, scale=0.5) * _sigmoid(ga)).reshape(TS_PROJ, B, LRU_GW)

    a, bx = _lru_coeffs(g, xc, kexp[:, cs], b_r[:, cs], b_i[:, cs])
    hcur = car_ref[:, cs]
    for s in range(TS_PROJ):
      hcur = a[s * B:(s + 1) * B] * hcur + bx[s * B:(s + 1) * B]
      hf_ref[s, :, cs] = hcur.astype(bf16)
    car_ref[:, cs] = hcur


def _inproj_a(hm, w_a, conv_w, conv_b, wg, bg, lam):
  nt = S // TS_PROJ
  bm_spec = pl.BlockSpec((B, TS_PROJ, D), lambda j: (0, j, 0))
  tm_spec = pl.BlockSpec((TS_PROJ, B, D), lambda j: (j, 0, 0))
  next_spec = pl.BlockSpec((B, TS_PROJ, D), lambda j: (0, jnp.minimum(j + 1, nt - 1), 0))
  tm_out = jax.ShapeDtypeStruct((S, B, D), f32)
  return pl.pallas_call(
      _inproj_a_kernel,
      grid=(nt,),
      in_specs=[
          bm_spec,
          next_spec,
          _const_spec((D, 4 * D)),
          _const_spec((CONV_W, D)),
          _const_spec((1, D)),
          _const_spec((D // LRU_GW, LRU_GW, 2 * LRU_GW)),
          _const_spec((2, D)),
          _const_spec((1, D)),
      ],
      out_specs=[tm_spec, tm_spec, tm_spec, bm_spec],
      out_shape=[tm_out, jax.ShapeDtypeStruct((S, B, D), bf16), tm_out,
                 jax.ShapeDtypeStruct((B, S, D), f32)],
      scratch_shapes=[pltpu.VMEM((D // LANES, B * PITCH_PROJ, LANES), f32),
                      pltpu.VMEM((D // LANES, B * TS_PROJ, LANES), f32),
                      pltpu.VMEM((CONV_LEFT, B, D), f32),
                      pltpu.VMEM((B, D), f32)],
      compiler_params=pltpu.CompilerParams(
          dimension_semantics=("arbitrary",), vmem_limit_bytes=VMEM_LIMIT),
      name="inproj_a",
  )(hm, hm, w_a, conv_w, conv_b, wg, bg, lam)


def _inproj_b_kernel(h_ref, xc_ref, hf_ref, w_ref, qn_ref, kn_ref, ones_ref,
                     wg_ref, bg_ref, lam_ref, q_ref, k_ref, v_ref, hs_ref, car_ref):
  rows = B * TS_B

  @pl.when(pl.program_id(0) == 0)
  def _():
    car_ref[...] = jnp.zeros_like(car_ref)

  h = h_ref[...].reshape(rows, D)
  kexp, b_r, b_i = _lru_consts(bg_ref, lam_ref)
  bm = lambda a: a.astype(bf16).reshape(B, TS_B, MXU_N)
  head_ms = lambda t: _dot((t * t).astype(bf16), ones_ref[...]) * (1.0 / HEAD_DIM)

  for c in range(D // MXU_N):
    cs = slice(c * MXU_N, (c + 1) * MXU_N)
    wcol = lambda i: w_ref[:, i * D + c * MXU_N:i * D + (c + 1) * MXU_N]
    xc = xc_ref[:, :, cs].reshape(rows, LRU_GW)
    g = _dot(xc.astype(bf16), wg_ref[c])
    q, k, v = _dot(h, wcol(0)), _dot(h, wcol(1)), _dot(h, wcol(2))
    a, bx = _lru_coeffs(g, xc, kexp[:, cs], b_r[:, cs], b_i[:, cs])
    hcur = car_ref[:, cs]
    for s in reversed(range(TS_B)):
      hcur = a[s * B:(s + 1) * B] * hcur + bx[s * B:(s + 1) * B]
      hs_ref[s, :, cs] = (hf_ref[s, :, cs].astype(f32) + hcur).astype(bf16)
    car_ref[:, cs] = hcur

    v_ref[:, :, cs] = bm(v)
    q = (q * lax.rsqrt(head_ms(q) + EPS)) * qn_ref[:, cs] * (HEAD_DIM ** -0.5 * LOG2E)
    q_ref[:, :, cs] = bm(q)
    k_ref[:, :, cs] = bm((k * lax.rsqrt(head_ms(k) + EPS)) * kn_ref[:, cs])


def _inproj_b(hm, xc, hf, w_b, qn, kn, ones_bd, wg, bg, lam):
  nt = S // TS_B
  bm_spec = pl.BlockSpec((B, TS_B, D), lambda i: (0, nt - 1 - i, 0))
  tm_spec = pl.BlockSpec((TS_B, B, D), lambda i: (nt - 1 - i, 0, 0))
  bm_out = jax.ShapeDtypeStruct((B, S, D), bf16)
  return pl.pallas_call(
      _inproj_b_kernel,
      grid=(nt,),
      in_specs=[
          bm_spec, tm_spec, tm_spec,
          _const_spec((D, 3 * D)),
          _const_spec((1, D)),
          _const_spec((1, D)),
          _const_spec((MXU_N, MXU_N)),
          _const_spec((D // LRU_GW, LRU_GW, 2 * LRU_GW)),
          _const_spec((2, D)),
          _const_spec((1, D)),
      ],
      out_specs=[bm_spec, bm_spec, bm_spec, tm_spec],
      out_shape=[bm_out, bm_out, bm_out, jax.ShapeDtypeStruct((S, B, D), bf16)],
      scratch_shapes=[pltpu.VMEM((B, D), f32)],
      compiler_params=pltpu.CompilerParams(
          dimension_semantics=("arbitrary",), vmem_limit_bytes=VMEM_LIMIT),
      name="inproj_b",
  )(hm, xc, hf, w_b, qn, kn, ones_bd, wg, bg, lam)


def _key_rows(rg):
  if rg == 0:
    return 0, WIN_ROWS
  if rg == N_RG - 1:
    return ROWS - WIN_ROWS, WIN_ROWS
  return QR * rg - WIN_ROWS // 2, KR


def _key_row0(rg):
  return _key_rows(rg)[0]


def _key_col0(n):
  return int(np.clip(QC * n - WIN_COLS // 2, 0, GRID_W - KC))


_KIND_ROWGROUPS = (0, 1, N_RG - 1)


def _table_kind(rg):
  return 0 if rg == 0 else (2 if rg == N_RG - 1 else 1)


def _natt_kernel(q_ref, k_ref, v_ref, g2_ref, cb_ref, ok_ref, o_ref, k8_ref, v8_ref, tbl_ref):
  @pl.when(pl.program_id(1) == 0)
  def _():
    for kind, rg in enumerate(_KIND_ROWGROUPS):
      for n in range(N_CB):
        for h in range(HG):
          wide = cb_ref[0, h, n]
          for r in range(QR):
            slot0 = _key_row0(rg) - (QR * rg + r) + WIN_ROWS - 1
            strip = pltpu.roll(wide, (BIAS_LANES - slot0 * KC) % BIAS_LANES, axis=1)[:, :NKEY]
            row0 = h * NQ + r * QC
            tbl_ref[kind, n, row0:row0 + QC, :] = jnp.where(
                ok_ref[kind, r:r + 1, :] != 0, strip, NEG)

  half = BF16_ROWS // 2
  lane_head = lax.broadcasted_iota(jnp.int32, (NQ, HW), 1) // HEAD_DIM

  def key_block(ref0, ref8, rows):
    parts = []
    for r in rows:
      parts.append(ref0[r:r + KC, :] if r % BF16_ROWS == 0 else ref8[r - half:r - half + KC, :])
    return jnp.concatenate(parts, axis=0)

  def one_batch(bb, carry):
    q_b, k_b, v_b, g2_b, o_b = (ref.at[bb] for ref in (q_ref, k_ref, v_ref, g2_ref, o_ref))
    for src, dst in ((k_b, k8_ref), (v_b, v8_ref)):
      for c0 in range(0, S - BF16_ROWS, KV_CHUNK):
        n = min(KV_CHUNK, S - BF16_ROWS - c0)
        wide = src[c0:c0 + n + BF16_ROWS, :].astype(f32)
        dst[c0:c0 + n, :] = wide[half:half + n].astype(bf16)

    for rg in range(N_RG):
      k0, kr = _key_rows(rg)
      nkey = kr * KC
      for n in range(N_CB):
        c0 = _key_col0(n)
        q_rows = [(QR * rg + rl) * GRID_W + QC * n for rl in range(QR)]
        k_rows = [(k0 + i) * GRID_W + c0 for i in range(kr)]
        qs = jnp.concatenate([q_b[r:r + QC, :] for r in q_rows], axis=0)
        xq = jnp.concatenate(
            [jnp.where(lane_head == h, qs, jnp.zeros_like(qs)) for h in range(HG)], axis=0)
        kb = key_block(k_b, k8_ref, k_rows)
        vb = key_block(v_b, v8_ref, k_rows)
        s = lax.dot_general(xq, kb, (((1,), (1,)), ((), ())), preferred_element_type=f32)
        s = s + tbl_ref[_table_kind(rg), n, :, 0:nkey]
        p = jnp.exp2(s - jnp.max(s, axis=1, keepdims=True))
        inv = 1.0 / jnp.sum(p, axis=1, keepdims=True)
        o = _dot(p.astype(bf16), vb) * inv
        acc = jnp.where(lane_head == 0, o[0:NQ], 0.0)
        for h in range(1, HG):
          acc = jnp.where(lane_head == h, o[h * NQ:(h + 1) * NQ], acc)
        for rl, r in enumerate(q_rows):
          o_b[r:r + QC, :] = g2_b[r:r + QC, :] * acc[rl * QC:(rl + 1) * QC]
    return carry

  lax.fori_loop(0, NB_ATT, one_batch, 0)


def _natt(q, k, v, g2, col_bias, row_ok):
  ng = N_HEADS // HG
  spec = pl.BlockSpec((NB_ATT, S, HW), lambda g, b: (b, 0, g))
  return pl.pallas_call(
      _natt_kernel,
      grid=(ng, B // NB_ATT),
      in_specs=[
          spec, spec, spec, spec,
          pl.BlockSpec((1, HG, N_CB, QC, BIAS_LANES), lambda g, b: (g, 0, 0, 0, 0)),
          pl.BlockSpec((len(_KIND_ROWGROUPS), QR, NKEY), lambda g, b: (0, 0, 0)),
      ],
      out_specs=spec,
      out_shape=jax.ShapeDtypeStruct((B, S, D), f32),
      scratch_shapes=[pltpu.VMEM((S, HW), bf16), pltpu.VMEM((S, HW), bf16),
                      pltpu.VMEM((len(_KIND_ROWGROUPS), N_CB, HG * NQ, NKEY), f32)],
      compiler_params=pltpu.CompilerParams(
          dimension_semantics=("arbitrary", "arbitrary"), vmem_limit_bytes=VMEM_LIMIT),
      name="natt",
  )(q, k, v, g2, col_bias, row_ok)


def _bias_columns(rpb):
  n_dr, n_dc = 2 * WIN_ROWS - 1, 2 * WIN_COLS - 1
  c = np.arange(KC)[:, None]
  ql = np.arange(QC)[None, :]
  col_hot, col_ok = [], []
  for n in range(N_CB):
    qc = QC * n + ql
    kc = _key_col0(n) + c
    cs = np.clip(qc - WIN_COLS // 2, 0, GRID_W - WIN_COLS)
    col_ok.append((kc >= cs) & (kc < cs + WIN_COLS))
    col_hot.append(np.eye(n_dc)[np.clip(kc - qc, -(WIN_COLS - 1), WIN_COLS - 1) + WIN_COLS - 1])
  col_ok = np.stack(col_ok).transpose(0, 2, 1)
  ng = N_HEADS // HG
  bias = jnp.einsum("ghab,ncqb->ghnqac", rpb.astype(f32).reshape(ng, HG, n_dr, n_dc),
                    jnp.asarray(np.stack(col_hot), f32), precision=lax.Precision.HIGHEST)
  bias = jnp.where(col_ok[None, None, :, :, None, :], LOG2E * bias, NEG)
  slots = BIAS_LANES // KC
  bias = jnp.pad(bias, ((0, 0),) * 4 + ((0, slots - n_dr), (0, 0)))
  return bias.reshape(ng, HG, N_CB, QC, BIAS_LANES)


def _row_window_mask():
  i = np.arange(KR)[:, None]
  rl = np.arange(QR)[None, :]
  out = []
  for rg in _KIND_ROWGROUPS:
    r = QR * rg + rl
    rs = np.clip(r - WIN_ROWS // 2, 0, ROWS - WIN_ROWS)
    krow = _key_row0(rg) + i
    ok = (krow >= rs) & (krow < rs + WIN_ROWS)
    out.append(np.repeat(ok.T[:, :, None], KC, axis=2).reshape(QR, NKEY))
  return jnp.asarray(np.stack(out), jnp.int32)


def _ffn2_kernel(x1_ref, hs_ref, g1_ref, ya_ref, wo_ref, g_ref, wgu_ref, wdn_ref,
                 o_ref, act_ref, perm_ref):
  rows = B * TS_OUT
  y_lru = (g1_ref[...] * hs_ref[...].astype(f32)).reshape(rows, D)
  y_lru = _swap_rows(y_lru, perm_ref, TS_OUT, B, PITCH_OUT)
  y = y_lru + ya_ref[...].reshape(rows, D)
  x2 = x1_ref[...].reshape(rows, D) + _dot(y.astype(bf16), wo_ref[...])
  out = x2 + FFN_RES * _swiglu(x2, g_ref, wgu_ref, wdn_ref, act_ref)
  o_ref[...] = out.reshape(B, TS_OUT, D)


def _ffn2(x1, hs, g1, ya, wo, g, wgu, wdn):
  bm_spec = pl.BlockSpec((B, TS_OUT, D), lambda j: (0, j, 0))
  tm_spec = pl.BlockSpec((TS_OUT, B, D), lambda j: (j, 0, 0))
  return pl.pallas_call(
      _ffn2_kernel,
      grid=(S // TS_OUT,),
      in_specs=[bm_spec, tm_spec, tm_spec, bm_spec,
                _const_spec((D, D)),
                _const_spec((1, D)),
                _const_spec((D, 2 * D_FF)),
                _const_spec((D_FF, D))],
      out_specs=bm_spec,
      out_shape=jax.ShapeDtypeStruct((B, S, D), f32),
      scratch_shapes=[pltpu.VMEM((B * TS_OUT, D_FF), bf16),
                      pltpu.VMEM((D // LANES, TS_OUT * PITCH_OUT, LANES), f32)],
      compiler_params=pltpu.CompilerParams(
          dimension_semantics=("parallel",), vmem_limit_bytes=VMEM_LIMIT),
      name="ffn2",
  )(x1, hs, g1, ya, wo, g, wgu, wdn)


def kernel(x, norm_ffn1, w_ffn1_gu, w_ffn1_down, norm_mix, w_in, conv_w, conv_b,
           lru_w_gates, lru_b_gates, lru_lambda, q_norm, k_norm, rel_pos_bias, w_out,
           norm_ffn2, w_ffn2_gu, w_ffn2_down):
  assert x.shape == (B, S, D) and norm_ffn1.shape[0] == 1
  l = 0
  row = lambda a: a.reshape(1, D).astype(f32)
  ones_bd = jnp.asarray(
      np.kron(np.eye(MXU_N // HEAD_DIM), np.ones((HEAD_DIM, HEAD_DIM))), dtype=bf16)
  wi = w_in[l].astype(bf16)
  w_a = jnp.concatenate([wi[:, :2 * D], wi[:, 5 * D:]], axis=1)
  w_b = wi[:, 2 * D:5 * D]

  x1, hm = _ffn1(x.reshape(T, D), row(norm_ffn1[l]), w_ffn1_gu[l].astype(bf16),
                 w_ffn1_down[l].astype(bf16), row(norm_mix[l]))
  x1, hm = x1.reshape(B, S, D), hm.reshape(B, S, D)

  gates = lambda d: (_gate_weights(lru_w_gates[l, d]), lru_b_gates[l, d].astype(f32),
                     row(lru_lambda[l, d]))
  xc, hf, g1, g2 = _inproj_a(hm, w_a, conv_w[l].astype(f32), row(conv_b[l]), *gates(0))
  q, k, v, hs = _inproj_b(hm, xc, hf, w_b,
                          row(jnp.tile(q_norm[l], N_HEADS)), row(jnp.tile(k_norm[l], N_HEADS)),
                          ones_bd, *gates(1))

  ya = _natt(q, k, v, g2, _bias_columns(rel_pos_bias[l]), _row_window_mask())

  return _ffn2(x1, hs, g1, ya, w_out[l].astype(bf16),
               row(norm_ffn2[l]), w_ffn2_gu[l].astype(bf16), w_ffn2_down[l].astype(bf16))
```

```python
import jax
import jax.numpy as jnp
import numpy as np
from jax import lax
from jax.experimental import pallas as pl
from jax.experimental.pallas import tpu as pltpu

D = 1024
B = 32
S = 2048
T = B * S
GRID_W = 64
ROWS = S // GRID_W
N_HEADS = 16
HEAD_DIM = 64
WIN_ROWS = 8
WIN_COLS = 16
LRU_BLOCK = 64
LRU_C = 8.0
CONV_W = 4
CONV_LEFT = 2
D_FF = 2816
FFN_RES = 0.5
EPS = 1e-6
NEG = -1e30
LOG2E = float(np.log2(np.e))
TINY = 1e-37

SUBLANES = 8
BF16_ROWS = 16
LANES = 128
MXU_N = 256
VMEM_LIMIT = 56 * 1024 * 1024

TM_FFN = 1024
TS_PROJ = 16
TS_B = 32
PITCH_PROJ = 24
TS_OUT = 16
PITCH_OUT = 40
FF_CHUNK = MXU_N
LRU_GW = MXU_N
KV_CHUNK = 256
HG = 4
HW = HG * HEAD_DIM
QR = 4
QC = 16
KR = 12
KC = 32
NQ = QR * QC
NKEY = KR * KC
N_CB = GRID_W // QC
N_RG = ROWS // QR
BIAS_LANES = 640

f32 = jnp.float32
bf16 = jnp.bfloat16


def _sigmoid(x):
  return 0.5 * jnp.tanh(0.5 * x) + 0.5


def _softplus(x):
  return jnp.maximum(x, 0.0) + jnp.log1p(jnp.exp(-jnp.abs(x)))


def _gelu_tanh(x, scale=1.0):
  c = float(np.sqrt(2.0 / np.pi))
  return (0.5 * scale) * x * (1.0 + jnp.tanh(c * (x + 0.044715 * (x * x * x))))


def _rmsnorm(x, g):
  ms = jnp.mean(x * x, axis=-1, keepdims=True)
  return (x * lax.rsqrt(ms + EPS)) * g


def _dot(a, b):
  return jnp.dot(a, b, preferred_element_type=f32)


def _const_spec(shape):
  nd = len(shape)
  return pl.BlockSpec(shape, lambda *_: (0,) * nd, pipeline_mode=pl.Buffered(1))


def _swiglu_hidden(x, g_ref, wgu_ref, act_ref):
  h = _rmsnorm(x, g_ref[...]).astype(bf16)
  for c in range(D_FF // FF_CHUNK):
    lo = c * FF_CHUNK
    gate = _dot(h, wgu_ref[:, lo:lo + FF_CHUNK])
    up = _dot(h, wgu_ref[:, D_FF + lo:D_FF + lo + FF_CHUNK])
    act_ref[:, lo:lo + FF_CHUNK] = (gate * _sigmoid(gate) * up).astype(bf16)


def _swiglu(x, g_ref, wgu_ref, wdn_ref, act_ref):
  _swiglu_hidden(x, g_ref, wgu_ref, act_ref)
  return _dot(act_ref[...], wdn_ref[...])


def _swap_rows(x, perm_ref, m, n, pitch):
  for i in range(m):
    for c in range(D // LANES):
      perm_ref[c, i * pitch:i * pitch + n, :] = x[i * n:(i + 1) * n, c * LANES:(c + 1) * LANES]
  return jnp.concatenate(
      [jnp.concatenate([perm_ref[c, pl.ds(j, m, stride=pitch), :] for c in range(D // LANES)], axis=-1)
       for j in range(n)], axis=0)


def _ffn1_kernel(x_ref, g_ref, wgu_ref, wdn_ref, gm_ref, o_ref, hm_ref, act_ref):
  _swiglu_hidden(x_ref[...], g_ref, wgu_ref, act_ref)
  half = TM_FFN // 2
  for lo in (0, half):
    x1 = x_ref[lo:lo + half, :] + FFN_RES * _dot(act_ref[lo:lo + half, :], wdn_ref[...])
    o_ref[lo:lo + half, :] = x1
    hm_ref[lo:lo + half, :] = _rmsnorm(x1, gm_ref[...]).astype(bf16)


def _ffn1(x2d, g, wgu, wdn, g_mix):
  spec = pl.BlockSpec((TM_FFN, D), lambda i: (i, 0))
  return pl.pallas_call(
      _ffn1_kernel,
      grid=(T // TM_FFN,),
      in_specs=[
          spec,
          _const_spec((1, D)),
          _const_spec((D, 2 * D_FF)),
          _const_spec((D_FF, D)),
          _const_spec((1, D)),
      ],
      out_specs=[spec, spec],
      out_shape=[jax.ShapeDtypeStruct((T, D), f32), jax.ShapeDtypeStruct((T, D), bf16)],
      scratch_shapes=[pltpu.VMEM((TM_FFN, D_FF), bf16)],
      compiler_params=pltpu.CompilerParams(
          dimension_semantics=("parallel",), vmem_limit_bytes=VMEM_LIMIT),
      name="ffn1",
  )(x2d, g, wgu, wdn, g_mix)


def _lru_consts(bg_ref, lam_ref):
  kexp = (-0.5 * LRU_C * LOG2E) * _softplus(-lam_ref[...])
  return kexp, 0.5 * bg_ref[0:1, :], 0.5 * bg_ref[1:2, :]


def _lru_coeffs(g, xh, kexp, b_r, b_i):
  tr = jnp.tanh(g[:, :LRU_GW] + b_r)
  ti = jnp.tanh(g[:, LRU_GW:] + b_i)
  a = jnp.exp2(kexp * (tr + 1.0))
  v = 1.0 - a * a
  sq = v * lax.rsqrt(jnp.maximum(v, TINY))
  return a, sq * ((ti + 1.0) * xh)


def _gate_weights(w):
  per = LRU_GW // LRU_BLOCK
  ncb = D // LRU_GW
  w = w.reshape(2, ncb, per, LRU_BLOCK, LRU_BLOCK)
  eye = jnp.eye(per, dtype=w.dtype)
  dense = w[:, :, :, :, None, :] * eye[None, None, :, None, :, None]
  dense = dense.reshape(2, ncb, LRU_GW, LRU_GW)
  return (0.5 * jnp.concatenate([dense[0], dense[1]], axis=-1)).astype(bf16)


def _inproj_a_kernel(h_ref, hn_ref, wx_ref, wgr_ref, wga_ref, wgb_ref, cw_ref, cb_ref, wg_ref, bg_ref,
                     lam_ref,
                     xc_ref, hf_ref, g1_ref, g2_ref, perm_ref, halo_ref, xprev_ref, car_ref):
  j = pl.program_id(0)
  last = pl.num_programs(0) - 1
  rows = B * TS_PROJ

  @pl.when(j == 0)
  def _():
    xprev_ref[...] = jnp.zeros_like(xprev_ref)
    car_ref[...] = jnp.zeros_like(car_ref)

  h_bm = h_ref[...].reshape(rows, D)
  g2_ref[...] = _sigmoid(_dot(h_bm, wgb_ref[...])).reshape(B, TS_PROJ, D)
  h_tm = _swap_rows(h_bm.astype(f32), perm_ref, B, TS_PROJ, PITCH_PROJ)
  hn = hn_ref[...].reshape(rows, D).astype(f32)
  for c in range(D // LANES):
    halo_ref[c] = hn[:, c * LANES:(c + 1) * LANES]
  hn0 = jnp.concatenate(
      [halo_ref[c, pl.ds(0, B, stride=TS_PROJ), :] for c in range(D // LANES)], axis=-1)

  lhs_x = jnp.concatenate([h_tm, hn0], axis=0).astype(bf16)
  h_tm = lhs_x[:rows]
  kexp, b_r, b_i = _lru_consts(bg_ref, lam_ref)

  for c in range(D // LRU_GW):
    cs = slice(c * LRU_GW, (c + 1) * LRU_GW)
    xr = _dot(lhs_x, wx_ref[:, cs])
    gr, ga = _dot(h_tm, wgr_ref[:, cs]), _dot(h_tm, wga_ref[:, cs])
    nxt = jnp.where(j == last, 0.0, xr[rows:]).reshape(1, B, LRU_GW)
    xr = xr[:rows].reshape(TS_PROJ, B, LRU_GW)
    seq = jnp.concatenate([xprev_ref[:, :, cs], xr, nxt], axis=0)
    xprev_ref[:, :, cs] = xr[TS_PROJ - CONV_LEFT:]
    xc = cb_ref[:, cs]
    for k in range(CONV_W):
      xc = xc + seq[k:k + TS_PROJ] * cw_ref[k:k + 1, cs]
    xc_ref[:, :, cs] = xc

    xc = xc.reshape(rows, LRU_GW)
    g = _dot(xc.astype(bf16), wg_ref[c])
    g1_ref[:, :, cs] = (_gelu_tanh(gr, scale=0.5) * _sigmoid(ga)).reshape(TS_PROJ, B, LRU_GW)

    a, bx = _lru_coeffs(g, xc, kexp[:, cs], b_r[:, cs], b_i[:, cs])
    hcur = car_ref[:, cs]
    for s in range(TS_PROJ):
      hcur = a[s * B:(s + 1) * B] * hcur + bx[s * B:(s + 1) * B]
      hf_ref[s, :, cs] = hcur.astype(bf16)
    car_ref[:, cs] = hcur


def _w_in_spec(group):
  return pl.BlockSpec((D, D), lambda *_: (0, group), pipeline_mode=pl.Buffered(1))


def _inproj_a(hm, w_in, conv_w, conv_b, wg, bg, lam):
  nt = S // TS_PROJ
  bm_spec = pl.BlockSpec((B, TS_PROJ, D), lambda j: (0, j, 0))
  tm_spec = pl.BlockSpec((TS_PROJ, B, D), lambda j: (j, 0, 0))
  next_spec = pl.BlockSpec((B, TS_PROJ, D), lambda j: (0, jnp.minimum(j + 1, nt - 1), 0))
  tm_out = jax.ShapeDtypeStruct((S, B, D), f32)
  return pl.pallas_call(
      _inproj_a_kernel,
      grid=(nt,),
      in_specs=[
          bm_spec,
          next_spec,
          _w_in_spec(0), _w_in_spec(1), _w_in_spec(5), _w_in_spec(6),
          _const_spec((CONV_W, D)),
          _const_spec((1, D)),
          _const_spec((D // LRU_GW, LRU_GW, 2 * LRU_GW)),
          _const_spec((2, D)),
          _const_spec((1, D)),
      ],
      out_specs=[tm_spec, tm_spec, tm_spec, bm_spec],
      out_shape=[tm_out, jax.ShapeDtypeStruct((S, B, D), bf16), tm_out,
                 jax.ShapeDtypeStruct((B, S, D), f32)],
      scratch_shapes=[pltpu.VMEM((D // LANES, B * PITCH_PROJ, LANES), f32),
                      pltpu.VMEM((D // LANES, B * TS_PROJ, LANES), f32),
                      pltpu.VMEM((CONV_LEFT, B, D), f32),
                      pltpu.VMEM((B, D), f32)],
      compiler_params=pltpu.CompilerParams(
          dimension_semantics=("arbitrary",), vmem_limit_bytes=VMEM_LIMIT),
      name="inproj_a",
  )(hm, hm, w_in, w_in, w_in, w_in, conv_w, conv_b, wg, bg, lam)


def _inproj_b_kernel(h_ref, xc_ref, hf_ref, wq_ref, wk_ref, wv_ref, qn_ref, kn_ref, ones_ref,
                     wg_ref, bg_ref, lam_ref, q_ref, k_ref, v_ref, hs_ref, car_ref):
  rows = B * TS_B

  @pl.when(pl.program_id(0) == 0)
  def _():
    car_ref[...] = jnp.zeros_like(car_ref)

  h = h_ref[...].reshape(rows, D)
  kexp, b_r, b_i = _lru_consts(bg_ref, lam_ref)
  bm = lambda a: a.astype(bf16).reshape(B, TS_B, MXU_N)
  head_ms = lambda t: _dot((t * t).astype(bf16), ones_ref[...]) * (1.0 / HEAD_DIM)

  for c in range(D // MXU_N):
    cs = slice(c * MXU_N, (c + 1) * MXU_N)
    xc = xc_ref[:, :, cs].reshape(rows, LRU_GW)
    g = _dot(xc.astype(bf16), wg_ref[c])
    q, k, v = _dot(h, wq_ref[:, cs]), _dot(h, wk_ref[:, cs]), _dot(h, wv_ref[:, cs])
    a, bx = _lru_coeffs(g, xc, kexp[:, cs], b_r[:, cs], b_i[:, cs])
    hcur = car_ref[:, cs]
    for s in reversed(range(TS_B)):
      hcur = a[s * B:(s + 1) * B] * hcur + bx[s * B:(s + 1) * B]
      hs_ref[s, :, cs] = (hf_ref[s, :, cs].astype(f32) + hcur).astype(bf16)
    car_ref[:, cs] = hcur

    v_ref[:, :, cs] = bm(v)
    q = (q * lax.rsqrt(head_ms(q) + EPS)) * qn_ref[:, cs] * (HEAD_DIM ** -0.5 * LOG2E)
    q_ref[:, :, cs] = bm(q)
    k_ref[:, :, cs] = bm((k * lax.rsqrt(head_ms(k) + EPS)) * kn_ref[:, cs])


def _inproj_b(hm, xc, hf, w_in, qn, kn, ones_bd, wg, bg, lam):
  nt = S // TS_B
  bm_spec = pl.BlockSpec((B, TS_B, D), lambda i: (0, nt - 1 - i, 0))
  tm_spec = pl.BlockSpec((TS_B, B, D), lambda i: (nt - 1 - i, 0, 0))
  bm_out = jax.ShapeDtypeStruct((B, S, D), bf16)
  return pl.pallas_call(
      _inproj_b_kernel,
      grid=(nt,),
      in_specs=[
          bm_spec, tm_spec, tm_spec,
          _w_in_spec(2), _w_in_spec(3), _w_in_spec(4),
          _const_spec((1, D)),
          _const_spec((1, D)),
          _const_spec((MXU_N, MXU_N)),
          _const_spec((D // LRU_GW, LRU_GW, 2 * LRU_GW)),
          _const_spec((2, D)),
          _const_spec((1, D)),
      ],
      out_specs=[bm_spec, bm_spec, bm_spec, tm_spec],
      out_shape=[bm_out, bm_out, bm_out, jax.ShapeDtypeStruct((S, B, D), bf16)],
      scratch_shapes=[pltpu.VMEM((B, D), f32)],
      compiler_params=pltpu.CompilerParams(
          dimension_semantics=("arbitrary",), vmem_limit_bytes=VMEM_LIMIT),
      name="inproj_b",
  )(hm, xc, hf, w_in, w_in, w_in, qn, kn, ones_bd, wg, bg, lam)


def _key_rows(rg):
  if rg == 0:
    return 0, WIN_ROWS
  if rg == N_RG - 1:
    return ROWS - WIN_ROWS, WIN_ROWS
  return QR * rg - WIN_ROWS // 2, KR


def _key_row0(rg):
  return _key_rows(rg)[0]


def _key_col0(n):
  return int(np.clip(QC * n - WIN_COLS // 2, 0, GRID_W - KC))


_KIND_ROWGROUPS = (0, 1, N_RG - 1)


def _table_kind(rg):
  return 0 if rg == 0 else (2 if rg == N_RG - 1 else 1)


def _natt_kernel(q_ref, k_ref, v_ref, g2_ref, cb_ref, ok_ref, o_ref, k8_ref, v8_ref, tbl_ref):
  @pl.when(pl.program_id(1) == 0)
  def _():
    for kind, rg in enumerate(_KIND_ROWGROUPS):
      for n in range(N_CB):
        for h in range(HG):
          wide = cb_ref[0, h, n]
          for r in range(QR):
            slot0 = _key_row0(rg) - (QR * rg + r) + WIN_ROWS - 1
            strip = pltpu.roll(wide, (BIAS_LANES - slot0 * KC) % BIAS_LANES, axis=1)[:, :NKEY]
            row0 = h * NQ + r * QC
            tbl_ref[kind, n, row0:row0 + QC, :] = jnp.where(
                ok_ref[kind, r:r + 1, :] != 0, strip, NEG)

  half = BF16_ROWS // 2
  for src, dst in ((k_ref, k8_ref), (v_ref, v8_ref)):
    for c0 in range(0, S - BF16_ROWS, KV_CHUNK):
      n = min(KV_CHUNK, S - BF16_ROWS - c0)
      wide = src[c0:c0 + n + BF16_ROWS, :].astype(f32)
      dst[c0:c0 + n, :] = wide[half:half + n].astype(bf16)

  def key_block(ref0, ref8, rows):
    parts = []
    for r in rows:
      parts.append(ref0[r:r + KC, :] if r % BF16_ROWS == 0 else ref8[r - half:r - half + KC, :])
    return jnp.concatenate(parts, axis=0)

  lane_head = lax.broadcasted_iota(jnp.int32, (NQ, HW), 1) // HEAD_DIM
  for rg in range(N_RG):
    k0, kr = _key_rows(rg)
    nkey = kr * KC
    for n in range(N_CB):
      c0 = _key_col0(n)
      q_rows = [(QR * rg + rl) * GRID_W + QC * n for rl in range(QR)]
      k_rows = [(k0 + i) * GRID_W + c0 for i in range(kr)]
      qs = jnp.concatenate([q_ref[r:r + QC, :] for r in q_rows], axis=0)
      xq = jnp.concatenate(
          [jnp.where(lane_head == h, qs, jnp.zeros_like(qs)) for h in range(HG)], axis=0)
      kb = key_block(k_ref, k8_ref, k_rows)
      vb = key_block(v_ref, v8_ref, k_rows)
      s = lax.dot_general(xq, kb, (((1,), (1,)), ((), ())), preferred_element_type=f32)
      s = s + tbl_ref[_table_kind(rg), n, :, 0:nkey]
      p = jnp.exp2(s - jnp.max(s, axis=1, keepdims=True))
      inv = 1.0 / jnp.sum(p, axis=1, keepdims=True)
      o = _dot(p.astype(bf16), vb) * inv
      acc = jnp.where(lane_head == 0, o[0:NQ], 0.0)
      for h in range(1, HG):
        acc = jnp.where(lane_head == h, o[h * NQ:(h + 1) * NQ], acc)
      for rl, r in enumerate(q_rows):
        o_ref[r:r + QC, :] = g2_ref[r:r + QC, :] * acc[rl * QC:(rl + 1) * QC]


def _natt(q, k, v, g2, col_bias, row_ok):
  ng = N_HEADS // HG
  spec = pl.BlockSpec((None, S, HW), lambda g, b: (b, 0, g))
  return pl.pallas_call(
      _natt_kernel,
      grid=(ng, B),
      in_specs=[
          spec, spec, spec, spec,
          pl.BlockSpec((1, HG, N_CB, QC, BIAS_LANES), lambda g, b: (g, 0, 0, 0, 0)),
          pl.BlockSpec((len(_KIND_ROWGROUPS), QR, NKEY), lambda g, b: (0, 0, 0)),
      ],
      out_specs=spec,
      out_shape=jax.ShapeDtypeStruct((B, S, D), f32),
      scratch_shapes=[pltpu.VMEM((S, HW), bf16), pltpu.VMEM((S, HW), bf16),
                      pltpu.VMEM((len(_KIND_ROWGROUPS), N_CB, HG * NQ, NKEY), f32)],
      compiler_params=pltpu.CompilerParams(
          dimension_semantics=("arbitrary", "arbitrary"), vmem_limit_bytes=VMEM_LIMIT),
      name="natt",
  )(q, k, v, g2, col_bias, row_ok)


def _bias_columns(rpb):
  n_dr, n_dc = 2 * WIN_ROWS - 1, 2 * WIN_COLS - 1
  c = np.arange(KC)[:, None]
  ql = np.arange(QC)[None, :]
  col_hot, col_ok = [], []
  for n in range(N_CB):
    qc = QC * n + ql
    kc = _key_col0(n) + c
    cs = np.clip(qc - WIN_COLS // 2, 0, GRID_W - WIN_COLS)
    col_ok.append((kc >= cs) & (kc < cs + WIN_COLS))
    col_hot.append(np.eye(n_dc)[np.clip(kc - qc, -(WIN_COLS - 1), WIN_COLS - 1) + WIN_COLS - 1])
  col_ok = np.stack(col_ok).transpose(0, 2, 1)
  ng = N_HEADS // HG
  bias = jnp.einsum("ghab,ncqb->ghnqac", rpb.astype(f32).reshape(ng, HG, n_dr, n_dc),
                    jnp.asarray(np.stack(col_hot), f32), precision=lax.Precision.HIGHEST)
  bias = jnp.where(col_ok[None, None, :, :, None, :], LOG2E * bias, NEG)
  slots = BIAS_LANES // KC
  bias = jnp.pad(bias, ((0, 0),) * 4 + ((0, slots - n_dr), (0, 0)))
  return bias.reshape(ng, HG, N_CB, QC, BIAS_LANES)


def _row_window_mask():
  i = np.arange(KR)[:, None]
  rl = np.arange(QR)[None, :]
  out = []
  for rg in _KIND_ROWGROUPS:
    r = QR * rg + rl
    rs = np.clip(r - WIN_ROWS // 2, 0, ROWS - WIN_ROWS)
    krow = _key_row0(rg) + i
    ok = (krow >= rs) & (krow < rs + WIN_ROWS)
    out.append(np.repeat(ok.T[:, :, None], KC, axis=2).reshape(QR, NKEY))
  return jnp.asarray(np.stack(out), jnp.int32)


def _ffn2_kernel(x1_ref, hs_ref, g1_ref, ya_ref, wo_ref, g_ref, wgu_ref, wdn_ref,
                 o_ref, act_ref, perm_ref):
  rows = B * TS_OUT
  y_lru = (g1_ref[...] * hs_ref[...].astype(f32)).reshape(rows, D)
  y_lru = _swap_rows(y_lru, perm_ref, TS_OUT, B, PITCH_OUT)
  y = y_lru + ya_ref[...].reshape(rows, D)
  x2 = x1_ref[...].reshape(rows, D) + _dot(y.astype(bf16), wo_ref[...])
  out = x2 + FFN_RES * _swiglu(x2, g_ref, wgu_ref, wdn_ref, act_ref)
  o_ref[...] = out.reshape(B, TS_OUT, D)


def _ffn2(x1, hs, g1, ya, wo, g, wgu, wdn):
  bm_spec = pl.BlockSpec((B, TS_OUT, D), lambda j: (0, j, 0))
  tm_spec = pl.BlockSpec((TS_OUT, B, D), lambda j: (j, 0, 0))
  return pl.pallas_call(
      _ffn2_kernel,
      grid=(S // TS_OUT,),
      in_specs=[bm_spec, tm_spec, tm_spec, bm_spec,
                _const_spec((D, D)),
                _const_spec((1, D)),
                _const_spec((D, 2 * D_FF)),
                _const_spec((D_FF, D))],
      out_specs=bm_spec,
      out_shape=jax.ShapeDtypeStruct((B, S, D), f32),
      scratch_shapes=[pltpu.VMEM((B * TS_OUT, D_FF), bf16),
                      pltpu.VMEM((D // LANES, TS_OUT * PITCH_OUT, LANES), f32)],
      compiler_params=pltpu.CompilerParams(
          dimension_semantics=("parallel",), vmem_limit_bytes=VMEM_LIMIT),
      name="ffn2",
  )(x1, hs, g1, ya, wo, g, wgu, wdn)


def kernel(x, norm_ffn1, w_ffn1_gu, w_ffn1_down, norm_mix, w_in, conv_w, conv_b,
           lru_w_gates, lru_b_gates, lru_lambda, q_norm, k_norm, rel_pos_bias, w_out,
           norm_ffn2, w_ffn2_gu, w_ffn2_down):
  assert x.shape == (B, S, D) and norm_ffn1.shape[0] == 1
  l = 0
  row = lambda a: a.reshape(1, D).astype(f32)
  ones_bd = jnp.asarray(
      np.kron(np.eye(MXU_N // HEAD_DIM), np.ones((HEAD_DIM, HEAD_DIM))), dtype=bf16)
  wi = w_in[l].astype(bf16)

  x1, hm = _ffn1(x.reshape(T, D), row(norm_ffn1[l]), w_ffn1_gu[l].astype(bf16),
                 w_ffn1_down[l].astype(bf16), row(norm_mix[l]))
  x1, hm = x1.reshape(B, S, D), hm.reshape(B, S, D)

  gates = lambda d: (_gate_weights(lru_w_gates[l, d]), lru_b_gates[l, d].astype(f32),
                     row(lru_lambda[l, d]))
  xc, hf, g1, g2 = _inproj_a(hm, wi, conv_w[l].astype(f32), row(conv_b[l]), *gates(0))
  q, k, v, hs = _inproj_b(hm, xc, hf, wi,
                          row(jnp.tile(q_norm[l], N_HEADS)), row(jnp.tile(k_norm[l], N_HEADS)),
                          ones_bd, *gates(1))

  ya = _natt(q, k, v, g2, _bias_columns(rel_pos_bias[l]), _row_window_mask())

  return _ffn2(x1, hs, g1, ya, w_out[l].astype(bf16),
               row(norm_ffn2[l]), w_ffn2_gu[l].astype(bf16), w_ffn2_down[l].astype(bf16))
```

```python
import jax
import jax.numpy as jnp
import numpy as np
from jax import lax
from jax.experimental import pallas as pl
from jax.experimental.pallas import tpu as pltpu

D = 1024
B = 32
S = 2048
T = B * S
GRID_W = 64
ROWS = S // GRID_W
N_HEADS = 16
HEAD_DIM = 64
WIN_ROWS = 8
WIN_COLS = 16
LRU_BLOCK = 64
LRU_C = 8.0
CONV_W = 4
CONV_LEFT = 2
D_FF = 2816
FFN_RES = 0.5
EPS = 1e-6
NEG = -1e30
LOG2E = float(np.log2(np.e))
TINY = 1e-37

SUBLANES = 8
BF16_ROWS = 16
LANES = 128
MXU_N = 256
VMEM_LIMIT = 56 * 1024 * 1024

TM_FFN = 1024
TS_PROJ = 16
TS_B = 32
PITCH_PROJ = 24
TS_OUT = 16
PITCH_OUT = 40
FF_CHUNK = MXU_N
LRU_GW = MXU_N
KV_CHUNK = 256
HG = 4
NB_ATT = 2
HW = HG * HEAD_DIM
QR = 4
QC = 16
KR = 12
KC = 32
NQ = QR * QC
NKEY = KR * KC
N_CB = GRID_W // QC
N_RG = ROWS // QR
BIAS_LANES = 640

f32 = jnp.float32
bf16 = jnp.bfloat16


def _sigmoid(x):
  return 0.5 * jnp.tanh(0.5 * x) + 0.5


def _softplus(x):
  return jnp.maximum(x, 0.0) + jnp.log1p(jnp.exp(-jnp.abs(x)))


def _gelu_tanh(x, scale=1.0):
  c = float(np.sqrt(2.0 / np.pi))
  return (0.5 * scale) * x * (1.0 + jnp.tanh(c * (x + 0.044715 * (x * x * x))))


def _rmsnorm(x, g):
  ms = jnp.mean(x * x, axis=-1, keepdims=True)
  return (x * lax.rsqrt(ms + EPS)) * g


def _dot(a, b):
  return jnp.dot(a, b, preferred_element_type=f32)


def _const_spec(shape):
  nd = len(shape)
  return pl.BlockSpec(shape, lambda *_: (0,) * nd, pipeline_mode=pl.Buffered(1))


def _swiglu_hidden(x, g_ref, wgu_ref, act_ref):
  h = _rmsnorm(x, g_ref[...]).astype(bf16)
  for c in range(D_FF // FF_CHUNK):
    lo = c * FF_CHUNK
    gate = _dot(h, wgu_ref[:, lo:lo + FF_CHUNK])
    up = _dot(h, wgu_ref[:, D_FF + lo:D_FF + lo + FF_CHUNK])
    act_ref[:, lo:lo + FF_CHUNK] = (gate * _sigmoid(gate) * up).astype(bf16)


def _swiglu(x, g_ref, wgu_ref, wdn_ref, act_ref):
  _swiglu_hidden(x, g_ref, wgu_ref, act_ref)
  return _dot(act_ref[...], wdn_ref[...])


def _swap_rows(x, perm_ref, m, n, pitch):
  for i in range(m):
    for c in range(D // LANES):
      perm_ref[c, i * pitch:i * pitch + n, :] = x[i * n:(i + 1) * n, c * LANES:(c + 1) * LANES]
  return jnp.concatenate(
      [jnp.concatenate([perm_ref[c, pl.ds(j, m, stride=pitch), :] for c in range(D // LANES)], axis=-1)
       for j in range(n)], axis=0)


def _ffn1_kernel(x_ref, g_ref, wgu_ref, wdn_ref, gm_ref, o_ref, hm_ref, act_ref):
  _swiglu_hidden(x_ref[...], g_ref, wgu_ref, act_ref)
  half = TM_FFN // 2
  for lo in (0, half):
    x1 = x_ref[lo:lo + half, :] + FFN_RES * _dot(act_ref[lo:lo + half, :], wdn_ref[...])
    o_ref[lo:lo + half, :] = x1
    hm_ref[lo:lo + half, :] = _rmsnorm(x1, gm_ref[...]).astype(bf16)


def _ffn1(x2d, g, wgu, wdn, g_mix):
  spec = pl.BlockSpec((TM_FFN, D), lambda i: (i, 0))
  return pl.pallas_call(
      _ffn1_kernel,
      grid=(T // TM_FFN,),
      in_specs=[
          spec,
          _const_spec((1, D)),
          _const_spec((D, 2 * D_FF)),
          _const_spec((D_FF, D)),
          _const_spec((1, D)),
      ],
      out_specs=[spec, spec],
      out_shape=[jax.ShapeDtypeStruct((T, D), f32), jax.ShapeDtypeStruct((T, D), bf16)],
      scratch_shapes=[pltpu.VMEM((TM_FFN, D_FF), bf16)],
      compiler_params=pltpu.CompilerParams(
          dimension_semantics=("parallel",), vmem_limit_bytes=VMEM_LIMIT),
      name="ffn1",
  )(x2d, g, wgu, wdn, g_mix)


def _lru_consts(bg_ref, lam_ref):
  kexp = (-0.5 * LRU_C * LOG2E) * _softplus(-lam_ref[...])
  return kexp, 0.5 * bg_ref[0:1, :], 0.5 * bg_ref[1:2, :]


def _lru_coeffs(g, xh, kexp, b_r, b_i):
  tr = jnp.tanh(g[:, :LRU_GW] + b_r)
  ti = jnp.tanh(g[:, LRU_GW:] + b_i)
  a = jnp.exp2(kexp * (tr + 1.0))
  v = 1.0 - a * a
  sq = v * lax.rsqrt(jnp.maximum(v, TINY))
  return a, sq * ((ti + 1.0) * xh)


def _gate_weights(w):
  per = LRU_GW // LRU_BLOCK
  ncb = D // LRU_GW
  w = w.reshape(2, ncb, per, LRU_BLOCK, LRU_BLOCK)
  eye = jnp.eye(per, dtype=w.dtype)
  dense = w[:, :, :, :, None, :] * eye[None, None, :, None, :, None]
  dense = dense.reshape(2, ncb, LRU_GW, LRU_GW)
  return (0.5 * jnp.concatenate([dense[0], dense[1]], axis=-1)).astype(bf16)


def _inproj_a_kernel(h_ref, hn_ref, wx_ref, wgr_ref, wga_ref, wgb_ref, cw_ref, cb_ref, wg_ref, bg_ref,
                     lam_ref,
                     xc_ref, hf_ref, g1_ref, g2_ref, perm_ref, halo_ref, xprev_ref, car_ref):
  j = pl.program_id(0)
  last = pl.num_programs(0) - 1
  rows = B * TS_PROJ

  @pl.when(j == 0)
  def _():
    xprev_ref[...] = jnp.zeros_like(xprev_ref)
    car_ref[...] = jnp.zeros_like(car_ref)

  h_bm = h_ref[...].reshape(rows, D)
  g2_ref[...] = _sigmoid(_dot(h_bm, wgb_ref[...])).reshape(B, TS_PROJ, D)
  h_tm = _swap_rows(h_bm.astype(f32), perm_ref, B, TS_PROJ, PITCH_PROJ)
  hn = hn_ref[...].reshape(rows, D).astype(f32)
  for c in range(D // LANES):
    halo_ref[c] = hn[:, c * LANES:(c + 1) * LANES]
  hn0 = jnp.concatenate(
      [halo_ref[c, pl.ds(0, B, stride=TS_PROJ), :] for c in range(D // LANES)], axis=-1)

  lhs_x = jnp.concatenate([h_tm, hn0], axis=0).astype(bf16)
  h_tm = lhs_x[:rows]
  kexp, b_r, b_i = _lru_consts(bg_ref, lam_ref)

  for c in range(D // LRU_GW):
    cs = slice(c * LRU_GW, (c + 1) * LRU_GW)
    xr = _dot(lhs_x, wx_ref[:, cs])
    gr, ga = _dot(h_tm, wgr_ref[:, cs]), _dot(h_tm, wga_ref[:, cs])
    nxt = jnp.where(j == last, 0.0, xr[rows:]).reshape(1, B, LRU_GW)
    xr = xr[:rows].reshape(TS_PROJ, B, LRU_GW)
    seq = jnp.concatenate([xprev_ref[:, :, cs], xr, nxt], axis=0)
    xprev_ref[:, :, cs] = xr[TS_PROJ - CONV_LEFT:]
    xc = cb_ref[:, cs]
    for k in range(CONV_W):
      xc = xc + seq[k:k + TS_PROJ] * cw_ref[k:k + 1, cs]
    xc_ref[:, :, cs] = xc

    xc = xc.reshape(rows, LRU_GW)
    g = _dot(xc.astype(bf16), wg_ref[c])
    g1_ref[:, :, cs] = (_gelu_tanh(gr, scale=0.5) * _sigmoid(ga)).reshape(TS_PROJ, B, LRU_GW)

    a, bx = _lru_coeffs(g, xc, kexp[:, cs], b_r[:, cs], b_i[:, cs])
    hcur = car_ref[:, cs]
    for s in range(TS_PROJ):
      hcur = a[s * B:(s + 1) * B] * hcur + bx[s * B:(s + 1) * B]
      hf_ref[s, :, cs] = hcur.astype(bf16)
    car_ref[:, cs] = hcur


def _w_in_spec(group):
  return pl.BlockSpec((D, D), lambda *_: (0, group), pipeline_mode=pl.Buffered(1))


def _inproj_a(hm, w_in, conv_w, conv_b, wg, bg, lam):
  nt = S // TS_PROJ
  bm_spec = pl.BlockSpec((B, TS_PROJ, D), lambda j: (0, j, 0))
  tm_spec = pl.BlockSpec((TS_PROJ, B, D), lambda j: (j, 0, 0))
  next_spec = pl.BlockSpec((B, TS_PROJ, D), lambda j: (0, jnp.minimum(j + 1, nt - 1), 0))
  tm_out = jax.ShapeDtypeStruct((S, B, D), f32)
  return pl.pallas_call(
      _inproj_a_kernel,
      grid=(nt,),
      in_specs=[
          bm_spec,
          next_spec,
          _w_in_spec(0), _w_in_spec(1), _w_in_spec(5), _w_in_spec(6),
          _const_spec((CONV_W, D)),
          _const_spec((1, D)),
          _const_spec((D // LRU_GW, LRU_GW, 2 * LRU_GW)),
          _const_spec((2, D)),
          _const_spec((1, D)),
      ],
      out_specs=[tm_spec, tm_spec, tm_spec, bm_spec],
      out_shape=[tm_out, jax.ShapeDtypeStruct((S, B, D), bf16), tm_out,
                 jax.ShapeDtypeStruct((B, S, D), f32)],
      scratch_shapes=[pltpu.VMEM((D // LANES, B * PITCH_PROJ, LANES), f32),
                      pltpu.VMEM((D // LANES, B * TS_PROJ, LANES), f32),
                      pltpu.VMEM((CONV_LEFT, B, D), f32),
                      pltpu.VMEM((B, D), f32)],
      compiler_params=pltpu.CompilerParams(
          dimension_semantics=("arbitrary",), vmem_limit_bytes=VMEM_LIMIT),
      name="inproj_a",
  )(hm, hm, w_in, w_in, w_in, w_in, conv_w, conv_b, wg, bg, lam)


def _inproj_b_kernel(h_ref, xc_ref, hf_ref, wq_ref, wk_ref, wv_ref, qn_ref, kn_ref, ones_ref,
                     wg_ref, bg_ref, lam_ref, q_ref, k_ref, v_ref, hs_ref, car_ref):
  rows = B * TS_B

  @pl.when(pl.program_id(0) == 0)
  def _():
    car_ref[...] = jnp.zeros_like(car_ref)

  h = h_ref[...].reshape(rows, D)
  kexp, b_r, b_i = _lru_consts(bg_ref, lam_ref)
  bm = lambda a: a.astype(bf16).reshape(B, TS_B, MXU_N)
  head_ms = lambda t: _dot((t * t).astype(bf16), ones_ref[...]) * (1.0 / HEAD_DIM)

  for c in range(D // MXU_N):
    cs = slice(c * MXU_N, (c + 1) * MXU_N)
    xc = xc_ref[:, :, cs].reshape(rows, LRU_GW)
    g = _dot(xc.astype(bf16), wg_ref[c])
    q, k, v = _dot(h, wq_ref[:, cs]), _dot(h, wk_ref[:, cs]), _dot(h, wv_ref[:, cs])
    a, bx = _lru_coeffs(g, xc, kexp[:, cs], b_r[:, cs], b_i[:, cs])
    hcur = car_ref[:, cs]
    for s in reversed(range(TS_B)):
      hcur = a[s * B:(s + 1) * B] * hcur + bx[s * B:(s + 1) * B]
      hs_ref[s, :, cs] = (hf_ref[s, :, cs].astype(f32) + hcur).astype(bf16)
    car_ref[:, cs] = hcur

    v_ref[:, :, cs] = bm(v)
    q = (q * lax.rsqrt(head_ms(q) + EPS)) * qn_ref[:, cs] * (HEAD_DIM ** -0.5 * LOG2E)
    q_ref[:, :, cs] = bm(q)
    k_ref[:, :, cs] = bm((k * lax.rsqrt(head_ms(k) + EPS)) * kn_ref[:, cs])


def _inproj_b(hm, xc, hf, w_in, qn, kn, ones_bd, wg, bg, lam):
  nt = S // TS_B
  bm_spec = pl.BlockSpec((B, TS_B, D), lambda i: (0, nt - 1 - i, 0))
  tm_spec = pl.BlockSpec((TS_B, B, D), lambda i: (nt - 1 - i, 0, 0))
  bm_out = jax.ShapeDtypeStruct((B, S, D), bf16)
  return pl.pallas_call(
      _inproj_b_kernel,
      grid=(nt,),
      in_specs=[
          bm_spec, tm_spec, tm_spec,
          _w_in_spec(2), _w_in_spec(3), _w_in_spec(4),
          _const_spec((1, D)),
          _const_spec((1, D)),
          _const_spec((MXU_N, MXU_N)),
          _const_spec((D // LRU_GW, LRU_GW, 2 * LRU_GW)),
          _const_spec((2, D)),
          _const_spec((1, D)),
      ],
      out_specs=[bm_spec, bm_spec, bm_spec, tm_spec],
      out_shape=[bm_out, bm_out, bm_out, jax.ShapeDtypeStruct((S, B, D), bf16)],
      scratch_shapes=[pltpu.VMEM((B, D), f32)],
      compiler_params=pltpu.CompilerParams(
          dimension_semantics=("arbitrary",), vmem_limit_bytes=VMEM_LIMIT),
      name="inproj_b",
  )(hm, xc, hf, w_in, w_in, w_in, qn, kn, ones_bd, wg, bg, lam)


def _key_rows(rg):
  if rg == 0:
    return 0, WIN_ROWS
  if rg == N_RG - 1:
    return ROWS - WIN_ROWS, WIN_ROWS
  return QR * rg - WIN_ROWS // 2, KR


def _key_row0(rg):
  return _key_rows(rg)[0]


def _key_col0(n):
  return int(np.clip(QC * n - WIN_COLS // 2, 0, GRID_W - KC))


_KIND_ROWGROUPS = (0, 1, N_RG - 1)


def _table_kind(rg):
  return 0 if rg == 0 else (2 if rg == N_RG - 1 else 1)


def _natt_kernel(q_ref, k_ref, v_ref, g2_ref, cb_ref, ok_ref, o_ref, k8_ref, v8_ref, tbl_ref):
  @pl.when(pl.program_id(1) == 0)
  def _():
    for kind, rg in enumerate(_KIND_ROWGROUPS):
      for n in range(N_CB):
        for h in range(HG):
          wide = cb_ref[0, h, n]
          for r in range(QR):
            slot0 = _key_row0(rg) - (QR * rg + r) + WIN_ROWS - 1
            strip = pltpu.roll(wide, (BIAS_LANES - slot0 * KC) % BIAS_LANES, axis=1)[:, :NKEY]
            row0 = h * NQ + r * QC
            tbl_ref[kind, n, row0:row0 + QC, :] = jnp.where(
                ok_ref[kind, r:r + 1, :] != 0, strip, NEG)

  half = BF16_ROWS // 2
  lane_head = lax.broadcasted_iota(jnp.int32, (NQ, HW), 1) // HEAD_DIM

  for bb in range(NB_ATT):
    for src, dst in ((k_ref.at[bb], k8_ref.at[bb]), (v_ref.at[bb], v8_ref.at[bb])):
      for c0 in range(0, S - BF16_ROWS, KV_CHUNK):
        n = min(KV_CHUNK, S - BF16_ROWS - c0)
        wide = src[c0:c0 + n + BF16_ROWS, :].astype(f32)
        dst[c0:c0 + n, :] = wide[half:half + n].astype(bf16)

    def key_block(ref0, ref8, rows):
      parts = []
      for r in rows:
        parts.append(ref0[r:r + KC, :] if r % BF16_ROWS == 0 else ref8[r - half:r - half + KC, :])
      return jnp.concatenate(parts, axis=0)

    for rg in range(N_RG):
      k0, kr = _key_rows(rg)
      nkey = kr * KC
      for n in range(N_CB):
        c0 = _key_col0(n)
        q_rows = [(QR * rg + rl) * GRID_W + QC * n for rl in range(QR)]
        k_rows = [(k0 + i) * GRID_W + c0 for i in range(kr)]
        qs = jnp.concatenate([q_ref[bb, r:r + QC, :] for r in q_rows], axis=0)
        xq = jnp.concatenate(
            [jnp.where(lane_head == h, qs, jnp.zeros_like(qs)) for h in range(HG)], axis=0)
        kb = key_block(k_ref.at[bb], k8_ref.at[bb], k_rows)
        vb = key_block(v_ref.at[bb], v8_ref.at[bb], k_rows)
        s = lax.dot_general(xq, kb, (((1,), (1,)), ((), ())), preferred_element_type=f32)
        s = s + tbl_ref[_table_kind(rg), n, :, 0:nkey]
        p = jnp.exp2(s - jnp.max(s, axis=1, keepdims=True))
        inv = 1.0 / jnp.sum(p, axis=1, keepdims=True)
        o = _dot(p.astype(bf16), vb) * inv
        acc = jnp.where(lane_head == 0, o[0:NQ], 0.0)
        for h in range(1, HG):
          acc = jnp.where(lane_head == h, o[h * NQ:(h + 1) * NQ], acc)
        for rl, r in enumerate(q_rows):
          o_ref[bb, r:r + QC, :] = g2_ref[bb, r:r + QC, :] * acc[rl * QC:(rl + 1) * QC]


def _natt(q, k, v, g2, col_bias, row_ok):
  ng = N_HEADS // HG
  spec = pl.BlockSpec((NB_ATT, S, HW), lambda g, b: (b, 0, g))
  return pl.pallas_call(
      _natt_kernel,
      grid=(ng, B // NB_ATT),
      in_specs=[
          spec, spec, spec, spec,
          pl.BlockSpec((1, HG, N_CB, QC, BIAS_LANES), lambda g, b: (g, 0, 0, 0, 0)),
          pl.BlockSpec((len(_KIND_ROWGROUPS), QR, NKEY), lambda g, b: (0, 0, 0)),
      ],
      out_specs=spec,
      out_shape=jax.ShapeDtypeStruct((B, S, D), f32),
      scratch_shapes=[pltpu.VMEM((NB_ATT, S, HW), bf16), pltpu.VMEM((NB_ATT, S, HW), bf16),
                      pltpu.VMEM((len(_KIND_ROWGROUPS), N_CB, HG * NQ, NKEY), f32)],
      compiler_params=pltpu.CompilerParams(
          dimension_semantics=("arbitrary", "arbitrary"), vmem_limit_bytes=VMEM_LIMIT),
      name="natt",
  )(q, k, v, g2, col_bias, row_ok)


def _bias_columns(rpb):
  n_dr, n_dc = 2 * WIN_ROWS - 1, 2 * WIN_COLS - 1
  c = np.arange(KC)[:, None]
  ql = np.arange(QC)[None, :]
  col_hot, col_ok = [], []
  for n in range(N_CB):
    qc = QC * n + ql
    kc = _key_col0(n) + c
    cs = np.clip(qc - WIN_COLS // 2, 0, GRID_W - WIN_COLS)
    col_ok.append((kc >= cs) & (kc < cs + WIN_COLS))
    col_hot.append(np.eye(n_dc)[np.clip(kc - qc, -(WIN_COLS - 1), WIN_COLS - 1) + WIN_COLS - 1])
  col_ok = np.stack(col_ok).transpose(0, 2, 1)
  ng = N_HEADS // HG
  bias = jnp.einsum("ghab,ncqb->ghnqac", rpb.astype(f32).reshape(ng, HG, n_dr, n_dc),
                    jnp.asarray(np.stack(col_hot), f32), precision=lax.Precision.HIGHEST)
  bias = jnp.where(col_ok[None, None, :, :, None, :], LOG2E * bias, NEG)
  slots = BIAS_LANES // KC
  bias = jnp.pad(bias, ((0, 0),) * 4 + ((0, slots - n_dr), (0, 0)))
  return bias.reshape(ng, HG, N_CB, QC, BIAS_LANES)


def _row_window_mask():
  i = np.arange(KR)[:, None]
  rl = np.arange(QR)[None, :]
  out = []
  for rg in _KIND_ROWGROUPS:
    r = QR * rg + rl
    rs = np.clip(r - WIN_ROWS // 2, 0, ROWS - WIN_ROWS)
    krow = _key_row0(rg) + i
    ok = (krow >= rs) & (krow < rs + WIN_ROWS)
    out.append(np.repeat(ok.T[:, :, None], KC, axis=2).reshape(QR, NKEY))
  return jnp.asarray(np.stack(out), jnp.int32)


def _ffn2_kernel(x1_ref, hs_ref, g1_ref, ya_ref, wo_ref, g_ref, wgu_ref, wdn_ref,
                 o_ref, act_ref, perm_ref):
  rows = B * TS_OUT
  y_lru = (g1_ref[...] * hs_ref[...].astype(f32)).reshape(rows, D)
  y_lru = _swap_rows(y_lru, perm_ref, TS_OUT, B, PITCH_OUT)
  y = y_lru + ya_ref[...].reshape(rows, D)
  x2 = x1_ref[...].reshape(rows, D) + _dot(y.astype(bf16), wo_ref[...])
  out = x2 + FFN_RES * _swiglu(x2, g_ref, wgu_ref, wdn_ref, act_ref)
  o_ref[...] = out.reshape(B, TS_OUT, D)


def _ffn2(x1, hs, g1, ya, wo, g, wgu, wdn):
  bm_spec = pl.BlockSpec((B, TS_OUT, D), lambda j: (0, j, 0))
  tm_spec = pl.BlockSpec((TS_OUT, B, D), lambda j: (j, 0, 0))
  return pl.pallas_call(
      _ffn2_kernel,
      grid=(S // TS_OUT,),
      in_specs=[bm_spec, tm_spec, tm_spec, bm_spec,
                _const_spec((D, D)),
                _const_spec((1, D)),
                _const_spec((D, 2 * D_FF)),
                _const_spec((D_FF, D))],
      out_specs=bm_spec,
      out_shape=jax.ShapeDtypeStruct((B, S, D), f32),
      scratch_shapes=[pltpu.VMEM((B * TS_OUT, D_FF), bf16),
                      pltpu.VMEM((D // LANES, TS_OUT * PITCH_OUT, LANES), f32)],
      compiler_params=pltpu.CompilerParams(
          dimension_semantics=("parallel",), vmem_limit_bytes=VMEM_LIMIT),
      name="ffn2",
  )(x1, hs, g1, ya, wo, g, wgu, wdn)


def kernel(x, norm_ffn1, w_ffn1_gu, w_ffn1_down, norm_mix, w_in, conv_w, conv_b,
           lru_w_gates, lru_b_gates, lru_lambda, q_norm, k_norm, rel_pos_bias, w_out,
           norm_ffn2, w_ffn2_gu, w_ffn2_down):
  assert x.shape == (B, S, D) and norm_ffn1.shape[0] == 1
  l = 0
  row = lambda a: a.reshape(1, D).astype(f32)
  ones_bd = jnp.asarray(
      np.kron(np.eye(MXU_N // HEAD_DIM), np.ones((HEAD_DIM, HEAD_DIM))), dtype=bf16)
  wi = w_in[l].astype(bf16)

  x1, hm = _ffn1(x.reshape(T, D), row(norm_ffn1[l]), w_ffn1_gu[l].astype(bf16),
                 w_ffn1_down[l].astype(bf16), row(norm_mix[l]))
  x1, hm = x1.reshape(B, S, D), hm.reshape(B, S, D)

  gates = lambda d: (_gate_weights(lru_w_gates[l, d]), lru_b_gates[l, d].astype(f32),
                     row(lru_lambda[l, d]))
  xc, hf, g1, g2 = _inproj_a(hm, wi, conv_w[l].astype(f32), row(conv_b[l]), *gates(0))
  q, k, v, hs = _inproj_b(hm, xc, hf, wi,
                          row(jnp.tile(q_norm[l], N_HEADS)), row(jnp.tile(k_norm[l], N_HEADS)),
                          ones_bd, *gates(1))

  ya = _natt(q, k, v, g2, _bias_columns(rel_pos_bias[l]), _row_window_mask())

  return _ffn2(x1, hs, g1, ya, w_out[l].astype(bf16),
               row(norm_ffn2[l]), w_ffn2_gu[l].astype(bf16), w_ffn2_down[l].astype(bf16))
```
